```python
import math
import jax
import jax.numpy as jnp
from jax import lax
import numpy as np

D_MODEL = 1024
BATCH = 4
SEQ = 4096
DEPTH = 4
DEC_BATCH = 32
DEC_SEQ = 1
PAST_LEN = 8192
PAGE_SIZE = 128

N_AB = (DEPTH + 1) // 2
N_CD = DEPTH // 2
D_MIX = D_MODEL
D_HALF = D_MIX // 2
S5_GROUP = 16
S5_GROUPS = D_HALF // S5_GROUP
S5_STATE = 64
MOBA_HEADS = 8
MOBA_HD = D_HALF // MOBA_HEADS
MOBA_BLOCK = 256
MOBA_TOPK = 3
MOBA_QCHUNK = 32
ROPE_DIM = MOBA_HD // 4
ROPE_THETA = 500000.0
RWKV_HEADS = 8
RWKV_HD = D_HALF // RWKV_HEADS
LORA_W = 64
LORA_A = 64
LORA_V = 32
LORA_G = 128
RWKV_GN_EPS = 64e-5
CONV_D_WIDTH = 31
D_FF = 2816
FFN_CONV_WIDTH = 3
RMS_EPS = 1e-6
LN_EPS = 1e-5
F32 = jnp.float32

kernel_name = 'hybrid_s5_moba_rwkv7_conformer_convffn_step'


def _rmsnorm(x, g):
    xf = x.astype(F32)
    y = xf * lax.rsqrt(jnp.mean(xf * xf, axis=-1, keepdims=True) + RMS_EPS)
    return (y * g.astype(F32)).astype(x.dtype)


def _rope(x, pos):
    half = ROPE_DIM // 2
    inv = 1.0 / (ROPE_THETA ** (jnp.arange(half, dtype=F32) / half))
    ang = pos.astype(F32)[:, None] * inv[None, :]
    cos = jnp.cos(ang)[None, :, None, :]
    sin = jnp.sin(ang)[None, :, None, :]
    xf = x.astype(F32)
    x1, x2 = xf[..., :half], xf[..., half:ROPE_DIM]
    out = jnp.concatenate([x1 * cos - x2 * sin, x2 * cos + x1 * sin, xf[..., ROPE_DIM:]], axis=-1)
    return out.astype(x.dtype)


def _causal_dwconv(x, buf, w, b):
    xp = jnp.concatenate([buf.astype(x.dtype), x], axis=1)
    y = lax.conv_general_dilated(xp, w[:, None, :].astype(x.dtype), window_strides=(1,), padding='VALID',
                                 dimension_numbers=('NWC', 'WIO', 'NWC'), feature_group_count=x.shape[-1])
    return y + b.astype(x.dtype), xp[:, -(w.shape[0] - 1):]


def _s5(u, h_re0, h_im0, a_re, a_im, log_step, b_re, b_im, c_re, c_im, d_skip, w_glu, b_glu):
    bt, L, _ = u.shape
    uf = u.astype(F32).reshape(bt, L, S5_GROUPS, S5_GROUP)
    ar, ai = a_re.astype(F32), a_im.astype(F32)
    dt = jnp.exp(log_step.astype(F32))[:, None]
    mag = jnp.exp(ar * dt)
    abar_r, abar_i = mag * jnp.cos(ai * dt), mag * jnp.sin(ai * dt)
    inv_abs2 = 1.0 / (ar * ar + ai * ai)
    coef_r = ((abar_r - 1.0) * ar + abar_i * ai) * inv_abs2
    coef_i = (abar_i * ar - (abar_r - 1.0) * ai) * inv_abs2
    br, bi = b_re.astype(F32), b_im.astype(F32)
    bbar_r = coef_r[..., None] * br - coef_i[..., None] * bi
    bbar_i = coef_r[..., None] * bi + coef_i[..., None] * br
    bu_r = jnp.einsum('gnc,blgc->blgn', bbar_r, uf)
    bu_i = jnp.einsum('gnc,blgc->blgn', bbar_i, uf)
    h0r, h0i = h_re0.astype(F32), h_im0.astype(F32)
    bu_r = bu_r.at[:, 0].add(abar_r * h0r - abar_i * h0i)
    bu_i = bu_i.at[:, 0].add(abar_r * h0i + abar_i * h0r)

    def combine(e1, e2):
        a1r, a1i, b1r, b1i = e1
        a2r, a2i, b2r, b2i = e2
        return (a1r * a2r - a1i * a2i, a1r * a2i + a1i * a2r,
                a2r * b1r - a2i * b1i + b2r, a2r * b1i + a2i * b1r + b2i)

    a_seq_r = jnp.broadcast_to(abar_r, bu_r.shape)
    a_seq_i = jnp.broadcast_to(abar_i, bu_i.shape)
    _, _, hr, hi = lax.associative_scan(combine, (a_seq_r, a_seq_i, bu_r, bu_i), axis=1)
    y = (jnp.einsum('gcn,blgn->blgc', c_re.astype(F32), hr)
         - jnp.einsum('gcn,blgn->blgc', c_im.astype(F32), hi)
         + d_skip.astype(F32) * uf).reshape(bt, L, D_HALF)
    yg = jax.nn.gelu(y)
    out = yg * jax.nn.sigmoid(yg @ w_glu.astype(F32) + b_glu.astype(F32))
    return out.astype(u.dtype), hr[:, -1], hi[:, -1]


_take_blocks = jax.vmap(jax.vmap(lambda blocks, ids: blocks[ids]))


def _moba(q, k, v, k_past, v_past, pos0):
    bt, L, H, hd = q.shape
    if k_past is not None:
        k = jnp.concatenate([k_past.astype(k.dtype), k], axis=1)
        v = jnp.concatenate([v_past.astype(v.dtype), v], axis=1)
    lk = k.shape[1]
    nb = -(-lk // MOBA_BLOCK)
    pad = nb * MOBA_BLOCK - lk
    k = jnp.pad(k, ((0, 0), (0, pad), (0, 0), (0, 0)))
    v = jnp.pad(v, ((0, 0), (0, pad), (0, 0), (0, 0)))
    kblk = k.transpose(0, 2, 1, 3).reshape(bt, H, nb, MOBA_BLOCK, hd)
    vblk = v.transpose(0, 2, 1, 3).reshape(bt, H, nb, MOBA_BLOCK, hd)
    kmean = jnp.mean(kblk.astype(F32), axis=3)
    k_sel = min(MOBA_TOPK, nb)
    scale = hd ** -0.5

    def attend(args):
        qc, qpos = args
        own = qpos // MOBA_BLOCK
        s = jnp.einsum('bhqd,bhnd->bhqn', qc.astype(F32), kmean)
        s = jnp.where(jnp.arange(nb)[None, :] < own[:, None], s, -jnp.inf)
        _, top = lax.top_k(s, k_sel)
        own_ids = jnp.broadcast_to(own[None, None, :, None], top.shape[:3] + (1,))
        ids = jnp.concatenate([top, own_ids.astype(top.dtype)], axis=-1)
        kg = _take_blocks(kblk, ids)
        vg = _take_blocks(vblk, ids)
        logits = jnp.einsum('bhqd,bhqjtd->bhqjt', qc, kg).astype(F32) * scale
        kpos = ids[..., None] * MOBA_BLOCK + jnp.arange(MOBA_BLOCK)
        slot_ok = jnp.concatenate([jnp.arange(k_sel)[None, :] < own[:, None],
                                   jnp.ones((own.shape[0], 1), dtype=bool)], axis=-1)
        mask = slot_ok[None, None, :, :, None] & (kpos <= qpos[None, None, :, None, None])
        logits = jnp.where(mask, logits, -jnp.inf)
        shp = logits.shape
        p = jax.nn.softmax(logits.reshape(shp[:3] + (shp[3] * shp[4],)), axis=-1).reshape(shp)
        return jnp.einsum('bhqjt,bhqjtd->bhqd', p.astype(vg.dtype), vg)

    qh = q.transpose(0, 2, 1, 3)
    qpos = pos0 + jnp.arange(L, dtype=jnp.int32)
    if L % MOBA_QCHUNK == 0 and L > MOBA_QCHUNK:
        nc = L // MOBA_QCHUNK
        qs = jnp.moveaxis(qh.reshape(bt, H, nc, MOBA_QCHUNK, hd), 2, 0)
        out = lax.map(attend, (qs, qpos.reshape(nc, MOBA_QCHUNK)))
        out = jnp.moveaxis(out, 0, 2).reshape(bt, H, L, hd)
    else:
        out = attend((qh, qpos))
    return out.transpose(0, 2, 1, 3).reshape(bt, L, H * hd)


def _rwkv7(r, k, v, z, shift_prev, wkv0, v_first, p, li):
    bt, L, C = r.shape
    H, N = RWKV_HEADS, RWKV_HD
    cat = jnp.concatenate([r, k, v, z], axis=-1).astype(F32)
    prev = jnp.concatenate([shift_prev.astype(F32)[:, None], cat[:, :-1]], axis=1)
    dlt = prev - cat
    mu = p['rwkv_mu_rkv'][li].astype(F32)
    r_, k_, v_ = (cat[..., j * C:(j + 1) * C] + dlt[..., j * C:(j + 1) * C] * mu[j] for j in range(3))
    z0, dz = cat[..., 3 * C:], dlt[..., 3 * C:]
    mz = p['rwkv_mu_z'][li].astype(F32)
    zw, za, zg = (z0 + dz * mz[j] for j in range(3))
    w = -jax.nn.softplus(-(p['rwkv_w0'][li] + jnp.tanh(zw @ p['rwkv_w1'][li]) @ p['rwkv_w2'][li])) - 0.5
    decay = jnp.exp(-jnp.exp(w.astype(F32)))
    a = jax.nn.sigmoid(p['rwkv_a0'][li] + (za @ p['rwkv_a1'][li]) @ p['rwkv_a2'][li]).astype(F32)
    g = (jax.nn.sigmoid(zg @ p['rwkv_g1'][li]) @ p['rwkv_g2'][li]).astype(F32)
    if li > 0:
        zv = z0 + dz * p['rwkv_mu_v'][li - 1]
        vmix = jax.nn.sigmoid(p['rwkv_v0'][li - 1] + (zv @ p['rwkv_v1'][li - 1]) @ p['rwkv_v2'][li - 1])
        v_ = (v_ + (v_first - v_) * vmix).astype(F32)

    def heads(t):
        return t.reshape(bt, L, H, N)

    kk = heads(k_ * p['rwkv_k_k'][li]).astype(F32)
    kk = kk / jnp.maximum(jnp.linalg.norm(kk, axis=-1, keepdims=True), 1e-12)
    k_ = (k_ * (1.0 + (a - 1.0) * p['rwkv_k_a'][li])).astype(F32)
    rh, wh, kh, vh, ah = heads(r_), heads(decay), heads(k_), heads(v_), heads(a)

    def step(S, inp):
        rt, wt, kt, vt, kkt, at = inp
        sa = jnp.einsum('bhvk,bhk->bhv', S, -kkt)
        S = S * wt[:, :, None, :] + sa[..., None] * (kkt * at)[:, :, None, :] + vt[..., None] * kt[:, :, None, :]
        return S, jnp.einsum('bhvk,bhk->bhv', S, rt)

    xs = tuple(jnp.moveaxis(t, 1, 0) for t in (rh, wh, kh, vh, kk, ah))
    s_fin, o = lax.scan(step, wkv0.astype(F32), xs)
    o = jnp.moveaxis(o, 0, 1)
    mean = jnp.mean(o, axis=-1, keepdims=True)
    var = jnp.mean(jnp.square(o - mean), axis=-1, keepdims=True)
    on = (o - mean) * lax.rsqrt(var + RWKV_GN_EPS)
    on = on * p['rwkv_ln_w'][li].astype(F32).reshape(H, N) + p['rwkv_ln_b'][li].astype(F32).reshape(H, N)
    bonus = jnp.sum(rh * kh * p['rwkv_r_k'][li].astype(F32), axis=-1, keepdims=True) * vh
    out = ((on + bonus).reshape(bt, L, C) * g).astype(r.dtype)
    return out, s_fin, cat[:, -1], v_


def _conformer_conv(gv, gg, buf, w, b, ln_g, ln_b):
    u = gv * jax.nn.sigmoid(gg)
    y, new_buf = _causal_dwconv(u, buf, w, b)
    yf = y.astype(F32)
    m = jnp.mean(yf, axis=-1, keepdims=True)
    var = jnp.mean(jnp.square(yf - m), axis=-1, keepdims=True)
    yn = (yf - m) * lax.rsqrt(var + LN_EPS) * ln_g.astype(F32) + ln_b.astype(F32)
    return jax.nn.silu(yn).astype(gv.dtype), new_buf


def _conv_ffn(h, buf, w_up, cw, cb, w_down):
    u, new_buf = _causal_dwconv(h @ w_up, buf, cw, cb)
    a, b = jnp.split(u, 2, axis=-1)
    return (jax.nn.silu(a) * b) @ w_down, new_buf


def _trunk(x, pos0, paged, s5_re0, s5_im0, wkv0, shift0, convd0, ffn0, p):
    bt, L, _ = x.shape
    pos = pos0 + jnp.arange(L, dtype=jnp.int32)
    ks, vs, s5r, s5i, wkvs, shifts, convs, ffns = [], [], [], [], [], [], [], []
    v_first = None
    for i in range(DEPTH):
        li = i // 2
        h = _rmsnorm(x, p['g_mix'][i])
        if i % 2 == 0:
            u, q, k, v = jnp.split(h @ p['w_in_ab'][li], 4, axis=-1)
            a_out, hr, hi = _s5(u, s5_re0[li], s5_im0[li], p['s5_a_re'][li], p['s5_a_im'][li], p['s5_log_step'][li],
                                p['s5_b_re'][li], p['s5_b_im'][li], p['s5_c_re'][li], p['s5_c_im'][li],
                                p['s5_d'][li], p['s5_w_glu'][li], p['s5_b_glu'][li])
            q = _rope(q.reshape(bt, L, MOBA_HEADS, MOBA_HD), pos)
            k = _rope(k.reshape(bt, L, MOBA_HEADS, MOBA_HD), pos)
            v = v.reshape(bt, L, MOBA_HEADS, MOBA_HD)
            if paged is None:
                k_past, v_past = None, None
            else:
                cache_k, cache_v, page_table = paged
                n_past = page_table.shape[1] * cache_k.shape[2]
                k_past = cache_k[li][page_table].reshape(bt, n_past, MOBA_HEADS, MOBA_HD)
                v_past = cache_v[li][page_table].reshape(bt, n_past, MOBA_HEADS, MOBA_HD)
            b_out = _moba(q, k, v, k_past, v_past, pos0)
            mix = jnp.concatenate([a_out, b_out.astype(a_out.dtype)], axis=-1) @ p['w_out_ab'][li]
            ks.append(k)
            vs.append(v)
            s5r.append(hr)
            s5i.append(hi)
        else:
            r, k, v, z, gv, gg = jnp.split(h @ p['w_in_cd'][li], 6, axis=-1)
            c_out, s_fin, sh, v_c = _rwkv7(r, k, v, z, shift0[li], wkv0[li], v_first, p, li)
            if li == 0:
                v_first = v_c
            d_out, cbuf = _conformer_conv(gv, gg, convd0[li], p['conv_d_w'][li], p['conv_d_b'][li],
                                          p['conv_d_ln_g'][li], p['conv_d_ln_b'][li])
            mix = jnp.concatenate([c_out, d_out], axis=-1) @ p['w_out_cd'][li]
            wkvs.append(s_fin)
            shifts.append(sh)
            convs.append(cbuf)
        x = x + mix.astype(x.dtype)
        f, fbuf = _conv_ffn(_rmsnorm(x, p['g_ffn'][i]), ffn0[i], p['ffn_w_up'][i], p['ffn_conv_w'][i],
                            p['ffn_conv_b'][i], p['ffn_w_down'][i])
        x = x + f.astype(x.dtype)
        ffns.append(fbuf)
    y = _rmsnorm(x, p['g_final'])
    return (y, jnp.stack(ks), jnp.stack(vs), jnp.stack(s5r), jnp.stack(s5i), jnp.stack(wkvs),
            jnp.stack(shifts), jnp.stack(convs), jnp.stack(ffns))


def setup_inputs(seed: int = 0) -> dict:
    key = jax.random.key(seed)
    keys = jax.random.split(key, 96)
    cnt = [0]

    def nk():
        cnt[0] += 1
        return keys[cnt[0] - 1]

    def nrm(shape, scale):
        return jax.random.normal(nk(), shape, F32) * scale

    def gain(shape):
        return 1.0 + 0.02 * jax.random.normal(nk(), shape, F32)

    def unif(shape, lo, hi):
        return jax.random.uniform(nk(), shape, F32, lo, hi)

    D, C, G, N = D_MODEL, D_HALF, S5_GROUPS, S5_STATE
    H, HN = RWKV_HEADS, RWKV_HD
    n_pages = PAST_LEN // PAGE_SIZE
    n_used = DEC_BATCH * n_pages
    n_pool = n_used + n_used // 4
    n_vres = max(N_CD - 1, 0)
    page_table = jax.random.permutation(nk(), n_pool)[:n_used].reshape(DEC_BATCH, n_pages).astype(jnp.int32)
    a_im_init = jnp.broadcast_to(math.pi * jnp.arange(N, dtype=F32), (N_AB, G, N))
    return {
        'x_prompt': nrm((BATCH, SEQ, D), 1.0),
        'x_sample': nrm((DEC_BATCH, DEC_SEQ, D), 1.0),
        'cache_k_moba': nrm((N_AB, n_pool, PAGE_SIZE, MOBA_HEADS, MOBA_HD), 1.0),
        'cache_v_moba': nrm((N_AB, n_pool, PAGE_SIZE, MOBA_HEADS, MOBA_HD), 1.0),
        'page_table': page_table,
        'state_s5_re': nrm((N_AB, DEC_BATCH, G, N), 0.1),
        'state_s5_im': nrm((N_AB, DEC_BATCH, G, N), 0.1),
        'state_rwkv_wkv': nrm((N_CD, DEC_BATCH, H, HN, HN), 0.1),
        'state_rwkv_shift': nrm((N_CD, DEC_BATCH, 4 * C), 1.0),
        'state_conv_d': nrm((N_CD, DEC_BATCH, CONV_D_WIDTH - 1, C), 0.5),
        'state_ffn_conv': nrm((DEPTH, DEC_BATCH, FFN_CONV_WIDTH - 1, 2 * D_FF), 1.0),
        'g_mix': gain((DEPTH, D)),
        'g_ffn': gain((DEPTH, D)),
        'g_final': gain((D,)),
        'w_in_ab': nrm((N_AB, D, 4 * C), D ** -0.5),
        'w_out_ab': nrm((N_AB, D_MIX, D), D_MIX ** -0.5),
        's5_a_re': -0.5 + nrm((N_AB, G, N), 1e-3),
        's5_a_im': a_im_init + nrm((N_AB, G, N), 1e-3),
        's5_log_step': unif((N_AB, G), math.log(1e-3), math.log(1e-1)),
        's5_b_re': nrm((N_AB, G, N, S5_GROUP), (2 * S5_GROUP) ** -0.5),
        's5_b_im': nrm((N_AB, G, N, S5_GROUP), (2 * S5_GROUP) ** -0.5),
        's5_c_re': nrm((N_AB, G, S5_GROUP, N), (2 * N) ** -0.5),
        's5_c_im': nrm((N_AB, G, S5_GROUP, N), (2 * N) ** -0.5),
        's5_d': nrm((N_AB, G, S5_GROUP), 0.5),
        's5_w_glu': nrm((N_AB, C, C), C ** -0.5),
        's5_b_glu': nrm((N_AB, C), 0.01),
        'w_in_cd': nrm((N_CD, D, 6 * C), D ** -0.5),
        'w_out_cd': nrm((N_CD, D_MIX, D), D_MIX ** -0.5),
        'rwkv_mu_rkv': unif((N_CD, 3, C), 0.0, 1.0),
        'rwkv_mu_z': unif((N_CD, 3, C), 0.0, 1.0),
        'rwkv_w0': unif((N_CD, C), -6.0, -1.0),
        'rwkv_w1': nrm((N_CD, C, LORA_W), 0.5 * C ** -0.5),
        'rwkv_w2': nrm((N_CD, LORA_W, C), 0.1),
        'rwkv_a0': nrm((N_CD, C), 0.1),
        'rwkv_a1': nrm((N_CD, C, LORA_A), C ** -0.5),
        'rwkv_a2': nrm((N_CD, LORA_A, C), 0.5 * LORA_A ** -0.5),
        'rwkv_g1': nrm((N_CD, C, LORA_G), C ** -0.5),
        'rwkv_g2': nrm((N_CD, LORA_G, C), LORA_G ** -0.5),
        'rwkv_k_k': 0.85 + nrm((N_CD, C), 0.02),
        'rwkv_k_a': gain((N_CD, C)),
        'rwkv_r_k': nrm((N_CD, H, HN), 0.1),
        'rwkv_ln_w': gain((N_CD, C)),
        'rwkv_ln_b': nrm((N_CD, C), 0.01),
        'rwkv_mu_v': unif((n_vres, C), 0.0, 1.0),
        'rwkv_v0': nrm((n_vres, C), 0.1),
        'rwkv_v1': nrm((n_vres, C, LORA_V), C ** -0.5),
        'rwkv_v2': nrm((n_vres, LORA_V, C), 0.5 * LORA_V ** -0.5),
        'conv_d_w': nrm((N_CD, CONV_D_WIDTH, C), CONV_D_WIDTH ** -0.5),
        'conv_d_b': nrm((N_CD, C), 0.01),
        'conv_d_ln_g': gain((N_CD, C)),
        'conv_d_ln_b': nrm((N_CD, C), 0.01),
        'ffn_w_up': nrm((DEPTH, D, 2 * D_FF), D ** -0.5),
        'ffn_conv_w': nrm((DEPTH, FFN_CONV_WIDTH, 2 * D_FF), FFN_CONV_WIDTH ** -0.5),
        'ffn_conv_b': nrm((DEPTH, 2 * D_FF), 0.01),
        'ffn_w_down': nrm((DEPTH, D_FF, D), D_FF ** -0.5),
    }


def reference(x_prompt, x_sample, cache_k_moba, cache_v_moba, page_table, state_s5_re, state_s5_im,
              state_rwkv_wkv, state_rwkv_shift, state_conv_d, state_ffn_conv, g_mix, g_ffn, g_final,
              w_in_ab, w_out_ab, s5_a_re, s5_a_im, s5_log_step, s5_b_re, s5_b_im, s5_c_re, s5_c_im, s5_d,
              s5_w_glu, s5_b_glu, w_in_cd, w_out_cd, rwkv_mu_rkv, rwkv_mu_z, rwkv_w0, rwkv_w1, rwkv_w2,
              rwkv_a0, rwkv_a1, rwkv_a2, rwkv_g1, rwkv_g2, rwkv_k_k, rwkv_k_a, rwkv_r_k, rwkv_ln_w, rwkv_ln_b,
              rwkv_mu_v, rwkv_v0, rwkv_v1, rwkv_v2, conv_d_w, conv_d_b, conv_d_ln_g, conv_d_ln_b,
              ffn_w_up, ffn_conv_w, ffn_conv_b, ffn_w_down):
    p = dict(g_mix=g_mix, g_ffn=g_ffn, g_final=g_final, w_in_ab=w_in_ab, w_out_ab=w_out_ab,
             s5_a_re=s5_a_re, s5_a_im=s5_a_im, s5_log_step=s5_log_step, s5_b_re=s5_b_re, s5_b_im=s5_b_im,
             s5_c_re=s5_c_re, s5_c_im=s5_c_im, s5_d=s5_d, s5_w_glu=s5_w_glu, s5_b_glu=s5_b_glu,
             w_in_cd=w_in_cd, w_out_cd=w_out_cd, rwkv_mu_rkv=rwkv_mu_rkv, rwkv_mu_z=rwkv_mu_z,
             rwkv_w0=rwkv_w0, rwkv_w1=rwkv_w1, rwkv_w2=rwkv_w2, rwkv_a0=rwkv_a0, rwkv_a1=rwkv_a1,
             rwkv_a2=rwkv_a2, rwkv_g1=rwkv_g1, rwkv_g2=rwkv_g2, rwkv_k_k=rwkv_k_k, rwkv_k_a=rwkv_k_a,
             rwkv_r_k=rwkv_r_k, rwkv_ln_w=rwkv_ln_w, rwkv_ln_b=rwkv_ln_b, rwkv_mu_v=rwkv_mu_v,
             rwkv_v0=rwkv_v0, rwkv_v1=rwkv_v1, rwkv_v2=rwkv_v2, conv_d_w=conv_d_w, conv_d_b=conv_d_b,
             conv_d_ln_g=conv_d_ln_g, conv_d_ln_b=conv_d_ln_b, ffn_w_up=ffn_w_up, ffn_conv_w=ffn_conv_w,
             ffn_conv_b=ffn_conv_b, ffn_w_down=ffn_w_down)
    bp, seq, _ = x_prompt.shape
    dt = x_prompt.dtype
    (y_prompt, k_p, v_p, s5r_p, s5i_p, wkv_p, sh_p, cd_p, ff_p) = _trunk(
        x_prompt, 0, None,
        jnp.zeros((N_AB, bp, S5_GROUPS, S5_STATE), dt), jnp.zeros((N_AB, bp, S5_GROUPS, S5_STATE), dt),
        jnp.zeros((N_CD, bp, RWKV_HEADS, RWKV_HD, RWKV_HD), F32), jnp.zeros((N_CD, bp, 4 * D_HALF), dt),
        jnp.zeros((N_CD, bp, CONV_D_WIDTH - 1, D_HALF), dt),
        jnp.zeros((DEPTH, bp, FFN_CONV_WIDTH - 1, 2 * D_FF), dt), p)
    page = cache_k_moba.shape[2]
    past_len = page_table.shape[1] * page
    (y_sample, k_s, v_s, s5r_s, s5i_s, wkv_s, sh_s, cd_s, ff_s) = _trunk(
        x_sample, past_len, (cache_k_moba, cache_v_moba, page_table), state_s5_re, state_s5_im,
        state_rwkv_wkv, state_rwkv_shift, state_conv_d, state_ffn_conv, p)
    k_p = k_p.reshape(N_AB, bp, seq // page, page, MOBA_HEADS, MOBA_HD)
    v_p = v_p.reshape(N_AB, bp, seq // page, page, MOBA_HEADS, MOBA_HD)
    return (y_prompt, y_sample, k_p, v_p, k_s, v_s, s5r_p, s5i_p, s5r_s, s5i_s,
            wkv_p, wkv_s, sh_p, sh_s, cd_p, cd_s, ff_p, ff_s)
```

```python
import functools
import math

import numpy as np
import jax
import jax.numpy as jnp
from jax import lax
from jax.experimental import pallas as pl
from jax.experimental.pallas import tpu as pltpu

F32 = jnp.float32
BF16 = jnp.bfloat16
HIGHEST = lax.Precision.HIGHEST

D_MODEL = 1024
D_HALF = 512
S5_GROUP = 16
S5_GROUPS = 32
S5_STATE = 64
S5_LANES = S5_GROUPS * S5_STATE
HEADS = 8
HEAD_DIM = 64
MOBA_BLOCK = 256
MOBA_TOPK = 3
PAGE_SIZE = 128
ROPE_DIM = 16
ROPE_THETA = 500000.0
RWKV_GN_EPS = 64e-5
CONV_D_WIDTH = 31
D_FF = 2816
RMS_EPS = 1e-6
LN_EPS = 1e-5
LANE = 128
SUBLANE = 8
VMEM_LIMIT = 48 * 1024 * 1024
NEG = -1e30


def _cp(*sem):
    return pltpu.CompilerParams(dimension_semantics=sem, vmem_limit_bytes=VMEM_LIMIT)


def _bdot(a, b):
    return jnp.dot(a.astype(BF16), b.astype(BF16), preferred_element_type=F32)


def _sigmoid(x):
    return 1.0 / (1.0 + jnp.exp(-x))


def _silu(x):
    return x * _sigmoid(x)


def _row_tile(m, pref):
    return pref if m % pref == 0 else m


def _rope_table_kernel(inv_ref, cos_ref, sa_ref, sb_ref, *, pos0, per_row_pos):
    rows = cos_ref.shape[0]
    lane = lax.broadcasted_iota(jnp.int32, (rows, LANE), 1) % HEAD_DIM
    if per_row_pos:
        pos = (pos0 + lax.broadcasted_iota(jnp.int32, (rows, LANE), 0)).astype(F32)
    else:
        pos = jnp.full((rows, LANE), pos0, F32)
    ang = pos * inv_ref[...]
    c, s = jnp.cos(ang), jnp.sin(ang)
    cos_ref[...] = jnp.where(lane < ROPE_DIM, c, 1.0)
    sa_ref[...] = jnp.where(lane < ROPE_DIM // 2, -s, 0.0)
    sb_ref[...] = jnp.where((lane >= ROPE_DIM // 2) & (lane < ROPE_DIM), s, 0.0)


def _rope_tables(rows, pos0, per_row_pos):
    half = ROPE_DIM // 2
    inv8 = (np.float32(1.0) / (np.float32(ROPE_THETA) ** (np.arange(half, dtype=np.float32) / np.float32(half))))
    inv = np.zeros((HEAD_DIM,), np.float32)
    inv[:half] = inv8
    inv[half:ROPE_DIM] = inv8
    inv = jnp.asarray(np.tile(inv, LANE // HEAD_DIM)[None, :])
    shp = jax.ShapeDtypeStruct((rows, LANE), F32)
    return pl.pallas_call(
        functools.partial(_rope_table_kernel, pos0=pos0, per_row_pos=per_row_pos),
        out_shape=(shp, shp, shp))(inv)


def _norm_matmul_kernel(x_ref, g_ref, w_ref, *rest, rope_cols):
    if rope_cols:
        cos_ref, sa_ref, sb_ref, o_ref, h_ref = rest
    else:
        o_ref, h_ref = rest
    j = pl.program_id(1)

    @pl.when(j == 0)
    def _():
        x = x_ref[...]
        ms = jnp.mean(x * x, axis=-1, keepdims=True)
        h_ref[...] = (x * lax.rsqrt(ms + RMS_EPS) * g_ref[...]).astype(BF16)

    y = jnp.dot(h_ref[...], w_ref[...], preferred_element_type=F32)
    if not rope_cols:
        o_ref[...] = y
        return
    is_rope = functools.reduce(jnp.logical_or, [j == c for c in rope_cols])

    @pl.when(is_rope)
    def _():
        cos, sa, sb = cos_ref[...], sa_ref[...], sb_ref[...]
        for c in range(y.shape[1] // LANE):
            yc = y[:, c * LANE:(c + 1) * LANE]
            o_ref[:, c * LANE:(c + 1) * LANE] = (
                yc * cos + pltpu.roll(yc, LANE - ROPE_DIM // 2, axis=1) * sa
                + pltpu.roll(yc, ROPE_DIM // 2, axis=1) * sb)

    @pl.when(jnp.logical_not(is_rope))
    def _():
        o_ref[...] = y


def _norm_matmul(x, g, w_bf, rope=None, rope_cols=()):
    m, d = x.shape
    n = w_bf.shape[1]
    tm = _row_tile(m, 512)
    tn = D_HALF
    in_specs = [pl.BlockSpec((tm, d), lambda i, j: (i, 0)),
                pl.BlockSpec((1, d), lambda i, j: (0, 0)),
                pl.BlockSpec((d, tn), lambda i, j: (0, j))]
    args = [x, g.reshape(1, d), w_bf]
    if rope_cols:
        nt = rope[0].shape[0] // tm
        for t in rope:
            in_specs.append(pl.BlockSpec((tm, LANE), lambda i, j: (i % nt, 0)))
            args.append(t)
    return pl.pallas_call(
        functools.partial(_norm_matmul_kernel, rope_cols=tuple(rope_cols)),
        grid=(m // tm, n // tn),
        in_specs=in_specs,
        out_specs=pl.BlockSpec((tm, tn), lambda i, j: (i, j)),
        out_shape=jax.ShapeDtypeStruct((m, n), F32),
        scratch_shapes=[pltpu.VMEM((tm, d), BF16)],
        compiler_params=_cp("parallel", "arbitrary"))(*args)


def _out_proj_kernel(a_ref, b_ref, wa_ref, wb_ref, res_ref, o_ref):
    o_ref[...] = res_ref[...] + (_bdot(a_ref[...], wa_ref[...]) + _bdot(b_ref[...], wb_ref[...]))


def _out_proj(a, b, w_bf, res):
    m, c = a.shape
    n = w_bf.shape[1]
    tm = _row_tile(m, 512)
    return pl.pallas_call(
        _out_proj_kernel,
        grid=(m // tm,),
        in_specs=[pl.BlockSpec((tm, c), lambda i: (i, 0)),
                  pl.BlockSpec((tm, c), lambda i: (i, 0)),
                  pl.BlockSpec((c, n), lambda i: (0, 0)),
                  pl.BlockSpec((c, n), lambda i: (1, 0)),
                  pl.BlockSpec((tm, n), lambda i: (i, 0))],
        out_specs=pl.BlockSpec((tm, n), lambda i: (i, 0)),
        out_shape=jax.ShapeDtypeStruct((m, n), F32),
        compiler_params=_cp("parallel"))(a, b, w_bf, w_bf, res)


FFN_TN = 256
FFN_HALO = SUBLANE


def _ffn_seq_kernel(x_ref, xh_ref, g_ref, wa_ref, wb_ref, cwa_ref, cwb_ref, cba_ref, cbb_ref, wd_ref,
                    o_ref, ua_ref, ub_ref, h_ref, acc_ref, *, tiles_per_seq):
    i, c = pl.program_id(0), pl.program_id(1)
    tm = x_ref.shape[0]
    first = (i % tiles_per_seq) == 0

    @pl.when(c == 0)
    def _():
        def norm(x):
            ms = jnp.mean(x * x, axis=-1, keepdims=True)
            return (x * lax.rsqrt(ms + RMS_EPS) * g_ref[...]).astype(BF16)
        h_ref[0:FFN_HALO, :] = norm(xh_ref[...])
        h_ref[FFN_HALO:, :] = norm(x_ref[...])
        acc_ref[...] = jnp.zeros_like(acc_ref)

    h = h_ref[...]
    row = lax.broadcasted_iota(jnp.int32, (tm + FFN_HALO, 1), 0)
    keep = jnp.logical_or(row >= FFN_HALO, jnp.logical_not(first))

    def branch(w_ref, cw_ref, cb_ref, u_ref):
        u = jnp.where(keep, jnp.dot(h, w_ref[...], preferred_element_type=F32), 0.0)
        u_ref[0] = u[tm + FFN_HALO - 2:, :]
        cw = cw_ref[...]
        y = (u * cw[2:3, :] + pltpu.roll(u, 1, axis=0) * cw[1:2, :]
             + pltpu.roll(u, 2, axis=0) * cw[0:1, :])
        return y[FFN_HALO:, :] + cb_ref[...]

    a = branch(wa_ref, cwa_ref, cba_ref, ua_ref)
    b = branch(wb_ref, cwb_ref, cbb_ref, ub_ref)
    acc_ref[...] += _bdot(_silu(a) * b, wd_ref[...])

    @pl.when(c == pl.num_programs(1) - 1)
    def _():
        o_ref[...] = x_ref[...] + acc_ref[...]


def _ffn_seq(x, batch, g, w_up_bf, cw, cb, w_down_bf):
    m, d = x.shape
    seq = m // batch
    tm = _row_tile(seq, 512)
    tps = seq // tm
    nc = D_FF // FFN_TN
    hb = tm // FFN_HALO
    out, ua, ub = pl.pallas_call(
        functools.partial(_ffn_seq_kernel, tiles_per_seq=tps),
        grid=(m // tm, nc),
        in_specs=[pl.BlockSpec((tm, d), lambda i, c: (i, 0)),
                  pl.BlockSpec((FFN_HALO, d), lambda i, c: (jnp.maximum(i * hb - 1, 0), 0)),
                  pl.BlockSpec((1, d), lambda i, c: (0, 0)),
                  pl.BlockSpec((d, FFN_TN), lambda i, c: (0, c)),
                  pl.BlockSpec((d, FFN_TN), lambda i, c: (0, nc + c)),
                  pl.BlockSpec((3, FFN_TN), lambda i, c: (0, c)),
                  pl.BlockSpec((3, FFN_TN), lambda i, c: (0, nc + c)),
                  pl.BlockSpec((1, FFN_TN), lambda i, c: (0, c)),
                  pl.BlockSpec((1, FFN_TN), lambda i, c: (0, nc + c)),
                  pl.BlockSpec((FFN_TN, d), lambda i, c: (c, 0))],
        out_specs=[pl.BlockSpec((tm, d), lambda i, c: (i, 0)),
                   pl.BlockSpec((1, 2, FFN_TN), lambda i, c: (i, 0, c)),
                   pl.BlockSpec((1, 2, FFN_TN), lambda i, c: (i, 0, c))],
        out_shape=[jax.ShapeDtypeStruct((m, d), F32),
                   jax.ShapeDtypeStruct((m // tm, 2, D_FF), F32),
                   jax.ShapeDtypeStruct((m // tm, 2, D_FF), F32)],
        scratch_shapes=[pltpu.VMEM((tm + FFN_HALO, d), BF16), pltpu.VMEM((tm, d), F32)],
        compiler_params=_cp("arbitrary", "arbitrary"),
    )(x, x, g.reshape(1, d), w_up_bf, w_up_bf, cw, cw, cb.reshape(1, -1), cb.reshape(1, -1), w_down_bf)
    return out, jnp.concatenate([ua[tps - 1::tps], ub[tps - 1::tps]], axis=-1)


def _ffn_step_kernel(x_ref, g_ref, wa_ref, wb_ref, cwa_ref, cwb_ref, cba_ref, cbb_ref, wd_ref,
                     bufa_ref, bufb_ref, o_ref, ua_ref, ub_ref, h_ref, acc_ref):
    c = pl.program_id(0)

    @pl.when(c == 0)
    def _():
        x = x_ref[...]
        ms = jnp.mean(x * x, axis=-1, keepdims=True)
        h_ref[...] = (x * lax.rsqrt(ms + RMS_EPS) * g_ref[...]).astype(BF16)
        acc_ref[...] = jnp.zeros_like(acc_ref)

    h = h_ref[...]

    def branch(w_ref, cw_ref, cb_ref, buf_ref, u_ref):
        u = jnp.dot(h, w_ref[...], preferred_element_type=F32)
        u_ref[...] = u
        cw = cw_ref[...]
        return u * cw[2:3, :] + buf_ref[1] * cw[1:2, :] + buf_ref[0] * cw[0:1, :] + cb_ref[...]

    a = branch(wa_ref, cwa_ref, cba_ref, bufa_ref, ua_ref)
    b = branch(wb_ref, cwb_ref, cbb_ref, bufb_ref, ub_ref)
    acc_ref[...] += _bdot(_silu(a) * b, wd_ref[...])

    @pl.when(c == pl.num_programs(0) - 1)
    def _():
        o_ref[...] = x_ref[...] + acc_ref[...]


def _ffn_step(x, buf, g, w_up_bf, cw, cb, w_down_bf):
    m, d = x.shape
    nc = D_FF // FFN_TN
    buf_t = jnp.swapaxes(buf, 0, 1)
    out, ua, ub = pl.pallas_call(
        _ffn_step_kernel,
        grid=(nc,),
        in_specs=[pl.BlockSpec((m, d), lambda c: (0, 0)),
                  pl.BlockSpec((1, d), lambda c: (0, 0)),
                  pl.BlockSpec((d, FFN_TN), lambda c: (0, c)),
                  pl.BlockSpec((d, FFN_TN), lambda c: (0, nc + c)),
                  pl.BlockSpec((3, FFN_TN), lambda c: (0, c)),
                  pl.BlockSpec((3, FFN_TN), lambda c: (0, nc + c)),
                  pl.BlockSpec((1, FFN_TN), lambda c: (0, c)),
                  pl.BlockSpec((1, FFN_TN), lambda c: (0, nc + c)),
                  pl.BlockSpec((FFN_TN, d), lambda c: (c, 0)),
                  pl.BlockSpec((2, m, FFN_TN), lambda c: (0, 0, c)),
                  pl.BlockSpec((2, m, FFN_TN), lambda c: (0, 0, nc + c))],
        out_specs=[pl.BlockSpec((m, d), lambda c: (0, 0)),
                   pl.BlockSpec((m, FFN_TN), lambda c: (0, c)),
                   pl.BlockSpec((m, FFN_TN), lambda c: (0, c))],
        out_shape=[jax.ShapeDtypeStruct((m, d), F32),
                   jax.ShapeDtypeStruct((m, D_FF), F32),
                   jax.ShapeDtypeStruct((m, D_FF), F32)],
        scratch_shapes=[pltpu.VMEM((m, d), BF16), pltpu.VMEM((m, d), F32)],
        compiler_params=_cp("arbitrary"),
    )(x, g.reshape(1, d), w_up_bf, w_up_bf, cw, cw, cb.reshape(1, -1), cb.reshape(1, -1), w_down_bf,
      buf_t, buf_t)
    u = jnp.concatenate([ua, ub], axis=-1)
    return out, jnp.stack([buf[:, 1], u], axis=1)


S5_STRIP = 512
S5_NSTRIP = S5_LANES // S5_STRIP


def _s5_param_kernel(are_ref, aim_ref, ls_ref, brt_ref, bit_ref, abr_ref, abi_ref, bbr_ref, bbi_ref):
    ar, ai = are_ref[...], aim_ref[...]
    dt = jnp.exp(ls_ref[...])
    mag = jnp.exp(ar * dt)
    abr, abi = mag * jnp.cos(ai * dt), mag * jnp.sin(ai * dt)
    inv_abs2 = 1.0 / (ar * ar + ai * ai)
    cr = ((abr - 1.0) * ar + abi * ai) * inv_abs2
    ci = (abi * ar - (abr - 1.0) * ai) * inv_abs2
    br, bi = brt_ref[...], bit_ref[...]
    abr_ref[...] = abr
    abi_ref[...] = abi
    bbr_ref[...] = cr * br - ci * bi
    bbi_ref[...] = cr * bi + ci * br


def _s5_params(a_re, a_im, log_step, b_re, b_im, c_re, c_im, d_skip):
    g, n, k = b_re.shape
    rep = lambda t: jnp.repeat(t, k, axis=0)
    brt = b_re.transpose(0, 2, 1).reshape(g * k, n)
    bit = b_im.transpose(0, 2, 1).reshape(g * k, n)
    shp = jax.ShapeDtypeStruct((g * k, n), F32)
    abr, abi, bbr, bbi = pl.pallas_call(_s5_param_kernel, out_shape=(shp, shp, shp, shp))(
        rep(a_re), rep(a_im), rep(log_step[:, None]), brt, bit)
    abr = abr[::k].reshape(1, g * n)
    abi = abi[::k].reshape(1, g * n)
    gs = S5_STRIP // n
    eye = jnp.eye(gs, dtype=F32)

    def in_blocks(t):
        t = t.reshape(S5_NSTRIP, gs, k, n)
        return jnp.einsum('cgkn,gh->cgkhn', t, eye).reshape(S5_NSTRIP, gs * k, gs * n).astype(BF16)

    def out_blocks(t):
        t = t.reshape(S5_NSTRIP, gs, k, n)
        return jnp.einsum('cgkn,gh->cgnhk', t, eye).reshape(S5_NSTRIP, gs * n, gs * k).astype(BF16)

    return dict(abr=abr, abi=abi, wbr=in_blocks(bbr), wbi=in_blocks(bbi),
                wcr=out_blocks(c_re), wci=out_blocks(c_im), d=d_skip.reshape(1, g * k))


def _s5_input(u, wbr_ref, wbi_ref, c):
    cw = S5_STRIP // (S5_STATE // S5_GROUP)
    uc = u[:, c * cw:(c + 1) * cw].astype(BF16)
    return (jnp.dot(uc, wbr_ref[c], preferred_element_type=F32),
            jnp.dot(uc, wbi_ref[c], preferred_element_type=F32))


def _s5_output(u, h_strip, wcr_ref, wci_ref, d_ref, wg_ref, bg_ref):
    ys = []
    for c in range(S5_NSTRIP):
        hr, hi = h_strip(c)
        ys.append(jnp.dot(hr.astype(BF16), wcr_ref[c], preferred_element_type=F32)
                  - jnp.dot(hi.astype(BF16), wci_ref[c], preferred_element_type=F32))
    y = jnp.concatenate(ys, axis=-1) + d_ref[...] * u
    yg = 0.5 * y * (1.0 + jnp.tanh(math.sqrt(2.0 / math.pi) * (y + 0.044715 * (y * y * y))))
    z = jnp.dot(yg.astype(BF16), wg_ref[...], preferred_element_type=F32) + bg_ref[...]
    return yg * _sigmoid(z)


def _cmul(ar, ai, br, bi):
    return ar * br - ai * bi, ar * bi + ai * br


def _s5_seq_kernel(u_ref, abr_ref, abi_ref, wbr_ref, wbi_ref, wcr_ref, wci_ref, d_ref, wg_ref, bg_ref,
                   o_ref, hro_ref, hio_ref, hr_ref, hi_ref, cr_ref, ci_ref, tab_ref):
    i = pl.program_id(1)
    tt = u_ref.shape[0]

    @pl.when(i == 0)
    def _():
        cr_ref[...] = jnp.zeros_like(cr_ref)
        ci_ref[...] = jnp.zeros_like(ci_ref)
        row = lax.broadcasted_iota(jnp.int32, (SUBLANE, S5_LANES), 0)
        p_r, p_i = [abr_ref[...]], [abi_ref[...]]
        for _ in range(SUBLANE - 1):
            nr, ni = _cmul(p_r[-1], p_i[-1], p_r[0], p_i[0])
            p_r.append(nr)
            p_i.append(ni)
        for k, s in enumerate((1, 2, 4)):
            tab_ref[2 * k] = jnp.where(row >= s, p_r[s - 1], 0.0)
            tab_ref[2 * k + 1] = jnp.where(row >= s, p_i[s - 1], 0.0)
        car_r = jnp.zeros((SUBLANE, S5_LANES), F32)
        car_i = jnp.zeros((SUBLANE, S5_LANES), F32)
        for j in range(SUBLANE):
            car_r = jnp.where(row == j, p_r[j], car_r)
            car_i = jnp.where(row == j, p_i[j], car_i)
        tab_ref[6] = car_r
        tab_ref[7] = car_i

    u = u_ref[...]
    for c in range(S5_NSTRIP):
        br, bi = _s5_input(u, wbr_ref, wbi_ref, c)
        hr_ref[:, c * S5_STRIP:(c + 1) * S5_STRIP] = br
        hi_ref[:, c * S5_STRIP:(c + 1) * S5_STRIP] = bi

    def block(r, carry):
        r8 = pl.multiple_of(r * SUBLANE, SUBLANE)
        for c in range(S5_NSTRIP):
            sl = slice(c * S5_STRIP, (c + 1) * S5_STRIP)
            xr, xi = hr_ref[pl.ds(r8, SUBLANE), sl], hi_ref[pl.ds(r8, SUBLANE), sl]
            for k, s in enumerate((1, 2, 4)):
                dr, di = _cmul(tab_ref[2 * k, :, sl], tab_ref[2 * k + 1, :, sl],
                               pltpu.roll(xr, s, axis=0), pltpu.roll(xi, s, axis=0))
                xr, xi = xr + dr, xi + di
            dr, di = _cmul(tab_ref[6, :, sl], tab_ref[7, :, sl], cr_ref[:, sl], ci_ref[:, sl])
            xr, xi = xr + dr, xi + di
            hr_ref[pl.ds(r8, SUBLANE), sl] = xr
            hi_ref[pl.ds(r8, SUBLANE), sl] = xi
            cr_ref[:, sl] = xr[SUBLANE - 1:, :]
            ci_ref[:, sl] = xi[SUBLANE - 1:, :]
        return carry

    lax.fori_loop(0, tt // SUBLANE, block, 0)

    def h_strip(c):
        sl = slice(c * S5_STRIP, (c + 1) * S5_STRIP)
        return hr_ref[:, sl], hi_ref[:, sl]

    o_ref[...] = _s5_output(u, h_strip, wcr_ref, wci_ref, d_ref, wg_ref, bg_ref)
    hro_ref[0] = cr_ref[...]
    hio_ref[0] = ci_ref[...]


def _full(shape):
    nd = len(shape)
    return pl.BlockSpec(shape, lambda *_: (0,) * nd)


def _s5_seq(proj, batch, sp, w_glu_bf, b_glu):
    m = proj.shape[0]
    seq = m // batch
    tt = _row_tile(seq, 256)
    nt = seq // tt
    params = [sp['abr'], sp['abi'], sp['wbr'], sp['wbi'], sp['wcr'], sp['wci'], sp['d'],
              w_glu_bf, b_glu.reshape(1, -1)]
    out, hr, hi = pl.pallas_call(
        _s5_seq_kernel,
        grid=(batch, nt),
        in_specs=[pl.BlockSpec((tt, D_HALF), lambda b, i: (b * nt + i, 0))] + [_full(p.shape) for p in params],
        out_specs=[pl.BlockSpec((tt, D_HALF), lambda b, i: (b * nt + i, 0)),
                   pl.BlockSpec((1, 1, S5_LANES), lambda b, i: (b, 0, 0)),
                   pl.BlockSpec((1, 1, S5_LANES), lambda b, i: (b, 0, 0))],
        out_shape=[jax.ShapeDtypeStruct((m, D_HALF), F32),
                   jax.ShapeDtypeStruct((batch, 1, S5_LANES), F32),
                   jax.ShapeDtypeStruct((batch, 1, S5_LANES), F32)],
        scratch_shapes=[pltpu.VMEM((tt, S5_LANES), F32), pltpu.VMEM((tt, S5_LANES), F32),
                        pltpu.VMEM((1, S5_LANES), F32), pltpu.VMEM((1, S5_LANES), F32),
                        pltpu.VMEM((8, SUBLANE, S5_LANES), F32)],
        compiler_params=_cp("arbitrary", "arbitrary"))(proj, *params)
    return out, hr[:, 0], hi[:, 0]


def _s5_step_kernel(u_ref, h0r_ref, h0i_ref, abr_ref, abi_ref, wbr_ref, wbi_ref, wcr_ref, wci_ref, d_ref,
                    wg_ref, bg_ref, o_ref, hro_ref, hio_ref):
    u = u_ref[...]
    for c in range(S5_NSTRIP):
        sl = slice(c * S5_STRIP, (c + 1) * S5_STRIP)
        br, bi = _s5_input(u, wbr_ref, wbi_ref, c)
        dr, di = _cmul(abr_ref[:, sl], abi_ref[:, sl], h0r_ref[:, sl], h0i_ref[:, sl])
        hro_ref[:, sl] = dr + br
        hio_ref[:, sl] = di + bi

    def h_strip(c):
        sl = slice(c * S5_STRIP, (c + 1) * S5_STRIP)
        return hro_ref[:, sl], hio_ref[:, sl]

    o_ref[...] = _s5_output(u, h_strip, wcr_ref, wci_ref, d_ref, wg_ref, bg_ref)


def _s5_step(proj, h0r, h0i, sp, w_glu_bf, b_glu):
    m = proj.shape[0]
    args = [h0r, h0i, sp['abr'], sp['abi'], sp['wbr'], sp['wbi'], sp['wcr'], sp['wci'], sp['d'],
            w_glu_bf, b_glu.reshape(1, -1)]
    return pl.pallas_call(
        _s5_step_kernel,
        grid=(1,),
        in_specs=[pl.BlockSpec((m, D_HALF), lambda i: (0, 0))] + [_full(a.shape) for a in args],
        out_specs=[_full((m, D_HALF)), _full((m, S5_LANES)), _full((m, S5_LANES))],
        out_shape=[jax.ShapeDtypeStruct((m, D_HALF), F32),
                   jax.ShapeDtypeStruct((m, S5_LANES), F32),
                   jax.ShapeDtypeStruct((m, S5_LANES), F32)],
        compiler_params=_cp("arbitrary"))(proj, *args)


HEADS_PER_TILE = LANE // HEAD_DIM
HEAD_TILES = HEADS // HEADS_PER_TILE
ATTN_SCALE = HEAD_DIM ** -0.5
NT_DIMS = (((1,), (1,)), ((), ()))


def _moba_seq_kernel(q_ref, k_ref, v_ref, o_ref, kbf_ref, vbf_ref, kmean_ref, *, nb):
    qi = pl.program_id(2)
    blk = MOBA_BLOCK

    @pl.when(qi == 0)
    def _():
        kbf_ref[...] = k_ref[...].astype(BF16)
        vbf_ref[...] = v_ref[...].astype(BF16)
        for j in range(nb):
            kmean_ref[j:j + 1, :] = jnp.mean(k_ref[j * blk:(j + 1) * blk, :], axis=0, keepdims=True)

    q = q_ref[...]
    col = lax.broadcasted_iota(jnp.int32, (blk, nb), 1)
    causal = (lax.broadcasted_iota(jnp.int32, (blk, blk), 0) >= lax.broadcasted_iota(jnp.int32, (blk, blk), 1))
    q0 = pl.multiple_of(qi * blk, blk)
    for h in range(HEADS_PER_TILE):
        sl = slice(h * HEAD_DIM, (h + 1) * HEAD_DIM)
        qh = q[:, sl]
        s = lax.dot_general(qh, kmean_ref[:, sl], NT_DIMS, precision=HIGHEST, preferred_element_type=F32)
        rank = jnp.zeros((blk, nb), F32)
        for i in range(nb):
            si = s[:, i:i + 1]
            beats = jnp.logical_or(si > s, jnp.logical_and(si == s, i < col))
            rank = rank + jnp.where(jnp.logical_and(beats, i < qi), 1.0, 0.0)
        sel = jnp.where(jnp.logical_and(col < qi, rank < MOBA_TOPK), 1.0, 0.0)

        qb = (qh * ATTN_SCALE).astype(BF16)
        lg = lax.dot_general(qb, kbf_ref[pl.ds(q0, blk), sl], NT_DIMS, preferred_element_type=F32)
        lg = jnp.where(causal, lg, NEG)
        m = jnp.max(lg, axis=-1, keepdims=True)
        p = jnp.exp(lg - m)
        l = jnp.sum(p, axis=-1, keepdims=True)
        acc = jnp.dot(p.astype(BF16), vbf_ref[pl.ds(q0, blk), sl], preferred_element_type=F32)

        def body(j, carry):
            m, l, acc = carry
            j0 = pl.multiple_of(j * blk, blk)
            lg = lax.dot_general(qb, kbf_ref[pl.ds(j0, blk), sl], NT_DIMS, preferred_element_type=F32)
            sel_j = jnp.sum(jnp.where(col == j, sel, 0.0), axis=-1, keepdims=True)
            lg = jnp.where(sel_j > 0.0, lg, NEG)
            m_new = jnp.maximum(m, jnp.max(lg, axis=-1, keepdims=True))
            alpha = jnp.exp(m - m_new)
            p = jnp.exp(lg - m_new)
            l = alpha * l + jnp.sum(p, axis=-1, keepdims=True)
            acc = alpha * acc + jnp.dot(p.astype(BF16), vbf_ref[pl.ds(j0, blk), sl],
                                        preferred_element_type=F32)
            return m_new, l, acc

        m, l, acc = lax.fori_loop(0, qi, body, (m, l, acc))
        o_ref[:, sl] = acc / l


def _moba_seq(proj, batch):
    m = proj.shape[0]
    seq = m // batch
    assert seq % MOBA_BLOCK == 0
    nb = seq // MOBA_BLOCK
    qoff, koff, voff = HEAD_TILES, 2 * HEAD_TILES, 3 * HEAD_TILES
    return pl.pallas_call(
        functools.partial(_moba_seq_kernel, nb=nb),
        grid=(batch, HEAD_TILES, nb),
        in_specs=[pl.BlockSpec((MOBA_BLOCK, LANE), lambda b, t, i: (b * nb + i, qoff + t)),
                  pl.BlockSpec((seq, LANE), lambda b, t, i: (b, koff + t)),
                  pl.BlockSpec((seq, LANE), lambda b, t, i: (b, voff + t))],
        out_specs=pl.BlockSpec((MOBA_BLOCK, LANE), lambda b, t, i: (b * nb + i, t)),
        out_shape=jax.ShapeDtypeStruct((m, D_HALF), F32),
        scratch_shapes=[pltpu.VMEM((seq, LANE), BF16), pltpu.VMEM((seq, LANE), BF16),
                        pltpu.VMEM((nb, LANE), F32)],
        compiler_params=_cp("parallel", "parallel", "arbitrary"))(proj, proj, proj)


MOBA_PAGES_PER_STEP = 16
PAGES_PER_BLOCK = MOBA_BLOCK // PAGE_SIZE


def _moba_rank_kernel(pt_ref, q_ref, *rest, nblk):
    k_refs = rest[:MOBA_PAGES_PER_STEP]
    idx_ref, kmean_ref = rest[MOBA_PAGES_PER_STEP:]
    j = pl.program_id(1)
    bps = MOBA_PAGES_PER_STEP // PAGES_PER_BLOCK
    for t in range(bps):
        tot = sum(jnp.sum(k_refs[t * PAGES_PER_BLOCK + p][...], axis=0, keepdims=True)
                  for p in range(PAGES_PER_BLOCK))
        kmean_ref[pl.ds(j * bps + t, 1), :] = tot * (1.0 / MOBA_BLOCK)

    @pl.when(j == pl.num_programs(1) - 1)
    def _():
        prod = kmean_ref[...] * q_ref[0]
        head_of_lane = lax.broadcasted_iota(jnp.int32, (D_HALF, LANE), 0) // HEAD_DIM
        onehot = jnp.where(head_of_lane == lax.broadcasted_iota(jnp.int32, (D_HALF, LANE), 1), 1.0, 0.0)
        s = jnp.dot(prod, onehot, precision=HIGHEST, preferred_element_type=F32)
        row = lax.broadcasted_iota(jnp.int32, (nblk, LANE), 0)
        rank = jnp.zeros((nblk, LANE), jnp.int32)
        for i in range(nblk):
            si = s[i:i + 1, :]
            beats = jnp.logical_or(si > s, jnp.logical_and(si == s, i < row))
            rank = rank + beats.astype(jnp.int32)
        out_row = lax.broadcasted_iota(jnp.int32, (SUBLANE, LANE), 0)
        out = jnp.zeros((SUBLANE, LANE), jnp.int32)
        for slot in range(MOBA_TOPK):
            idx = jnp.sum(jnp.where(rank == slot, row, 0), axis=0, keepdims=True)
            out = jnp.where(out_row == slot, idx, out)
        idx_ref[0] = out


def _moba_pick_kernel(pg_ref, q_ref, kn_ref, vn_ref, *rest):
    n = HEADS_PER_TILE * MOBA_TOPK * PAGES_PER_BLOCK
    k_refs, v_refs, o_ref = rest[:n], rest[n:2 * n], rest[2 * n]
    q, kn, vn = q_ref[0], kn_ref[0], vn_ref[0]
    head_of_lane = lax.broadcasted_iota(jnp.int32, (1, LANE), 1) // HEAD_DIM
    out = jnp.zeros((1, LANE), F32)
    per_head = MOBA_TOPK * PAGES_PER_BLOCK
    for h in range(HEADS_PER_TILE):
        mine = head_of_lane == h
        qm = jnp.where(mine, q * ATTN_SCALE, 0.0)
        lg_self = jnp.sum(qm * kn, axis=-1, keepdims=True)
        lgs = [jnp.sum(k_refs[h * per_head + t][...] * qm, axis=-1, keepdims=True) for t in range(per_head)]
        m = lg_self
        for lg in lgs:
            m = jnp.maximum(m, jnp.max(lg, axis=0, keepdims=True))
        p_self = jnp.exp(lg_self - m)
        l = p_self
        acc = p_self * vn
        for t, lg in enumerate(lgs):
            p = jnp.exp(lg - m)
            l = l + jnp.sum(p, axis=0, keepdims=True)
            acc = acc + jnp.sum(p * v_refs[h * per_head + t][...], axis=0, keepdims=True)
        out = jnp.where(mine, acc / l, out)
    o_ref[0] = out


def _moba_step(proj, cache_k, cache_v, page_table):
    bsz = proj.shape[0]
    n_pages = page_table.shape[1]
    assert n_pages % MOBA_PAGES_PER_STEP == 0
    nblk = n_pages // PAGES_PER_BLOCK
    steps = n_pages // MOBA_PAGES_PER_STEP
    proj3 = proj.reshape(bsz, 1, -1)
    qoff, koff, voff = HEAD_TILES, 2 * HEAD_TILES, 3 * HEAD_TILES

    def page_spec(t):
        return pl.BlockSpec((None, PAGE_SIZE, D_HALF),
                            lambda b, j, pt: (pt[b, j * MOBA_PAGES_PER_STEP + t], 0, 0))

    idx = pl.pallas_call(
        functools.partial(_moba_rank_kernel, nblk=nblk),
        grid_spec=pltpu.PrefetchScalarGridSpec(
            num_scalar_prefetch=1,
            grid=(bsz, steps),
            in_specs=[pl.BlockSpec((1, 1, D_HALF), lambda b, j, pt: (b, 0, 1))]
            + [page_spec(t) for t in range(MOBA_PAGES_PER_STEP)],
            out_specs=pl.BlockSpec((1, SUBLANE, LANE), lambda b, j, pt: (b, 0, 0)),
            scratch_shapes=[pltpu.VMEM((nblk, D_HALF), F32)]),
        out_shape=jax.ShapeDtypeStruct((bsz, SUBLANE, LANE), jnp.int32),
        compiler_params=_cp("arbitrary", "arbitrary"),
    )(page_table, proj3, *([cache_k] * MOBA_PAGES_PER_STEP))

    top = idx[:, :MOBA_TOPK, :HEADS].transpose(0, 2, 1)
    logical = top[..., None] * PAGES_PER_BLOCK + jnp.arange(PAGES_PER_BLOCK, dtype=jnp.int32)
    pages = jnp.take_along_axis(page_table, logical.reshape(bsz, -1), axis=1).reshape(-1)
    per_tile = HEADS_PER_TILE * MOBA_TOPK * PAGES_PER_BLOCK

    def pick_spec(t):
        return pl.BlockSpec((None, PAGE_SIZE, LANE),
                            lambda b, ht, pg: (pg[(b * HEAD_TILES + ht) * per_tile + t], 0, ht))

    def tok_spec(off):
        return pl.BlockSpec((1, 1, LANE), lambda b, ht, pg: (b, 0, off + ht))

    out = pl.pallas_call(
        _moba_pick_kernel,
        grid_spec=pltpu.PrefetchScalarGridSpec(
            num_scalar_prefetch=1,
            grid=(bsz, HEAD_TILES),
            in_specs=[tok_spec(qoff), tok_spec(koff), tok_spec(voff)]
            + [pick_spec(t) for t in range(per_tile)] * 2,
            out_specs=pl.BlockSpec((1, 1, LANE), lambda b, ht, pg: (b, 0, ht))),
        out_shape=jax.ShapeDtypeStruct((bsz, 1, D_HALF), F32),
        compiler_params=_cp("arbitrary", "arbitrary"),
    )(pages, proj3, proj3, proj3, *([cache_k] * per_tile), *([cache_v] * per_tile))
    return out.reshape(bsz, D_HALF)


def _seg_sum(x):
    seg = lax.broadcasted_iota(jnp.int32, (LANE, LANE), 0) // HEAD_DIM
    ones = jnp.where(seg == lax.broadcasted_iota(jnp.int32, (LANE, LANE), 1) // HEAD_DIM, 1.0, 0.0)
    parts = [jnp.dot(x[:, c * LANE:(c + 1) * LANE], ones, precision=HIGHEST, preferred_element_type=F32)
             for c in range(x.shape[1] // LANE)]
    return jnp.concatenate(parts, axis=-1)


def _softplus(x):
    return jnp.maximum(x, 0.0) + jnp.log(1.0 + jnp.exp(-jnp.abs(x)))


def _rwkv_mix(cat, prev, p, vfirst):
    c = D_HALF
    dlt = prev - cat
    part = lambda t, j: t[:, j * c:(j + 1) * c]
    mu, mz = p['mu'][...], p['mz'][...]
    r = part(cat, 0) + part(dlt, 0) * mu[0:1]
    k = part(cat, 1) + part(dlt, 1) * mu[1:2]
    v = part(cat, 2) + part(dlt, 2) * mu[2:3]
    z0, dz = part(cat, 3), part(dlt, 3)
    zw, za, zg = z0 + dz * mz[0:1], z0 + dz * mz[1:2], z0 + dz * mz[2:3]
    w_raw = -_softplus(-(p['w0'][...] + _bdot(jnp.tanh(_bdot(zw, p['w1'][...])), p['w2'][...]))) - 0.5
    log_decay = -jnp.exp(w_raw)
    a = _sigmoid(p['a0'][...] + _bdot(_bdot(za, p['a1'][...]), p['a2'][...]))
    g = _bdot(_sigmoid(_bdot(zg, p['g1'][...])), p['g2'][...])
    if vfirst is not None:
        zv = z0 + dz * p['mu_v'][...]
        vmix = _sigmoid(p['v0'][...] + _bdot(_bdot(zv, p['v1'][...]), p['v2'][...]))
        v = v + (vfirst - v) * vmix
    kk = k * p['k_k'][...]
    kk = kk / jnp.maximum(jnp.sqrt(_seg_sum(kk * kk)), 1e-12)
    k = k * (1.0 + (a - 1.0) * p['k_a'][...])
    return r, log_decay, k, v, kk, kk * a, g


RWKV_PARAM_NAMES = ('mu', 'mz', 'w0', 'w1', 'w2', 'a0', 'a1', 'a2', 'g1', 'g2', 'k_k', 'k_a')
RWKV_VRES_NAMES = ('mu_v', 'v0', 'v1', 'v2')
RWKV_N_OUT = 7


def _rwkv_pre_kernel(*refs, names, has_vfirst, seq_tiles):
    refs = list(refs)
    cat_ref = refs.pop(0)
    prev_ref = refs.pop(0)
    vf_ref = refs.pop(0) if has_vfirst else None
    p = {n: refs.pop(0) for n in names}
    outs = refs
    cat = cat_ref[...]
    if seq_tiles:
        first = (pl.program_id(0) % seq_tiles) == 0
        row = lax.broadcasted_iota(jnp.int32, (cat.shape[0], 1), 0)
        before = jnp.where(first, 0.0, prev_ref[SUBLANE - 1:SUBLANE, :])
        prev = jnp.where(row == 0, before, pltpu.roll(cat, 1, axis=0))
    else:
        prev = prev_ref[...]
    res = _rwkv_mix(cat, prev, p, None if vf_ref is None else vf_ref[...])
    for o_ref, val in zip(outs, res):
        o_ref[...] = val


def _rwkv_pre(proj, prev, rp, vfirst, batch):
    m = proj.shape[0]
    cw = 4 * D_HALF
    names = RWKV_PARAM_NAMES + (RWKV_VRES_NAMES if vfirst is not None else ())
    params = [rp[n] for n in names]
    if prev is None:
        seq = m // batch
        tm = _row_tile(seq, 256)
        seq_tiles = seq // tm
        hb = tm // SUBLANE
        prev_arg = proj
        prev_spec = pl.BlockSpec((SUBLANE, cw), lambda i: (jnp.maximum(i * hb - 1, 0), 0))
    else:
        tm, seq_tiles = m, 0
        prev_arg = prev
        prev_spec = pl.BlockSpec((tm, cw), lambda i: (i, 0))
    args = [proj, prev_arg]
    in_specs = [pl.BlockSpec((tm, cw), lambda i: (i, 0)), prev_spec]
    if vfirst is not None:
        args.append(vfirst)
        in_specs.append(pl.BlockSpec((tm, D_HALF), lambda i: (i, 0)))
    args += params
    in_specs += [_full(a.shape) for a in params]
    return pl.pallas_call(
        functools.partial(_rwkv_pre_kernel, names=names, has_vfirst=vfirst is not None, seq_tiles=seq_tiles),
        grid=(m // tm,),
        in_specs=in_specs,
        out_specs=[pl.BlockSpec((tm, D_HALF), lambda i: (i, 0))] * RWKV_N_OUT,
        out_shape=[jax.ShapeDtypeStruct((m, D_HALF), F32)] * RWKV_N_OUT,
        compiler_params=_cp("parallel"))(*args)


WKV_CHUNK = 64
WKV_SUB = 16


def _dot3(a, b):
    ah = a.astype(BF16)
    al = (a - ah.astype(F32)).astype(BF16)
    bh = b.astype(BF16)
    bl = (b - bh.astype(F32)).astype(BF16)
    d = lambda x, y: jnp.dot(x, y, preferred_element_type=F32)
    return d(ah, bh) + (d(ah, bl) + d(al, bh))


def _wkv_chunk(r, lw, k, v, kk, b, st, mm=_bdot, mm_state=_dot3, mm_inv=_dot3):
    c, n = r.shape
    ri = lax.broadcasted_iota(jnp.int32, (c, c), 0)
    ci = lax.broadcasted_iota(jnp.int32, (c, c), 1)
    incl, strict = ri >= ci, ri > ci
    same_sub = (ri // WKV_SUB) == (ci // WKV_SUB)
    eye = jnp.where(ri == ci, 1.0, 0.0)
    cum = jnp.dot(jnp.where(incl, 1.0, 0.0), lw, precision=HIGHEST, preferred_element_type=F32)
    tot = cum[c - 1:c, :]
    e_neg = jnp.exp(-cum)
    e_end = jnp.exp(tot - cum)
    kkm = kk * jnp.exp(cum - lw)
    rp = r * jnp.exp(cum)
    qa = lax.dot_general(jnp.concatenate([kkm, rp], axis=0).astype(BF16),
                         jnp.concatenate([b * e_neg, k * e_neg], axis=0).astype(BF16),
                         NT_DIMS, preferred_element_type=F32)
    a_ub = jnp.where(strict, qa[:c, :c], 0.0)
    a_vk = jnp.where(strict, qa[:c, c:], 0.0)
    a_rb = jnp.where(incl, qa[c:, :c], 0.0)
    a_rk = jnp.where(incl, qa[c:, c:], 0.0)
    a_d = jnp.where(same_sub, a_ub, 0.0)
    a_o = a_ub - a_d
    a2 = mm_inv(a_d, a_d)
    a4 = mm_inv(a2, a2)
    a8 = mm_inv(a4, a4)
    t_d = mm_inv(mm_inv(mm_inv(eye - a_d, eye + a2), eye + a4), eye + a8)
    nn = mm_inv(t_d, a_o)
    t_inv = mm_inv(mm_inv(eye - nn, eye + mm_inv(nn, nn)), t_d)
    av = mm(jnp.concatenate([a_vk, a_rk], axis=0), v)
    x1 = mm(t_inv, jnp.concatenate([kkm, av[:c]], axis=1))
    x2 = mm(a_rb, x1)
    x3 = mm((b * e_end).T, x1)
    rr = rp - x2[:, :n]
    o_loc = av[c:] - x2[:, n:]
    trans = eye[:n, :n] * jnp.exp(tot) - x3[:, :n]
    s_loc = mm((k * e_end).T, v) - x3[:, n:]
    o = mm(rr, st) + o_loc
    return o, mm_state(trans, st) + s_loc


WKV_TIME_TILE = 512


def _wkv_seq_kernel(r_ref, lw_ref, k_ref, v_ref, kk_ref, b_ref, g_ref, rk_ref, lnw_ref, lnb_ref,
                    o_ref, so_ref, st_ref):
    i = pl.program_id(2)

    @pl.when(i == 0)
    def _():
        st_ref[...] = jnp.zeros_like(st_ref)

    def chunk(ci, carry):
        rows = pl.ds(pl.multiple_of(ci * WKV_CHUNK, WKV_CHUNK), WKV_CHUNK)
        tiles = [ref[rows, :] for ref in (r_ref, lw_ref, k_ref, v_ref, kk_ref, b_ref, g_ref)]
        outs = []
        for h in range(HEADS_PER_TILE):
            sl = slice(h * HEAD_DIM, (h + 1) * HEAD_DIM)
            r, lw, k, v, kk, b, g = [t[:, sl] for t in tiles]
            o, st = _wkv_chunk(r, lw, k, v, kk, b, st_ref[h])
            st_ref[h] = st
            mean = jnp.mean(o, axis=-1, keepdims=True)
            var = jnp.mean(jnp.square(o - mean), axis=-1, keepdims=True)
            on = (o - mean) * lax.rsqrt(var + RWKV_GN_EPS) * lnw_ref[:, sl] + lnb_ref[:, sl]
            bonus = jnp.sum(r * k * rk_ref[:, sl], axis=-1, keepdims=True) * v
            outs.append((on + bonus) * g)
        o_ref[rows, :] = jnp.concatenate(outs, axis=-1)
        return carry

    lax.fori_loop(0, r_ref.shape[0] // WKV_CHUNK, chunk, 0)
    for h in range(HEADS_PER_TILE):
        so_ref[0, h] = st_ref[h].T


def _wkv_seq(r, lw, k, v, kk, b, g, r_k, ln_w, ln_b, batch):
    m = r.shape[0]
    seq = m // batch
    tt = _row_tile(seq, WKV_TIME_TILE)
    assert tt % WKV_CHUNK == 0
    nt = seq // tt
    row_spec = pl.BlockSpec((tt, LANE), lambda bi, t, i: (bi * nt + i, t))
    par_spec = pl.BlockSpec((1, LANE), lambda bi, t, i: (0, t))
    return pl.pallas_call(
        _wkv_seq_kernel,
        grid=(batch, HEAD_TILES, nt),
        in_specs=[row_spec] * 7 + [par_spec] * 3,
        out_specs=[row_spec,
                   pl.BlockSpec((1, HEADS_PER_TILE, HEAD_DIM, HEAD_DIM), lambda bi, t, i: (bi, t, 0, 0))],
        out_shape=[jax.ShapeDtypeStruct((m, D_HALF), F32),
                   jax.ShapeDtypeStruct((batch, HEADS, HEAD_DIM, HEAD_DIM), F32)],
        scratch_shapes=[pltpu.VMEM((HEADS_PER_TILE, HEAD_DIM, HEAD_DIM), F32)],
        compiler_params=_cp("parallel", "parallel", "arbitrary"),
    )(r, lw, k, v, kk, b, g, r_k.reshape(1, -1), ln_w.reshape(1, -1), ln_b.reshape(1, -1))


def _wkv_finish(o, r, k, v, g, rk, lnw, lnb):
    mean = jnp.mean(o, axis=-1, keepdims=True)
    var = jnp.mean(jnp.square(o - mean), axis=-1, keepdims=True)
    on = (o - mean) * lax.rsqrt(var + RWKV_GN_EPS) * lnw + lnb
    bonus = jnp.sum(r * k * rk, axis=-1, keepdims=True) * v
    return (on + bonus) * g


def _wkv_step_kernel(s_ref, r_ref, lw_ref, k_ref, v_ref, kk_ref, b_ref, g_ref, rk_ref, lnw_ref, lnb_ref,
                     o_ref, so_ref):
    ri = lax.broadcasted_iota(jnp.int32, (HEAD_DIM, HEAD_DIM), 0)
    eye = jnp.where(ri == lax.broadcasted_iota(jnp.int32, (HEAD_DIM, HEAD_DIM), 1), 1.0, 0.0)
    outs = []
    for h in range(HEADS):
        sl = slice(h * HEAD_DIM, (h + 1) * HEAD_DIM)
        r, lw, k, v, kk, b, g = [ref[0][:, sl] for ref in (r_ref, lw_ref, k_ref, v_ref, kk_ref, b_ref, g_ref)]
        s = s_ref[0, h]
        sa = -jnp.sum(s * kk, axis=-1, keepdims=True)
        v_col = jnp.sum(eye * v, axis=-1, keepdims=True)
        s = s * jnp.exp(lw) + sa * b + v_col * k
        so_ref[0, h] = s
        o_col = jnp.sum(s * r, axis=-1, keepdims=True)
        o = jnp.sum(eye * o_col, axis=0, keepdims=True)
        outs.append(_wkv_finish(o, r, k, v, g, rk_ref[:, sl], lnw_ref[:, sl], lnb_ref[:, sl]))
    o_ref[0] = jnp.concatenate(outs, axis=-1)


def _wkv_step(state, r, lw, k, v, kk, b, g, r_k, ln_w, ln_b):
    bsz = r.shape[0]
    rows = [t.reshape(bsz, 1, D_HALF) for t in (r, lw, k, v, kk, b, g)]
    row_spec = pl.BlockSpec((1, 1, D_HALF), lambda i: (i, 0, 0))
    st_spec = pl.BlockSpec((1, HEADS, HEAD_DIM, HEAD_DIM), lambda i: (i, 0, 0, 0))
    out, st = pl.pallas_call(
        _wkv_step_kernel,
        grid=(bsz,),
        in_specs=[st_spec] + [row_spec] * 7 + [_full((1, D_HALF))] * 3,
        out_specs=[row_spec, st_spec],
        out_shape=[jax.ShapeDtypeStruct((bsz, 1, D_HALF), F32),
                   jax.ShapeDtypeStruct(state.shape, F32)],
        compiler_params=_cp("parallel"),
    )(state, *rows, r_k.reshape(1, -1), ln_w.reshape(1, -1), ln_b.reshape(1, -1))
    return out.reshape(bsz, D_HALF), st


CONV_HALO = 32


def _layernorm_silu(y, g, b):
    m = jnp.mean(y, axis=-1, keepdims=True)
    var = jnp.mean(jnp.square(y - m), axis=-1, keepdims=True)
    return _silu((y - m) * lax.rsqrt(var + LN_EPS) * g + b)


def _convd_seq_kernel(gv_ref, gg_ref, hv_ref, hg_ref, w_ref, b_ref, lng_ref, lnb_ref,
                      o_ref, tail_ref, ext_ref, *, seq_tiles):
    tt = gv_ref.shape[0]
    first = (pl.program_id(0) % seq_tiles) == 0
    ext_ref[0:CONV_HALO, :] = jnp.where(first, 0.0, hv_ref[...] * _sigmoid(hg_ref[...]))
    ext_ref[CONV_HALO:, :] = gv_ref[...] * _sigmoid(gg_ref[...])
    off = CONV_HALO - (CONV_D_WIDTH - 1)
    acc = jnp.zeros((tt, D_HALF), F32)
    for j in range(CONV_D_WIDTH):
        acc = acc + w_ref[j:j + 1, :] * ext_ref[off + j:off + j + tt, :]
    o_ref[...] = _layernorm_silu(acc + b_ref[...], lng_ref[...], lnb_ref[...])
    tail_ref[0] = ext_ref[tt:, :]


def _convd_seq(proj, batch, w, b, ln_g, ln_b):
    m = proj.shape[0]
    seq = m // batch
    tt = _row_tile(seq, 256)
    seq_tiles = seq // tt
    hb = tt // CONV_HALO
    vcol, gcol = 4, 5
    tile = lambda col: pl.BlockSpec((tt, D_HALF), lambda i: (i, col))
    halo = lambda col: pl.BlockSpec((CONV_HALO, D_HALF), lambda i: (jnp.maximum(i * hb - 1, 0), col))
    params = [w, b.reshape(1, -1), ln_g.reshape(1, -1), ln_b.reshape(1, -1)]
    out, tail = pl.pallas_call(
        functools.partial(_convd_seq_kernel, seq_tiles=seq_tiles),
        grid=(m // tt,),
        in_specs=[tile(vcol), tile(gcol), halo(vcol), halo(gcol)] + [_full(p.shape) for p in params],
        out_specs=[pl.BlockSpec((tt, D_HALF), lambda i: (i, 0)),
                   pl.BlockSpec((1, CONV_HALO, D_HALF), lambda i: (i, 0, 0))],
        out_shape=[jax.ShapeDtypeStruct((m, D_HALF), F32),
                   jax.ShapeDtypeStruct((m // tt, CONV_HALO, D_HALF), F32)],
        scratch_shapes=[pltpu.VMEM((tt + CONV_HALO, D_HALF), F32)],
        compiler_params=_cp("parallel"))(proj, proj, proj, proj, *params)
    return out, tail[seq_tiles - 1::seq_tiles, CONV_HALO - (CONV_D_WIDTH - 1):]


def _convd_step_kernel(gv_ref, gg_ref, buf_ref, w_ref, b_ref, lng_ref, lnb_ref, o_ref, u_ref):
    u = gv_ref[...] * _sigmoid(gg_ref[...])
    u_ref[...] = u
    acc = u * w_ref[CONV_D_WIDTH - 1:CONV_D_WIDTH, :]
    for j in range(CONV_D_WIDTH - 1):
        acc = acc + w_ref[j:j + 1, :] * buf_ref[j]
    o_ref[...] = _layernorm_silu(acc + b_ref[...], lng_ref[...], lnb_ref[...])


def _convd_step(proj, buf, w, b, ln_g, ln_b):
    m = proj.shape[0]
    buf_t = jnp.swapaxes(buf, 0, 1)
    params = [w, b.reshape(1, -1), ln_g.reshape(1, -1), ln_b.reshape(1, -1)]
    out, u = pl.pallas_call(
        _convd_step_kernel,
        grid=(1,),
        in_specs=[pl.BlockSpec((m, D_HALF), lambda i: (0, 4)), pl.BlockSpec((m, D_HALF), lambda i: (0, 5)),
                  _full(buf_t.shape)] + [_full(p.shape) for p in params],
        out_specs=[_full((m, D_HALF)), _full((m, D_HALF))],
        out_shape=[jax.ShapeDtypeStruct((m, D_HALF), F32)] * 2,
        compiler_params=_cp("arbitrary"))(proj, proj, buf_t, *params)
    return out, jnp.concatenate([buf[:, 1:], u[:, None]], axis=1)


def _rmsnorm_kernel(x_ref, g_ref, o_ref):
    x = x_ref[...]
    ms = jnp.mean(x * x, axis=-1, keepdims=True)
    o_ref[...] = x * lax.rsqrt(ms + RMS_EPS) * g_ref[...]


def _rmsnorm(x, g):
    m, d = x.shape
    tm = _row_tile(m, 512)
    return pl.pallas_call(
        _rmsnorm_kernel,
        grid=(m // tm,),
        in_specs=[pl.BlockSpec((tm, d), lambda i: (i, 0)), _full((1, d))],
        out_specs=pl.BlockSpec((tm, d), lambda i: (i, 0)),
        out_shape=jax.ShapeDtypeStruct((m, d), F32),
        compiler_params=_cp("parallel"))(x, g.reshape(1, d))


def _rwkv_params(p, li):
    row = lambda t: t.reshape(1, -1)
    rp = dict(mu=p['rwkv_mu_rkv'][li], mz=p['rwkv_mu_z'][li], w0=row(p['rwkv_w0'][li]),
              w1=p['rwkv_w1'][li].astype(BF16), w2=p['rwkv_w2'][li].astype(BF16),
              a0=row(p['rwkv_a0'][li]), a1=p['rwkv_a1'][li].astype(BF16), a2=p['rwkv_a2'][li].astype(BF16),
              g1=p['rwkv_g1'][li].astype(BF16), g2=p['rwkv_g2'][li].astype(BF16),
              k_k=row(p['rwkv_k_k'][li]), k_a=row(p['rwkv_k_a'][li]))
    if li > 0:
        rp.update(mu_v=row(p['rwkv_mu_v'][li - 1]), v0=row(p['rwkv_v0'][li - 1]),
                  v1=p['rwkv_v1'][li - 1].astype(BF16), v2=p['rwkv_v2'][li - 1].astype(BF16))
    return rp


def _prepare(p):
    depth = p['g_mix'].shape[0]
    layers = []
    for i in range(depth):
        li = i // 2
        lay = dict(g_mix=p['g_mix'][i], g_ffn=p['g_ffn'][i],
                   w_up=p['ffn_w_up'][i].astype(BF16), w_down=p['ffn_w_down'][i].astype(BF16),
                   ffn_cw=p['ffn_conv_w'][i], ffn_cb=p['ffn_conv_b'][i])
        if i % 2 == 0:
            lay.update(w_in=p['w_in_ab'][li].astype(BF16), w_out=p['w_out_ab'][li].astype(BF16),
                       s5=_s5_params(p['s5_a_re'][li], p['s5_a_im'][li], p['s5_log_step'][li],
                                     p['s5_b_re'][li], p['s5_b_im'][li], p['s5_c_re'][li], p['s5_c_im'][li],
                                     p['s5_d'][li]),
                       w_glu=p['s5_w_glu'][li].astype(BF16), b_glu=p['s5_b_glu'][li])
        else:
            lay.update(w_in=p['w_in_cd'][li].astype(BF16), w_out=p['w_out_cd'][li].astype(BF16),
                       rwkv=_rwkv_params(p, li), r_k=p['rwkv_r_k'][li], ln_w=p['rwkv_ln_w'][li],
                       ln_b=p['rwkv_ln_b'][li], cd_w=p['conv_d_w'][li], cd_b=p['conv_d_b'][li],
                       cd_g=p['conv_d_ln_g'][li], cd_lb=p['conv_d_ln_b'][li])
        layers.append(lay)
    return layers


def _trunk_seq(x3, layers, g_final):
    batch, seq, d = x3.shape
    x = x3.reshape(batch * seq, d)
    rope = _rope_tables(seq, 0, True)
    ks, vs, s5r, s5i, wkvs, shifts, convs, ffns = [], [], [], [], [], [], [], []
    v_first = None
    for i, lay in enumerate(layers):
        if i % 2 == 0:
            proj = _norm_matmul(x, lay['g_mix'], lay['w_in'], rope=rope, rope_cols=(1, 2))
            a_out, hr, hi = _s5_seq(proj, batch, lay['s5'], lay['w_glu'], lay['b_glu'])
            b_out = _moba_seq(proj, batch)
            ks.append(proj[:, 2 * D_HALF:3 * D_HALF])
            vs.append(proj[:, 3 * D_HALF:])
            s5r.append(hr)
            s5i.append(hi)
            x = _out_proj(a_out, b_out, lay['w_out'], x)
        else:
            proj = _norm_matmul(x, lay['g_mix'], lay['w_in'])
            r, lw, k, v, kk, b, g = _rwkv_pre(proj, None, lay['rwkv'], v_first, batch)
            if v_first is None:
                v_first = v
            c_out, s_fin = _wkv_seq(r, lw, k, v, kk, b, g, lay['r_k'], lay['ln_w'], lay['ln_b'], batch)
            d_out, cbuf = _convd_seq(proj, batch, lay['cd_w'], lay['cd_b'], lay['cd_g'], lay['cd_lb'])
            wkvs.append(s_fin)
            shifts.append(proj.reshape(batch, seq, -1)[:, -1, :4 * D_HALF])
            convs.append(cbuf)
            x = _out_proj(c_out, d_out, lay['w_out'], x)
        x, fbuf = _ffn_seq(x, batch, lay['g_ffn'], lay['w_up'], lay['ffn_cw'], lay['ffn_cb'], lay['w_down'])
        ffns.append(fbuf)
    y = _rmsnorm(x, g_final).reshape(batch, seq, d)
    return (y, jnp.stack(ks), jnp.stack(vs), jnp.stack(s5r), jnp.stack(s5i), jnp.stack(wkvs),
            jnp.stack(shifts), jnp.stack(convs), jnp.stack(ffns))


def _trunk_step(x3, pos0, layers, g_final, cache_k, cache_v, page_table, s5_re0, s5_im0, wkv0, shift0,
                convd0, ffn0):
    bsz, _, d = x3.shape
    x = x3.reshape(bsz, d)
    rope = _rope_tables(bsz, pos0, False)
    n_pool = cache_k.shape[1]
    ks, vs, s5r, s5i, wkvs, shifts, convs, ffns = [], [], [], [], [], [], [], []
    v_first = None
    for i, lay in enumerate(layers):
        li = i // 2
        if i % 2 == 0:
            proj = _norm_matmul(x, lay['g_mix'], lay['w_in'], rope=rope, rope_cols=(1, 2))
            a_out, hr, hi = _s5_step(proj, s5_re0[li].reshape(bsz, -1), s5_im0[li].reshape(bsz, -1),
                                     lay['s5'], lay['w_glu'], lay['b_glu'])
            b_out = _moba_step(proj, cache_k[li].reshape(n_pool, PAGE_SIZE, D_HALF),
                               cache_v[li].reshape(n_pool, PAGE_SIZE, D_HALF), page_table)
            ks.append(proj[:, 2 * D_HALF:3 * D_HALF])
            vs.append(proj[:, 3 * D_HALF:])
            s5r.append(hr)
            s5i.append(hi)
            x = _out_proj(a_out, b_out, lay['w_out'], x)
        else:
            proj = _norm_matmul(x, lay['g_mix'], lay['w_in'])
            r, lw, k, v, kk, b, g = _rwkv_pre(proj, shift0[li], lay['rwkv'], v_first, bsz)
            if v_first is None:
                v_first = v
            c_out, s_fin = _wkv_step(wkv0[li], r, lw, k, v, kk, b, g, lay['r_k'], lay['ln_w'], lay['ln_b'])
            d_out, cbuf = _convd_step(proj, convd0[li], lay['cd_w'], lay['cd_b'], lay['cd_g'], lay['cd_lb'])
            wkvs.append(s_fin)
            shifts.append(proj[:, :4 * D_HALF])
            convs.append(cbuf)
            x = _out_proj(c_out, d_out, lay['w_out'], x)
        x, fbuf = _ffn_step(x, ffn0[i], lay['g_ffn'], lay['w_up'], lay['ffn_cw'], lay['ffn_cb'], lay['w_down'])
        ffns.append(fbuf)
    y = _rmsnorm(x, g_final).reshape(bsz, 1, d)
    return (y, jnp.stack(ks), jnp.stack(vs), jnp.stack(s5r), jnp.stack(s5i), jnp.stack(wkvs),
            jnp.stack(shifts), jnp.stack(convs), jnp.stack(ffns))


def kernel(x_prompt, x_sample, cache_k_moba, cache_v_moba, page_table, state_s5_re, state_s5_im,
           state_rwkv_wkv, state_rwkv_shift, state_conv_d, state_ffn_conv, g_mix, g_ffn, g_final,
           w_in_ab, w_out_ab, s5_a_re, s5_a_im, s5_log_step, s5_b_re, s5_b_im, s5_c_re, s5_c_im, s5_d,
           s5_w_glu, s5_b_glu, w_in_cd, w_out_cd, rwkv_mu_rkv, rwkv_mu_z, rwkv_w0, rwkv_w1, rwkv_w2,
           rwkv_a0, rwkv_a1, rwkv_a2, rwkv_g1, rwkv_g2, rwkv_k_k, rwkv_k_a, rwkv_r_k, rwkv_ln_w, rwkv_ln_b,
           rwkv_mu_v, rwkv_v0, rwkv_v1, rwkv_v2, conv_d_w, conv_d_b, conv_d_ln_g, conv_d_ln_b,
           ffn_w_up, ffn_conv_w, ffn_conv_b, ffn_w_down):
    p = dict(g_mix=g_mix, g_ffn=g_ffn, w_in_ab=w_in_ab, w_out_ab=w_out_ab,
             s5_a_re=s5_a_re, s5_a_im=s5_a_im, s5_log_step=s5_log_step, s5_b_re=s5_b_re, s5_b_im=s5_b_im,
             s5_c_re=s5_c_re, s5_c_im=s5_c_im, s5_d=s5_d, s5_w_glu=s5_w_glu, s5_b_glu=s5_b_glu,
             w_in_cd=w_in_cd, w_out_cd=w_out_cd, rwkv_mu_rkv=rwkv_mu_rkv, rwkv_mu_z=rwkv_mu_z,
             rwkv_w0=rwkv_w0, rwkv_w1=rwkv_w1, rwkv_w2=rwkv_w2, rwkv_a0=rwkv_a0, rwkv_a1=rwkv_a1,
             rwkv_a2=rwkv_a2, rwkv_g1=rwkv_g1, rwkv_g2=rwkv_g2, rwkv_k_k=rwkv_k_k, rwkv_k_a=rwkv_k_a,
             rwkv_r_k=rwkv_r_k.reshape(rwkv_r_k.shape[0], -1), rwkv_ln_w=rwkv_ln_w, rwkv_ln_b=rwkv_ln_b,
             rwkv_mu_v=rwkv_mu_v, rwkv_v0=rwkv_v0, rwkv_v1=rwkv_v1, rwkv_v2=rwkv_v2,
             conv_d_w=conv_d_w, conv_d_b=conv_d_b, conv_d_ln_g=conv_d_ln_g, conv_d_ln_b=conv_d_ln_b,
             ffn_w_up=ffn_w_up, ffn_conv_w=ffn_conv_w, ffn_conv_b=ffn_conv_b, ffn_w_down=ffn_w_down)
    layers = _prepare(p)
    bp, seq, _ = x_prompt.shape
    bs = x_sample.shape[0]
    n_ab = w_in_ab.shape[0]
    n_cd = w_in_cd.shape[0]
    page = cache_k_moba.shape[2]
    past_len = page_table.shape[1] * page
    (y_p, k_p, v_p, s5r_p, s5i_p, wkv_p, sh_p, cd_p, ff_p) = _trunk_seq(x_prompt, layers, g_final)
    (y_s, k_s, v_s, s5r_s, s5i_s, wkv_s, sh_s, cd_s, ff_s) = _trunk_step(
        x_sample, past_len, layers, g_final, cache_k_moba, cache_v_moba, page_table, state_s5_re, state_s5_im,
        state_rwkv_wkv, state_rwkv_shift, state_conv_d, state_ffn_conv)
    kv_p = (n_ab, bp, seq // page, page, HEADS, HEAD_DIM)
    kv_s = (n_ab, bs, 1, HEADS, HEAD_DIM)
    s5_p = (n_ab, bp, S5_GROUPS, S5_STATE)
    s5_s = (n_ab, bs, S5_GROUPS, S5_STATE)
    return (y_p, y_s, k_p.reshape(kv_p), v_p.reshape(kv_p), k_s.reshape(kv_s), v_s.reshape(kv_s),
            s5r_p.reshape(s5_p), s5i_p.reshape(s5_p), s5r_s.reshape(s5_s), s5i_s.reshape(s5_s),
            wkv_p, wkv_s, sh_p, sh_s, cd_p, cd_s, ff_p, ff_s)
```

```python
import functools
import math

import numpy as np
import jax
import jax.numpy as jnp
from jax import lax
from jax.experimental import pallas as pl
from jax.experimental.pallas import tpu as pltpu

F32 = jnp.float32
BF16 = jnp.bfloat16
HIGHEST = lax.Precision.HIGHEST

D_MODEL = 1024
D_HALF = 512
S5_GROUP = 16
S5_GROUPS = 32
S5_STATE = 64
S5_LANES = S5_GROUPS * S5_STATE
HEADS = 8
HEAD_DIM = 64
MOBA_BLOCK = 256
MOBA_TOPK = 3
PAGE_SIZE = 128
ROPE_DIM = 16
ROPE_THETA = 500000.0
RWKV_GN_EPS = 64e-5
CONV_D_WIDTH = 31
D_FF = 2816
RMS_EPS = 1e-6
LN_EPS = 1e-5
LANE = 128
SUBLANE = 8
VMEM_LIMIT = 48 * 1024 * 1024
NEG = -1e30


def _cp(*sem):
    return pltpu.CompilerParams(dimension_semantics=sem, vmem_limit_bytes=VMEM_LIMIT)


def _bdot(a, b):
    return jnp.dot(a.astype(BF16), b.astype(BF16), preferred_element_type=F32)


def _sigmoid(x):
    return 1.0 / (1.0 + jnp.exp(-x))


def _silu(x):
    return x * _sigmoid(x)


def _row_tile(m, pref):
    return pref if m % pref == 0 else m


def _rope_table_kernel(inv_ref, cos_ref, sa_ref, sb_ref, *, pos0, per_row_pos):
    rows = cos_ref.shape[0]
    lane = lax.broadcasted_iota(jnp.int32, (rows, LANE), 1) % HEAD_DIM
    if per_row_pos:
        pos = (pos0 + lax.broadcasted_iota(jnp.int32, (rows, LANE), 0)).astype(F32)
    else:
        pos = jnp.full((rows, LANE), pos0, F32)
    ang = pos * inv_ref[...]
    c, s = jnp.cos(ang), jnp.sin(ang)
    cos_ref[...] = jnp.where(lane < ROPE_DIM, c, 1.0)
    sa_ref[...] = jnp.where(lane < ROPE_DIM // 2, -s, 0.0)
    sb_ref[...] = jnp.where((lane >= ROPE_DIM // 2) & (lane < ROPE_DIM), s, 0.0)


def _rope_tables(rows, pos0, per_row_pos):
    half = ROPE_DIM // 2
    inv8 = (np.float32(1.0) / (np.float32(ROPE_THETA) ** (np.arange(half, dtype=np.float32) / np.float32(half))))
    inv = np.zeros((HEAD_DIM,), np.float32)
    inv[:half] = inv8
    inv[half:ROPE_DIM] = inv8
    inv = jnp.asarray(np.tile(inv, LANE // HEAD_DIM)[None, :])
    shp = jax.ShapeDtypeStruct((rows, LANE), F32)
    return pl.pallas_call(
        functools.partial(_rope_table_kernel, pos0=pos0, per_row_pos=per_row_pos),
        out_shape=(shp, shp, shp))(inv)


def _norm_matmul_kernel(x_ref, g_ref, w_ref, *rest, rope_cols):
    if rope_cols:
        cos_ref, sa_ref, sb_ref, o_ref, h_ref = rest
    else:
        o_ref, h_ref = rest
    j = pl.program_id(1)

    @pl.when(j == 0)
    def _():
        x = x_ref[...]
        ms = jnp.mean(x * x, axis=-1, keepdims=True)
        h_ref[...] = (x * lax.rsqrt(ms + RMS_EPS) * g_ref[...]).astype(BF16)

    y = jnp.dot(h_ref[...], w_ref[...], preferred_element_type=F32)
    if not rope_cols:
        o_ref[...] = y
        return
    is_rope = functools.reduce(jnp.logical_or, [j == c for c in rope_cols])

    @pl.when(is_rope)
    def _():
        cos, sa, sb = cos_ref[...], sa_ref[...], sb_ref[...]
        for c in range(y.shape[1] // LANE):
            yc = y[:, c * LANE:(c + 1) * LANE]
            o_ref[:, c * LANE:(c + 1) * LANE] = (
                yc * cos + pltpu.roll(yc, LANE - ROPE_DIM // 2, axis=1) * sa
                + pltpu.roll(yc, ROPE_DIM // 2, axis=1) * sb)

    @pl.when(jnp.logical_not(is_rope))
    def _():
        o_ref[...] = y


def _norm_matmul(x, g, w_bf, rope=None, rope_cols=()):
    m, d = x.shape
    n = w_bf.shape[1]
    tm = _row_tile(m, 512)
    tn = D_HALF
    in_specs = [pl.BlockSpec((tm, d), lambda i, j: (i, 0)),
                pl.BlockSpec((1, d), lambda i, j: (0, 0)),
                pl.BlockSpec((d, tn), lambda i, j: (0, j))]
    args = [x, g.reshape(1, d), w_bf]
    if rope_cols:
        nt = rope[0].shape[0] // tm
        for t in rope:
            in_specs.append(pl.BlockSpec((tm, LANE), lambda i, j: (i % nt, 0)))
            args.append(t)
    return pl.pallas_call(
        functools.partial(_norm_matmul_kernel, rope_cols=tuple(rope_cols)),
        grid=(m // tm, n // tn),
        in_specs=in_specs,
        out_specs=pl.BlockSpec((tm, tn), lambda i, j: (i, j)),
        out_shape=jax.ShapeDtypeStruct((m, n), F32),
        scratch_shapes=[pltpu.VMEM((tm, d), BF16)],
        compiler_params=_cp("parallel", "arbitrary"))(*args)


def _out_proj_kernel(a_ref, b_ref, wa_ref, wb_ref, res_ref, o_ref):
    o_ref[...] = res_ref[...] + (_bdot(a_ref[...], wa_ref[...]) + _bdot(b_ref[...], wb_ref[...]))


def _out_proj(a, b, w_bf, res):
    m, c = a.shape
    n = w_bf.shape[1]
    tm = _row_tile(m, 512)
    return pl.pallas_call(
        _out_proj_kernel,
        grid=(m // tm,),
        in_specs=[pl.BlockSpec((tm, c), lambda i: (i, 0)),
                  pl.BlockSpec((tm, c), lambda i: (i, 0)),
                  pl.BlockSpec((c, n), lambda i: (0, 0)),
                  pl.BlockSpec((c, n), lambda i: (1, 0)),
                  pl.BlockSpec((tm, n), lambda i: (i, 0))],
        out_specs=pl.BlockSpec((tm, n), lambda i: (i, 0)),
        out_shape=jax.ShapeDtypeStruct((m, n), F32),
        compiler_params=_cp("parallel"))(a, b, w_bf, w_bf, res)


FFN_TN = 256
FFN_HALO = SUBLANE


def _ffn_seq_kernel(x_ref, xh_ref, g_ref, wa_ref, wb_ref, cwa_ref, cwb_ref, cba_ref, cbb_ref, wd_ref,
                    o_ref, ua_ref, ub_ref, h_ref, acc_ref, *, tiles_per_seq):
    i, c = pl.program_id(0), pl.program_id(1)
    tm = x_ref.shape[0]
    first = (i % tiles_per_seq) == 0

    @pl.when(c == 0)
    def _():
        def norm(x):
            ms = jnp.mean(x * x, axis=-1, keepdims=True)
            return (x * lax.rsqrt(ms + RMS_EPS) * g_ref[...]).astype(BF16)
        h_ref[0:FFN_HALO, :] = norm(xh_ref[...])
        h_ref[FFN_HALO:, :] = norm(x_ref[...])
        acc_ref[...] = jnp.zeros_like(acc_ref)

    h = h_ref[...]
    row = lax.broadcasted_iota(jnp.int32, (tm + FFN_HALO, 1), 0)
    keep = jnp.logical_or(row >= FFN_HALO, jnp.logical_not(first))

    def branch(w_ref, cw_ref, cb_ref, u_ref):
        u = jnp.where(keep, jnp.dot(h, w_ref[...], preferred_element_type=F32), 0.0)
        u_ref[0] = u[tm + FFN_HALO - 2:, :]
        cw = cw_ref[...]
        y = (u * cw[2:3, :] + pltpu.roll(u, 1, axis=0) * cw[1:2, :]
             + pltpu.roll(u, 2, axis=0) * cw[0:1, :])
        return y[FFN_HALO:, :] + cb_ref[...]

    a = branch(wa_ref, cwa_ref, cba_ref, ua_ref)
    b = branch(wb_ref, cwb_ref, cbb_ref, ub_ref)
    acc_ref[...] += _bdot(_silu(a) * b, wd_ref[...])

    @pl.when(c == pl.num_programs(1) - 1)
    def _():
        o_ref[...] = x_ref[...] + acc_ref[...]


def _ffn_seq(x, batch, g, w_up_bf, cw, cb, w_down_bf):
    m, d = x.shape
    seq = m // batch
    tm = _row_tile(seq, 512)
    tps = seq // tm
    nc = D_FF // FFN_TN
    hb = tm // FFN_HALO
    out, ua, ub = pl.pallas_call(
        functools.partial(_ffn_seq_kernel, tiles_per_seq=tps),
        grid=(m // tm, nc),
        in_specs=[pl.BlockSpec((tm, d), lambda i, c: (i, 0)),
                  pl.BlockSpec((FFN_HALO, d), lambda i, c: (jnp.maximum(i * hb - 1, 0), 0)),
                  pl.BlockSpec((1, d), lambda i, c: (0, 0)),
                  pl.BlockSpec((d, FFN_TN), lambda i, c: (0, c)),
                  pl.BlockSpec((d, FFN_TN), lambda i, c: (0, nc + c)),
                  pl.BlockSpec((3, FFN_TN), lambda i, c: (0, c)),
                  pl.BlockSpec((3, FFN_TN), lambda i, c: (0, nc + c)),
                  pl.BlockSpec((1, FFN_TN), lambda i, c: (0, c)),
                  pl.BlockSpec((1, FFN_TN), lambda i, c: (0, nc + c)),
                  pl.BlockSpec((FFN_TN, d), lambda i, c: (c, 0))],
        out_specs=[pl.BlockSpec((tm, d), lambda i, c: (i, 0)),
                   pl.BlockSpec((1, 2, FFN_TN), lambda i, c: (i, 0, c)),
                   pl.BlockSpec((1, 2, FFN_TN), lambda i, c: (i, 0, c))],
        out_shape=[jax.ShapeDtypeStruct((m, d), F32),
                   jax.ShapeDtypeStruct((m // tm, 2, D_FF), F32),
                   jax.ShapeDtypeStruct((m // tm, 2, D_FF), F32)],
        scratch_shapes=[pltpu.VMEM((tm + FFN_HALO, d), BF16), pltpu.VMEM((tm, d), F32)],
        compiler_params=_cp("arbitrary", "arbitrary"),
    )(x, x, g.reshape(1, d), w_up_bf, w_up_bf, cw, cw, cb.reshape(1, -1), cb.reshape(1, -1), w_down_bf)
    return out, jnp.concatenate([ua[tps - 1::tps], ub[tps - 1::tps]], axis=-1)


def _ffn_step_kernel(x_ref, g_ref, wa_ref, wb_ref, cwa_ref, cwb_ref, cba_ref, cbb_ref, wd_ref,
                     bufa_ref, bufb_ref, o_ref, ua_ref, ub_ref, h_ref, acc_ref):
    c = pl.program_id(0)

    @pl.when(c == 0)
    def _():
        x = x_ref[...]
        ms = jnp.mean(x * x, axis=-1, keepdims=True)
        h_ref[...] = (x * lax.rsqrt(ms + RMS_EPS) * g_ref[...]).astype(BF16)
        acc_ref[...] = jnp.zeros_like(acc_ref)

    h = h_ref[...]

    def branch(w_ref, cw_ref, cb_ref, buf_ref, u_ref):
        u = jnp.dot(h, w_ref[...], preferred_element_type=F32)
        u_ref[...] = u
        cw = cw_ref[...]
        return u * cw[2:3, :] + buf_ref[1] * cw[1:2, :] + buf_ref[0] * cw[0:1, :] + cb_ref[...]

    a = branch(wa_ref, cwa_ref, cba_ref, bufa_ref, ua_ref)
    b = branch(wb_ref, cwb_ref, cbb_ref, bufb_ref, ub_ref)
    acc_ref[...] += _bdot(_silu(a) * b, wd_ref[...])

    @pl.when(c == pl.num_programs(0) - 1)
    def _():
        o_ref[...] = x_ref[...] + acc_ref[...]


def _ffn_step(x, buf, g, w_up_bf, cw, cb, w_down_bf):
    m, d = x.shape
    nc = D_FF // FFN_TN
    buf_t = jnp.swapaxes(buf, 0, 1)
    out, ua, ub = pl.pallas_call(
        _ffn_step_kernel,
        grid=(nc,),
        in_specs=[pl.BlockSpec((m, d), lambda c: (0, 0)),
                  pl.BlockSpec((1, d), lambda c: (0, 0)),
                  pl.BlockSpec((d, FFN_TN), lambda c: (0, c)),
                  pl.BlockSpec((d, FFN_TN), lambda c: (0, nc + c)),
                  pl.BlockSpec((3, FFN_TN), lambda c: (0, c)),
                  pl.BlockSpec((3, FFN_TN), lambda c: (0, nc + c)),
                  pl.BlockSpec((1, FFN_TN), lambda c: (0, c)),
                  pl.BlockSpec((1, FFN_TN), lambda c: (0, nc + c)),
                  pl.BlockSpec((FFN_TN, d), lambda c: (c, 0)),
                  pl.BlockSpec((2, m, FFN_TN), lambda c: (0, 0, c)),
                  pl.BlockSpec((2, m, FFN_TN), lambda c: (0, 0, nc + c))],
        out_specs=[pl.BlockSpec((m, d), lambda c: (0, 0)),
                   pl.BlockSpec((m, FFN_TN), lambda c: (0, c)),
                   pl.BlockSpec((m, FFN_TN), lambda c: (0, c))],
        out_shape=[jax.ShapeDtypeStruct((m, d), F32),
                   jax.ShapeDtypeStruct((m, D_FF), F32),
                   jax.ShapeDtypeStruct((m, D_FF), F32)],
        scratch_shapes=[pltpu.VMEM((m, d), BF16), pltpu.VMEM((m, d), F32)],
        compiler_params=_cp("arbitrary"),
    )(x, g.reshape(1, d), w_up_bf, w_up_bf, cw, cw, cb.reshape(1, -1), cb.reshape(1, -1), w_down_bf,
      buf_t, buf_t)
    u = jnp.concatenate([ua, ub], axis=-1)
    return out, jnp.stack([buf[:, 1], u], axis=1)


S5_STRIP = 512
S5_NSTRIP = S5_LANES // S5_STRIP


def _s5_param_kernel(are_ref, aim_ref, ls_ref, brt_ref, bit_ref, abr_ref, abi_ref, bbr_ref, bbi_ref):
    ar, ai = are_ref[...], aim_ref[...]
    dt = jnp.exp(ls_ref[...])
    mag = jnp.exp(ar * dt)
    abr, abi = mag * jnp.cos(ai * dt), mag * jnp.sin(ai * dt)
    inv_abs2 = 1.0 / (ar * ar + ai * ai)
    cr = ((abr - 1.0) * ar + abi * ai) * inv_abs2
    ci = (abi * ar - (abr - 1.0) * ai) * inv_abs2
    br, bi = brt_ref[...], bit_ref[...]
    abr_ref[...] = abr
    abi_ref[...] = abi
    bbr_ref[...] = cr * br - ci * bi
    bbi_ref[...] = cr * bi + ci * br


def _s5_params(a_re, a_im, log_step, b_re, b_im, c_re, c_im, d_skip):
    g, n, k = b_re.shape
    rep = lambda t: jnp.repeat(t, k, axis=0)
    brt = b_re.transpose(0, 2, 1).reshape(g * k, n)
    bit = b_im.transpose(0, 2, 1).reshape(g * k, n)
    shp = jax.ShapeDtypeStruct((g * k, n), F32)
    abr, abi, bbr, bbi = pl.pallas_call(_s5_param_kernel, out_shape=(shp, shp, shp, shp))(
        rep(a_re), rep(a_im), rep(log_step[:, None]), brt, bit)
    abr = abr[::k].reshape(1, g * n)
    abi = abi[::k].reshape(1, g * n)
    gs = S5_STRIP // n
    eye = jnp.eye(gs, dtype=F32)

    def in_blocks(t):
        t = t.reshape(S5_NSTRIP, gs, k, n)
        return jnp.einsum('cgkn,gh->cgkhn', t, eye).reshape(S5_NSTRIP, gs * k, gs * n).astype(BF16)

    def out_blocks(t):
        t = t.reshape(S5_NSTRIP, gs, k, n)
        return jnp.einsum('cgkn,gh->cgnhk', t, eye).reshape(S5_NSTRIP, gs * n, gs * k).astype(BF16)

    return dict(abr=abr, abi=abi, wbr=in_blocks(bbr), wbi=in_blocks(bbi),
                wcr=out_blocks(c_re), wci=out_blocks(c_im), d=d_skip.reshape(1, g * k))


def _s5_input(u, wbr_ref, wbi_ref, c):
    cw = S5_STRIP // (S5_STATE // S5_GROUP)
    uc = u[:, c * cw:(c + 1) * cw].astype(BF16)
    return (jnp.dot(uc, wbr_ref[c], preferred_element_type=F32),
            jnp.dot(uc, wbi_ref[c], preferred_element_type=F32))


def _s5_output(u, h_strip, wcr_ref, wci_ref, d_ref, wg_ref, bg_ref):
    ys = []
    for c in range(S5_NSTRIP):
        hr, hi = h_strip(c)
        ys.append(jnp.dot(hr.astype(BF16), wcr_ref[c], preferred_element_type=F32)
                  - jnp.dot(hi.astype(BF16), wci_ref[c], preferred_element_type=F32))
    y = jnp.concatenate(ys, axis=-1) + d_ref[...] * u
    yg = 0.5 * y * (1.0 + jnp.tanh(math.sqrt(2.0 / math.pi) * (y + 0.044715 * (y * y * y))))
    z = jnp.dot(yg.astype(BF16), wg_ref[...], preferred_element_type=F32) + bg_ref[...]
    return yg * _sigmoid(z)


def _cmul(ar, ai, br, bi):
    return ar * br - ai * bi, ar * bi + ai * br


def _s5_seq_kernel(u_ref, abr_ref, abi_ref, wbr_ref, wbi_ref, wcr_ref, wci_ref, d_ref, wg_ref, bg_ref,
                   o_ref, hro_ref, hio_ref, hr_ref, hi_ref, cr_ref, ci_ref, tab_ref):
    i = pl.program_id(1)
    tt = u_ref.shape[0]

    @pl.when(i == 0)
    def _():
        cr_ref[...] = jnp.zeros_like(cr_ref)
        ci_ref[...] = jnp.zeros_like(ci_ref)
        row = lax.broadcasted_iota(jnp.int32, (SUBLANE, S5_LANES), 0)
        p_r, p_i = [abr_ref[...]], [abi_ref[...]]
        for _ in range(SUBLANE - 1):
            nr, ni = _cmul(p_r[-1], p_i[-1], p_r[0], p_i[0])
            p_r.append(nr)
            p_i.append(ni)
        for k, s in enumerate((1, 2, 4)):
            tab_ref[2 * k] = jnp.where(row >= s, p_r[s - 1], 0.0)
            tab_ref[2 * k + 1] = jnp.where(row >= s, p_i[s - 1], 0.0)
        car_r = jnp.zeros((SUBLANE, S5_LANES), F32)
        car_i = jnp.zeros((SUBLANE, S5_LANES), F32)
        for j in range(SUBLANE):
            car_r = jnp.where(row == j, p_r[j], car_r)
            car_i = jnp.where(row == j, p_i[j], car_i)
        tab_ref[6] = car_r
        tab_ref[7] = car_i

    u = u_ref[...]
    for c in range(S5_NSTRIP):
        br, bi = _s5_input(u, wbr_ref, wbi_ref, c)
        hr_ref[:, c * S5_STRIP:(c + 1) * S5_STRIP] = br
        hi_ref[:, c * S5_STRIP:(c + 1) * S5_STRIP] = bi

    def block(r, carry):
        r8 = pl.multiple_of(r * SUBLANE, SUBLANE)
        for c in range(S5_NSTRIP):
            sl = slice(c * S5_STRIP, (c + 1) * S5_STRIP)
            xr, xi = hr_ref[pl.ds(r8, SUBLANE), sl], hi_ref[pl.ds(r8, SUBLANE), sl]
            for k, s in enumerate((1, 2, 4)):
                dr, di = _cmul(tab_ref[2 * k, :, sl], tab_ref[2 * k + 1, :, sl],
                               pltpu.roll(xr, s, axis=0), pltpu.roll(xi, s, axis=0))
                xr, xi = xr + dr, xi + di
            dr, di = _cmul(tab_ref[6, :, sl], tab_ref[7, :, sl], cr_ref[:, sl], ci_ref[:, sl])
            xr, xi = xr + dr, xi + di
            hr_ref[pl.ds(r8, SUBLANE), sl] = xr
            hi_ref[pl.ds(r8, SUBLANE), sl] = xi
            cr_ref[:, sl] = xr[SUBLANE - 1:, :]
            ci_ref[:, sl] = xi[SUBLANE - 1:, :]
        return carry

    lax.fori_loop(0, tt // SUBLANE, block, 0)

    def h_strip(c):
        sl = slice(c * S5_STRIP, (c + 1) * S5_STRIP)
        return hr_ref[:, sl], hi_ref[:, sl]

    o_ref[...] = _s5_output(u, h_strip, wcr_ref, wci_ref, d_ref, wg_ref, bg_ref)
    hro_ref[0] = cr_ref[...]
    hio_ref[0] = ci_ref[...]


def _full(shape):
    nd = len(shape)
    return pl.BlockSpec(shape, lambda *_: (0,) * nd)


def _s5_seq(proj, batch, sp, w_glu_bf, b_glu):
    m = proj.shape[0]
    seq = m // batch
    tt = _row_tile(seq, 256)
    nt = seq // tt
    params = [sp['abr'], sp['abi'], sp['wbr'], sp['wbi'], sp['wcr'], sp['wci'], sp['d'],
              w_glu_bf, b_glu.reshape(1, -1)]
    out, hr, hi = pl.pallas_call(
        _s5_seq_kernel,
        grid=(batch, nt),
        in_specs=[pl.BlockSpec((tt, D_HALF), lambda b, i: (b * nt + i, 0))] + [_full(p.shape) for p in params],
        out_specs=[pl.BlockSpec((tt, D_HALF), lambda b, i: (b * nt + i, 0)),
                   pl.BlockSpec((1, 1, S5_LANES), lambda b, i: (b, 0, 0)),
                   pl.BlockSpec((1, 1, S5_LANES), lambda b, i: (b, 0, 0))],
        out_shape=[jax.ShapeDtypeStruct((m, D_HALF), F32),
                   jax.ShapeDtypeStruct((batch, 1, S5_LANES), F32),
                   jax.ShapeDtypeStruct((batch, 1, S5_LANES), F32)],
        scratch_shapes=[pltpu.VMEM((tt, S5_LANES), F32), pltpu.VMEM((tt, S5_LANES), F32),
                        pltpu.VMEM((1, S5_LANES), F32), pltpu.VMEM((1, S5_LANES), F32),
                        pltpu.VMEM((8, SUBLANE, S5_LANES), F32)],
        compiler_params=_cp("arbitrary", "arbitrary"))(proj, *params)
    return out, hr[:, 0], hi[:, 0]


def _s5_step_kernel(u_ref, h0r_ref, h0i_ref, abr_ref, abi_ref, wbr_ref, wbi_ref, wcr_ref, wci_ref, d_ref,
                    wg_ref, bg_ref, o_ref, hro_ref, hio_ref):
    u = u_ref[...]
    for c in range(S5_NSTRIP):
        sl = slice(c * S5_STRIP, (c + 1) * S5_STRIP)
        br, bi = _s5_input(u, wbr_ref, wbi_ref, c)
        dr, di = _cmul(abr_ref[:, sl], abi_ref[:, sl], h0r_ref[:, sl], h0i_ref[:, sl])
        hro_ref[:, sl] = dr + br
        hio_ref[:, sl] = di + bi

    def h_strip(c):
        sl = slice(c * S5_STRIP, (c + 1) * S5_STRIP)
        return hro_ref[:, sl], hio_ref[:, sl]

    o_ref[...] = _s5_output(u, h_strip, wcr_ref, wci_ref, d_ref, wg_ref, bg_ref)


def _s5_step(proj, h0r, h0i, sp, w_glu_bf, b_glu):
    m = proj.shape[0]
    args = [h0r, h0i, sp['abr'], sp['abi'], sp['wbr'], sp['wbi'], sp['wcr'], sp['wci'], sp['d'],
            w_glu_bf, b_glu.reshape(1, -1)]
    return pl.pallas_call(
        _s5_step_kernel,
        grid=(1,),
        in_specs=[pl.BlockSpec((m, D_HALF), lambda i: (0, 0))] + [_full(a.shape) for a in args],
        out_specs=[_full((m, D_HALF)), _full((m, S5_LANES)), _full((m, S5_LANES))],
        out_shape=[jax.ShapeDtypeStruct((m, D_HALF), F32),
                   jax.ShapeDtypeStruct((m, S5_LANES), F32),
                   jax.ShapeDtypeStruct((m, S5_LANES), F32)],
        compiler_params=_cp("arbitrary"))(proj, *args)


HEADS_PER_TILE = LANE // HEAD_DIM
HEAD_TILES = HEADS // HEADS_PER_TILE
ATTN_SCALE = HEAD_DIM ** -0.5
NT_DIMS = (((1,), (1,)), ((), ()))
MOBA_GROUP = 4


def _moba_seq_kernel(q_ref, k_ref, v_ref, o_ref, kbf_ref, vt_ref, kmean_ref, sel_ref, *, nb):
    qi = pl.program_id(2)
    blk = MOBA_BLOCK

    @pl.when(qi == 0)
    def _():
        kbf_ref[...] = k_ref[...].astype(BF16)
        vt_ref[...] = v_ref[...].T.astype(BF16)
        for j in range(nb):
            kmean_ref[j:j + 1, :] = jnp.mean(k_ref[j * blk:(j + 1) * blk, :], axis=0, keepdims=True)

    q = q_ref[...]
    row = lax.broadcasted_iota(jnp.int32, (nb, blk), 0)
    causal = (lax.broadcasted_iota(jnp.int32, (blk, blk), 0) <= lax.broadcasted_iota(jnp.int32, (blk, blk), 1))
    q0 = pl.multiple_of(qi * blk, blk)
    head_lanes = [slice(h * HEAD_DIM, (h + 1) * HEAD_DIM) for h in range(HEADS_PER_TILE)]
    qts, state = [], []
    for h, sl in enumerate(head_lanes):
        qh = q[:, sl]
        s = lax.dot_general(kmean_ref[:, sl], qh, NT_DIMS, precision=HIGHEST, preferred_element_type=F32)
        rank = jnp.zeros((nb, blk), F32)
        for i in range(nb):
            si = s[i:i + 1, :]
            beats = jnp.logical_or(si > s, jnp.logical_and(si == s, i < row))
            rank = rank + jnp.where(jnp.logical_and(beats, i < qi), 1.0, 0.0)
        sel_ref[h] = jnp.where(jnp.logical_and(row < qi, rank < MOBA_TOPK), 1.0, 0.0)

        qt = (qh * ATTN_SCALE).T.astype(BF16)
        lg = jnp.dot(kbf_ref[pl.ds(q0, blk), sl], qt, preferred_element_type=F32)
        lg = jnp.where(causal, lg, NEG)
        m = jnp.max(lg, axis=0, keepdims=True)
        p = jnp.exp(lg - m)
        l = jnp.sum(p, axis=0, keepdims=True)
        acc = jnp.dot(vt_ref[sl, pl.ds(q0, blk)], p.astype(BF16), preferred_element_type=F32)
        qts.append(qt)
        state += [m, l, acc]

    grp = MOBA_GROUP

    def body(jj, carry):
        j0 = pl.multiple_of(jj * (grp * blk), grp * blk)
        out = []
        for h, sl in enumerate(head_lanes):
            m, l, acc = carry[3 * h:3 * h + 3]
            lg = jnp.dot(kbf_ref[pl.ds(j0, grp * blk), sl], qts[h], preferred_element_type=F32)
            lg = jnp.concatenate(
                [jnp.where(sel_ref[h, pl.ds(jj * grp + g, 1), :] > 0.0, lg[g * blk:(g + 1) * blk], NEG)
                 for g in range(grp)], axis=0)
            m_new = jnp.maximum(m, jnp.max(lg, axis=0, keepdims=True))
            alpha = jnp.exp(m - m_new)
            p = jnp.exp(lg - m_new)
            l = alpha * l + jnp.sum(p, axis=0, keepdims=True)
            acc = alpha * acc + jnp.dot(vt_ref[sl, pl.ds(j0, grp * blk)], p.astype(BF16),
                                        preferred_element_type=F32)
            out += [m_new, l, acc]
        return tuple(out)

    state = lax.fori_loop(0, (qi + grp - 1) // grp, body, tuple(state))
    for h, sl in enumerate(head_lanes):
        m, l, acc = state[3 * h:3 * h + 3]
        o_ref[:, sl] = (acc / l).T


def _moba_seq(proj, batch):
    m = proj.shape[0]
    seq = m // batch
    assert seq % (MOBA_BLOCK * MOBA_GROUP) == 0
    nb = seq // MOBA_BLOCK
    qoff, koff, voff = HEAD_TILES, 2 * HEAD_TILES, 3 * HEAD_TILES
    return pl.pallas_call(
        functools.partial(_moba_seq_kernel, nb=nb),
        grid=(batch, HEAD_TILES, nb),
        in_specs=[pl.BlockSpec((MOBA_BLOCK, LANE), lambda b, t, i: (b * nb + i, qoff + t)),
                  pl.BlockSpec((seq, LANE), lambda b, t, i: (b, koff + t)),
                  pl.BlockSpec((seq, LANE), lambda b, t, i: (b, voff + t))],
        out_specs=pl.BlockSpec((MOBA_BLOCK, LANE), lambda b, t, i: (b * nb + i, t)),
        out_shape=jax.ShapeDtypeStruct((m, D_HALF), F32),
        scratch_shapes=[pltpu.VMEM((seq, LANE), BF16), pltpu.VMEM((LANE, seq), BF16),
                        pltpu.VMEM((nb, LANE), F32), pltpu.VMEM((HEADS_PER_TILE, nb, MOBA_BLOCK), F32)],
        compiler_params=_cp("parallel", "parallel", "arbitrary"))(proj, proj, proj)


MOBA_PAGES_PER_STEP = 16
PAGES_PER_BLOCK = MOBA_BLOCK // PAGE_SIZE


def _moba_rank_kernel(pt_ref, q_ref, *rest, nblk):
    k_refs = rest[:MOBA_PAGES_PER_STEP]
    idx_ref, kmean_ref = rest[MOBA_PAGES_PER_STEP:]
    j = pl.program_id(1)
    bps = MOBA_PAGES_PER_STEP // PAGES_PER_BLOCK
    for t in range(bps):
        tot = sum(jnp.sum(k_refs[t * PAGES_PER_BLOCK + p][...], axis=0, keepdims=True)
                  for p in range(PAGES_PER_BLOCK))
        kmean_ref[pl.ds(j * bps + t, 1), :] = tot * (1.0 / MOBA_BLOCK)

    @pl.when(j == pl.num_programs(1) - 1)
    def _():
        prod = kmean_ref[...] * q_ref[0]
        head_of_lane = lax.broadcasted_iota(jnp.int32, (D_HALF, LANE), 0) // HEAD_DIM
        onehot = jnp.where(head_of_lane == lax.broadcasted_iota(jnp.int32, (D_HALF, LANE), 1), 1.0, 0.0)
        s = jnp.dot(prod, onehot, precision=HIGHEST, preferred_element_type=F32)
        row = lax.broadcasted_iota(jnp.int32, (nblk, LANE), 0)
        rank = jnp.zeros((nblk, LANE), jnp.int32)
        for i in range(nblk):
            si = s[i:i + 1, :]
            beats = jnp.logical_or(si > s, jnp.logical_and(si == s, i < row))
            rank = rank + beats.astype(jnp.int32)
        out_row = lax.broadcasted_iota(jnp.int32, (SUBLANE, LANE), 0)
        out = jnp.zeros((SUBLANE, LANE), jnp.int32)
        for slot in range(MOBA_TOPK):
            idx = jnp.sum(jnp.where(rank == slot, row, 0), axis=0, keepdims=True)
            out = jnp.where(out_row == slot, idx, out)
        idx_ref[0] = out


def _moba_pick_kernel(pg_ref, q_ref, kn_ref, vn_ref, *rest):
    n = HEADS_PER_TILE * MOBA_TOPK * PAGES_PER_BLOCK
    k_refs, v_refs, o_ref = rest[:n], rest[n:2 * n], rest[2 * n]
    q, kn, vn = q_ref[0], kn_ref[0], vn_ref[0]
    head_of_lane = lax.broadcasted_iota(jnp.int32, (1, LANE), 1) // HEAD_DIM
    out = jnp.zeros((1, LANE), F32)
    per_head = MOBA_TOPK * PAGES_PER_BLOCK
    for h in range(HEADS_PER_TILE):
        mine = head_of_lane == h
        qm = jnp.where(mine, q * ATTN_SCALE, 0.0)
        lg_self = jnp.sum(qm * kn, axis=-1, keepdims=True)
        lgs = [jnp.sum(k_refs[h * per_head + t][...] * qm, axis=-1, keepdims=True) for t in range(per_head)]
        m = lg_self
        for lg in lgs:
            m = jnp.maximum(m, jnp.max(lg, axis=0, keepdims=True))
        p_self = jnp.exp(lg_self - m)
        l = p_self
        acc = p_self * vn
        for t, lg in enumerate(lgs):
            p = jnp.exp(lg - m)
            l = l + jnp.sum(p, axis=0, keepdims=True)
            acc = acc + jnp.sum(p * v_refs[h * per_head + t][...], axis=0, keepdims=True)
        out = jnp.where(mine, acc / l, out)
    o_ref[0] = out


def _moba_step(proj, cache_k, cache_v, page_table):
    bsz = proj.shape[0]
    n_pages = page_table.shape[1]
    assert n_pages % MOBA_PAGES_PER_STEP == 0
    nblk = n_pages // PAGES_PER_BLOCK
    steps = n_pages // MOBA_PAGES_PER_STEP
    proj3 = proj.reshape(bsz, 1, -1)
    qoff, koff, voff = HEAD_TILES, 2 * HEAD_TILES, 3 * HEAD_TILES

    def page_spec(t):
        return pl.BlockSpec((None, PAGE_SIZE, D_HALF),
                            lambda b, j, pt: (pt[b, j * MOBA_PAGES_PER_STEP + t], 0, 0))

    idx = pl.pallas_call(
        functools.partial(_moba_rank_kernel, nblk=nblk),
        grid_spec=pltpu.PrefetchScalarGridSpec(
            num_scalar_prefetch=1,
            grid=(bsz, steps),
            in_specs=[pl.BlockSpec((1, 1, D_HALF), lambda b, j, pt: (b, 0, 1))]
            + [page_spec(t) for t in range(MOBA_PAGES_PER_STEP)],
            out_specs=pl.BlockSpec((1, SUBLANE, LANE), lambda b, j, pt: (b, 0, 0)),
            scratch_shapes=[pltpu.VMEM((nblk, D_HALF), F32)]),
        out_shape=jax.ShapeDtypeStruct((bsz, SUBLANE, LANE), jnp.int32),
        compiler_params=_cp("arbitrary", "arbitrary"),
    )(page_table, proj3, *([cache_k] * MOBA_PAGES_PER_STEP))

    top = idx[:, :MOBA_TOPK, :HEADS].transpose(0, 2, 1)
    logical = top[..., None] * PAGES_PER_BLOCK + jnp.arange(PAGES_PER_BLOCK, dtype=jnp.int32)
    pages = jnp.take_along_axis(page_table, logical.reshape(bsz, -1), axis=1).reshape(-1)
    per_tile = HEADS_PER_TILE * MOBA_TOPK * PAGES_PER_BLOCK

    def pick_spec(t):
        return pl.BlockSpec((None, PAGE_SIZE, LANE),
                            lambda b, ht, pg: (pg[(b * HEAD_TILES + ht) * per_tile + t], 0, ht))

    def tok_spec(off):
        return pl.BlockSpec((1, 1, LANE), lambda b, ht, pg: (b, 0, off + ht))

    out = pl.pallas_call(
        _moba_pick_kernel,
        grid_spec=pltpu.PrefetchScalarGridSpec(
            num_scalar_prefetch=1,
            grid=(bsz, HEAD_TILES),
            in_specs=[tok_spec(qoff), tok_spec(koff), tok_spec(voff)]
            + [pick_spec(t) for t in range(per_tile)] * 2,
            out_specs=pl.BlockSpec((1, 1, LANE), lambda b, ht, pg: (b, 0, ht))),
        out_shape=jax.ShapeDtypeStruct((bsz, 1, D_HALF), F32),
        compiler_params=_cp("arbitrary", "arbitrary"),
    )(pages, proj3, proj3, proj3, *([cache_k] * per_tile), *([cache_v] * per_tile))
    return out.reshape(bsz, D_HALF)


def _seg_sum(x):
    seg = lax.broadcasted_iota(jnp.int32, (LANE, LANE), 0) // HEAD_DIM
    ones = jnp.where(seg == lax.broadcasted_iota(jnp.int32, (LANE, LANE), 1) // HEAD_DIM, 1.0, 0.0)
    parts = [jnp.dot(x[:, c * LANE:(c + 1) * LANE], ones, precision=HIGHEST, preferred_element_type=F32)
             for c in range(x.shape[1] // LANE)]
    return jnp.concatenate(parts, axis=-1)


def _softplus(x):
    return jnp.maximum(x, 0.0) + jnp.log(1.0 + jnp.exp(-jnp.abs(x)))


def _rwkv_mix(cat, prev, p, vfirst):
    c = D_HALF
    dlt = prev - cat
    part = lambda t, j: t[:, j * c:(j + 1) * c]
    mu, mz = p['mu'][...], p['mz'][...]
    r = part(cat, 0) + part(dlt, 0) * mu[0:1]
    k = part(cat, 1) + part(dlt, 1) * mu[1:2]
    v = part(cat, 2) + part(dlt, 2) * mu[2:3]
    z0, dz = part(cat, 3), part(dlt, 3)
    zw, za, zg = z0 + dz * mz[0:1], z0 + dz * mz[1:2], z0 + dz * mz[2:3]
    w_raw = -_softplus(-(p['w0'][...] + _bdot(jnp.tanh(_bdot(zw, p['w1'][...])), p['w2'][...]))) - 0.5
    log_decay = -jnp.exp(w_raw)
    a = _sigmoid(p['a0'][...] + _bdot(_bdot(za, p['a1'][...]), p['a2'][...]))
    g = _bdot(_sigmoid(_bdot(zg, p['g1'][...])), p['g2'][...])
    if vfirst is not None:
        zv = z0 + dz * p['mu_v'][...]
        vmix = _sigmoid(p['v0'][...] + _bdot(_bdot(zv, p['v1'][...]), p['v2'][...]))
        v = v + (vfirst - v) * vmix
    kk = k * p['k_k'][...]
    kk = kk / jnp.maximum(jnp.sqrt(_seg_sum(kk * kk)), 1e-12)
    k = k * (1.0 + (a - 1.0) * p['k_a'][...])
    return r, log_decay, k, v, kk, kk * a, g


RWKV_PARAM_NAMES = ('mu', 'mz', 'w0', 'w1', 'w2', 'a0', 'a1', 'a2', 'g1', 'g2', 'k_k', 'k_a')
RWKV_VRES_NAMES = ('mu_v', 'v0', 'v1', 'v2')
RWKV_N_OUT = 7


def _rwkv_pre_kernel(*refs, names, has_vfirst, seq_tiles):
    refs = list(refs)
    cat_ref = refs.pop(0)
    prev_ref = refs.pop(0)
    vf_ref = refs.pop(0) if has_vfirst else None
    p = {n: refs.pop(0) for n in names}
    outs = refs
    cat = cat_ref[...]
    if seq_tiles:
        first = (pl.program_id(0) % seq_tiles) == 0
        row = lax.broadcasted_iota(jnp.int32, (cat.shape[0], 1), 0)
        before = jnp.where(first, 0.0, prev_ref[SUBLANE - 1:SUBLANE, :])
        prev = jnp.where(row == 0, before, pltpu.roll(cat, 1, axis=0))
    else:
        prev = prev_ref[...]
    res = _rwkv_mix(cat, prev, p, None if vf_ref is None else vf_ref[...])
    for o_ref, val in zip(outs, res):
        o_ref[...] = val


def _rwkv_pre(proj, prev, rp, vfirst, batch):
    m = proj.shape[0]
    cw = 4 * D_HALF
    names = RWKV_PARAM_NAMES + (RWKV_VRES_NAMES if vfirst is not None else ())
    params = [rp[n] for n in names]
    if prev is None:
        seq = m // batch
        tm = _row_tile(seq, 256)
        seq_tiles = seq // tm
        hb = tm // SUBLANE
        prev_arg = proj
        prev_spec = pl.BlockSpec((SUBLANE, cw), lambda i: (jnp.maximum(i * hb - 1, 0), 0))
    else:
        tm, seq_tiles = m, 0
        prev_arg = prev
        prev_spec = pl.BlockSpec((tm, cw), lambda i: (i, 0))
    args = [proj, prev_arg]
    in_specs = [pl.BlockSpec((tm, cw), lambda i: (i, 0)), prev_spec]
    if vfirst is not None:
        args.append(vfirst)
        in_specs.append(pl.BlockSpec((tm, D_HALF), lambda i: (i, 0)))
    args += params
    in_specs += [_full(a.shape) for a in params]
    return pl.pallas_call(
        functools.partial(_rwkv_pre_kernel, names=names, has_vfirst=vfirst is not None, seq_tiles=seq_tiles),
        grid=(m // tm,),
        in_specs=in_specs,
        out_specs=[pl.BlockSpec((tm, D_HALF), lambda i: (i, 0))] * RWKV_N_OUT,
        out_shape=[jax.ShapeDtypeStruct((m, D_HALF), F32)] * RWKV_N_OUT,
        compiler_params=_cp("parallel"))(*args)


WKV_CHUNK = 64


def _bmm(a, b):
    return lax.dot_general(a.astype(BF16), b.astype(BF16), (((2,), (1,)), ((0,), (0,))),
                           preferred_element_type=F32)


def _bmm3(a, b):
    ah = a.astype(BF16)
    al = (a - ah.astype(F32)).astype(BF16)
    bh = b.astype(BF16)
    bl = (b - bh.astype(F32)).astype(BF16)
    return _bmm(ah, bh) + (_bmm(ah, bl) + _bmm(al, bh))


def _wkv_tile(r, lw, k, v, kk, b, st):
    tt = r.shape[0]
    c, n, nh = WKV_CHUNK, HEAD_DIM, HEADS_PER_TILE
    nc = tt // c
    ri = lax.broadcasted_iota(jnp.int32, (1, c, c), 1)
    ci = lax.broadcasted_iota(jnp.int32, (1, c, c), 2)
    incl, strict = ri >= ci, ri > ci
    eye = jnp.where(ri == ci, 1.0, 0.0)
    tril = jnp.where(incl[0], 1.0, 0.0)
    lw3 = lw.reshape(nc, c, LANE)
    cum = jnp.stack([jnp.dot(tril, lw3[j], precision=HIGHEST, preferred_element_type=F32) for j in range(nc)])
    tot = cum[:, c - 1:c, :]
    e_neg = jnp.exp(-cum)
    e_end = jnp.exp(tot - cum)
    e_tot = jnp.exp(tot)

    def heads(x):
        return jnp.concatenate([x[:, :, h * n:(h + 1) * n] for h in range(nh)], axis=0)

    def chunks(x):
        return x.reshape(nc, c, LANE)

    kkm = heads(chunks(kk) * jnp.exp(cum - lw3))
    rp = heads(chunks(r) * jnp.exp(cum))
    bo, ko = heads(chunks(b) * e_neg), heads(chunks(k) * e_neg)
    bend, kend = heads(chunks(b) * e_end), heads(chunks(k) * e_end)
    vh = heads(chunks(v))
    e_tot = heads(e_tot)

    qa = lax.dot_general(jnp.concatenate([kkm, rp], axis=1).astype(BF16),
                         jnp.concatenate([bo, ko], axis=1).astype(BF16),
                         (((2,), (2,)), ((0,), (0,))), preferred_element_type=F32)
    a_ub = jnp.where(strict, qa[:, :c, :c], 0.0)
    a_vk = jnp.where(strict, qa[:, :c, c:], 0.0)
    a_rb = jnp.where(incl, qa[:, c:, :c], 0.0)
    a_rk = jnp.where(incl, qa[:, c:, c:], 0.0)

    m = 2
    t_inv = eye - jnp.where((ri // m) == (ci // m), a_ub, 0.0)
    while m < c:
        off = jnp.logical_and((ri // (2 * m)) == (ci // (2 * m)), (ri // m) != (ci // m))
        t_inv = t_inv - _bmm(_bmm(t_inv, jnp.where(off, a_ub, 0.0)), t_inv)
        m *= 2

    av = _bmm(jnp.concatenate([a_vk, a_rk], axis=1), vh)
    x1 = _bmm(t_inv, jnp.concatenate([kkm, av[:, :c]], axis=2))
    x2 = _bmm(a_rb, x1)
    x3 = _bmm(jnp.swapaxes(bend, 1, 2), x1)
    rr = rp - x2[:, :, :n]
    o_loc = av[:, c:] - x2[:, :, n:]
    trans = eye * e_tot - x3[:, :, :n]
    s_loc = _bmm(jnp.swapaxes(kend, 1, 2), vh) - x3[:, :, n:]

    by_head = lambda x: x.reshape(nh, nc, x.shape[1], x.shape[2])
    rr, o_loc, trans, s_loc = by_head(rr), by_head(o_loc), by_head(trans), by_head(s_loc)
    outs = []
    for j in range(nc):
        outs.append(_bmm(rr[:, j], st) + o_loc[:, j])
        st = _bmm3(trans[:, j], st) + s_loc[:, j]
    o = jnp.stack(outs, axis=1).reshape(nh, tt, n)
    return jnp.concatenate([o[h] for h in range(nh)], axis=-1), st


WKV_TIME_TILE = 512


def _seg_mean(x):
    return _seg_sum(x) * (1.0 / HEAD_DIM)


def _wkv_seq_kernel(r_ref, lw_ref, k_ref, v_ref, kk_ref, b_ref, g_ref, rk_ref, lnw_ref, lnb_ref,
                    o_ref, so_ref, st_ref):
    i = pl.program_id(2)

    @pl.when(i == 0)
    def _():
        st_ref[...] = jnp.zeros_like(st_ref)

    r, k, v = r_ref[...], k_ref[...], v_ref[...]
    o, st = _wkv_tile(r, lw_ref[...], k, v, kk_ref[...], b_ref[...], st_ref[...])
    st_ref[...] = st
    mean = _seg_mean(o)
    var = _seg_mean(jnp.square(o - mean))
    on = (o - mean) * lax.rsqrt(var + RWKV_GN_EPS) * lnw_ref[...] + lnb_ref[...]
    bonus = _seg_sum(r * k * rk_ref[...]) * v
    o_ref[...] = (on + bonus) * g_ref[...]
    for h in range(HEADS_PER_TILE):
        so_ref[0, h] = st[h].T


def _wkv_seq(r, lw, k, v, kk, b, g, r_k, ln_w, ln_b, batch):
    m = r.shape[0]
    seq = m // batch
    tt = _row_tile(seq, WKV_TIME_TILE)
    assert tt % WKV_CHUNK == 0
    nt = seq // tt
    row_spec = pl.BlockSpec((tt, LANE), lambda bi, t, i: (bi * nt + i, t))
    par_spec = pl.BlockSpec((1, LANE), lambda bi, t, i: (0, t))
    return pl.pallas_call(
        _wkv_seq_kernel,
        grid=(batch, HEAD_TILES, nt),
        in_specs=[row_spec] * 7 + [par_spec] * 3,
        out_specs=[row_spec,
                   pl.BlockSpec((1, HEADS_PER_TILE, HEAD_DIM, HEAD_DIM), lambda bi, t, i: (bi, t, 0, 0))],
        out_shape=[jax.ShapeDtypeStruct((m, D_HALF), F32),
                   jax.ShapeDtypeStruct((batch, HEADS, HEAD_DIM, HEAD_DIM), F32)],
        scratch_shapes=[pltpu.VMEM((HEADS_PER_TILE, HEAD_DIM, HEAD_DIM), F32)],
        compiler_params=_cp("parallel", "parallel", "arbitrary"),
    )(r, lw, k, v, kk, b, g, r_k.reshape(1, -1), ln_w.reshape(1, -1), ln_b.reshape(1, -1))


def _wkv_finish(o, r, k, v, g, rk, lnw, lnb):
    mean = jnp.mean(o, axis=-1, keepdims=True)
    var = jnp.mean(jnp.square(o - mean), axis=-1, keepdims=True)
    on = (o - mean) * lax.rsqrt(var + RWKV_GN_EPS) * lnw + lnb
    bonus = jnp.sum(r * k * rk, axis=-1, keepdims=True) * v
    return (on + bonus) * g


def _wkv_step_kernel(s_ref, r_ref, lw_ref, k_ref, v_ref, kk_ref, b_ref, g_ref, rk_ref, lnw_ref, lnb_ref,
                     o_ref, so_ref):
    ri = lax.broadcasted_iota(jnp.int32, (HEAD_DIM, HEAD_DIM), 0)
    eye = jnp.where(ri == lax.broadcasted_iota(jnp.int32, (HEAD_DIM, HEAD_DIM), 1), 1.0, 0.0)
    outs = []
    for h in range(HEADS):
        sl = slice(h * HEAD_DIM, (h + 1) * HEAD_DIM)
        r, lw, k, v, kk, b, g = [ref[0][:, sl] for ref in (r_ref, lw_ref, k_ref, v_ref, kk_ref, b_ref, g_ref)]
        s = s_ref[0, h]
        sa = -jnp.sum(s * kk, axis=-1, keepdims=True)
        v_col = jnp.sum(eye * v, axis=-1, keepdims=True)
        s = s * jnp.exp(lw) + sa * b + v_col * k
        so_ref[0, h] = s
        o_col = jnp.sum(s * r, axis=-1, keepdims=True)
        o = jnp.sum(eye * o_col, axis=0, keepdims=True)
        outs.append(_wkv_finish(o, r, k, v, g, rk_ref[:, sl], lnw_ref[:, sl], lnb_ref[:, sl]))
    o_ref[0] = jnp.concatenate(outs, axis=-1)


def _wkv_step(state, r, lw, k, v, kk, b, g, r_k, ln_w, ln_b):
    bsz = r.shape[0]
    rows = [t.reshape(bsz, 1, D_HALF) for t in (r, lw, k, v, kk, b, g)]
    row_spec = pl.BlockSpec((1, 1, D_HALF), lambda i: (i, 0, 0))
    st_spec = pl.BlockSpec((1, HEADS, HEAD_DIM, HEAD_DIM), lambda i: (i, 0, 0, 0))
    out, st = pl.pallas_call(
        _wkv_step_kernel,
        grid=(bsz,),
        in_specs=[st_spec] + [row_spec] * 7 + [_full((1, D_HALF))] * 3,
        out_specs=[row_spec, st_spec],
        out_shape=[jax.ShapeDtypeStruct((bsz, 1, D_HALF), F32),
                   jax.ShapeDtypeStruct(state.shape, F32)],
        compiler_params=_cp("parallel"),
    )(state, *rows, r_k.reshape(1, -1), ln_w.reshape(1, -1), ln_b.reshape(1, -1))
    return out.reshape(bsz, D_HALF), st


CONV_HALO = 32


def _layernorm_silu(y, g, b):
    m = jnp.mean(y, axis=-1, keepdims=True)
    var = jnp.mean(jnp.square(y - m), axis=-1, keepdims=True)
    return _silu((y - m) * lax.rsqrt(var + LN_EPS) * g + b)


def _convd_seq_kernel(gv_ref, gg_ref, hv_ref, hg_ref, w_ref, b_ref, lng_ref, lnb_ref,
                      o_ref, tail_ref, ext_ref, *, seq_tiles):
    tt = gv_ref.shape[0]
    first = (pl.program_id(0) % seq_tiles) == 0
    ext_ref[0:CONV_HALO, :] = jnp.where(first, 0.0, hv_ref[...] * _sigmoid(hg_ref[...]))
    ext_ref[CONV_HALO:, :] = gv_ref[...] * _sigmoid(gg_ref[...])
    off = CONV_HALO - (CONV_D_WIDTH - 1)
    acc = jnp.zeros((tt, D_HALF), F32)
    for j in range(CONV_D_WIDTH):
        acc = acc + w_ref[j:j + 1, :] * ext_ref[off + j:off + j + tt, :]
    o_ref[...] = _layernorm_silu(acc + b_ref[...], lng_ref[...], lnb_ref[...])
    tail_ref[0] = ext_ref[tt:, :]


def _convd_seq(proj, batch, w, b, ln_g, ln_b):
    m = proj.shape[0]
    seq = m // batch
    tt = _row_tile(seq, 256)
    seq_tiles = seq // tt
    hb = tt // CONV_HALO
    vcol, gcol = 4, 5
    tile = lambda col: pl.BlockSpec((tt, D_HALF), lambda i: (i, col))
    halo = lambda col: pl.BlockSpec((CONV_HALO, D_HALF), lambda i: (jnp.maximum(i * hb - 1, 0), col))
    params = [w, b.reshape(1, -1), ln_g.reshape(1, -1), ln_b.reshape(1, -1)]
    out, tail = pl.pallas_call(
        functools.partial(_convd_seq_kernel, seq_tiles=seq_tiles),
        grid=(m // tt,),
        in_specs=[tile(vcol), tile(gcol), halo(vcol), halo(gcol)] + [_full(p.shape) for p in params],
        out_specs=[pl.BlockSpec((tt, D_HALF), lambda i: (i, 0)),
                   pl.BlockSpec((1, CONV_HALO, D_HALF), lambda i: (i, 0, 0))],
        out_shape=[jax.ShapeDtypeStruct((m, D_HALF), F32),
                   jax.ShapeDtypeStruct((m // tt, CONV_HALO, D_HALF), F32)],
        scratch_shapes=[pltpu.VMEM((tt + CONV_HALO, D_HALF), F32)],
        compiler_params=_cp("parallel"))(proj, proj, proj, proj, *params)
    return out, tail[seq_tiles - 1::seq_tiles, CONV_HALO - (CONV_D_WIDTH - 1):]


def _convd_step_kernel(gv_ref, gg_ref, buf_ref, w_ref, b_ref, lng_ref, lnb_ref, o_ref, u_ref):
    u = gv_ref[...] * _sigmoid(gg_ref[...])
    u_ref[...] = u
    acc = u * w_ref[CONV_D_WIDTH - 1:CONV_D_WIDTH, :]
    for j in range(CONV_D_WIDTH - 1):
        acc = acc + w_ref[j:j + 1, :] * buf_ref[j]
    o_ref[...] = _layernorm_silu(acc + b_ref[...], lng_ref[...], lnb_ref[...])


def _convd_step(proj, buf, w, b, ln_g, ln_b):
    m = proj.shape[0]
    buf_t = jnp.swapaxes(buf, 0, 1)
    params = [w, b.reshape(1, -1), ln_g.reshape(1, -1), ln_b.reshape(1, -1)]
    out, u = pl.pallas_call(
        _convd_step_kernel,
        grid=(1,),
        in_specs=[pl.BlockSpec((m, D_HALF), lambda i: (0, 4)), pl.BlockSpec((m, D_HALF), lambda i: (0, 5)),
                  _full(buf_t.shape)] + [_full(p.shape) for p in params],
        out_specs=[_full((m, D_HALF)), _full((m, D_HALF))],
        out_shape=[jax.ShapeDtypeStruct((m, D_HALF), F32)] * 2,
        compiler_params=_cp("arbitrary"))(proj, proj, buf_t, *params)
    return out, jnp.concatenate([buf[:, 1:], u[:, None]], axis=1)


def _rmsnorm_kernel(x_ref, g_ref, o_ref):
    x = x_ref[...]
    ms = jnp.mean(x * x, axis=-1, keepdims=True)
    o_ref[...] = x * lax.rsqrt(ms + RMS_EPS) * g_ref[...]


def _rmsnorm(x, g):
    m, d = x.shape
    tm = _row_tile(m, 512)
    return pl.pallas_call(
        _rmsnorm_kernel,
        grid=(m // tm,),
        in_specs=[pl.BlockSpec((tm, d), lambda i: (i, 0)), _full((1, d))],
        out_specs=pl.BlockSpec((tm, d), lambda i: (i, 0)),
        out_shape=jax.ShapeDtypeStruct((m, d), F32),
        compiler_params=_cp("parallel"))(x, g.reshape(1, d))


def _kv_layout_kernel(*refs, n_layers):
    k_refs, v_refs = refs[:n_layers], refs[n_layers:2 * n_layers]
    ko_ref, vo_ref = refs[2 * n_layers:]
    layer = pl.program_id(0)
    for l in range(n_layers):
        @pl.when(layer == l)
        def _():
            ko_ref[...] = k_refs[l][...].reshape(ko_ref.shape)
            vo_ref[...] = v_refs[l][...].reshape(vo_ref.shape)


def _kv_layout(projs):
    n = len(projs)
    m = projs[0].shape[0]
    tm = _row_tile(m, 512)
    nt = m // tm

    def col_spec(l, col):
        return pl.BlockSpec((tm, D_HALF), lambda layer, i: (jnp.where(layer == l, i, 0), col))

    out_spec = pl.BlockSpec((tm, HEADS, HEAD_DIM), lambda layer, i: (layer * nt + i, 0, 0))
    shp = jax.ShapeDtypeStruct((n * m, HEADS, HEAD_DIM), F32)
    return pl.pallas_call(
        functools.partial(_kv_layout_kernel, n_layers=n),
        grid=(n, nt),
        in_specs=[col_spec(l, 2) for l in range(n)] + [col_spec(l, 3) for l in range(n)],
        out_specs=[out_spec, out_spec],
        out_shape=[shp, shp],
        compiler_params=_cp("parallel", "parallel"))(*projs, *projs)


def _rwkv_params(p, li):
    row = lambda t: t.reshape(1, -1)
    rp = dict(mu=p['rwkv_mu_rkv'][li], mz=p['rwkv_mu_z'][li], w0=row(p['rwkv_w0'][li]),
              w1=p['rwkv_w1'][li].astype(BF16), w2=p['rwkv_w2'][li].astype(BF16),
              a0=row(p['rwkv_a0'][li]), a1=p['rwkv_a1'][li].astype(BF16), a2=p['rwkv_a2'][li].astype(BF16),
              g1=p['rwkv_g1'][li].astype(BF16), g2=p['rwkv_g2'][li].astype(BF16),
              k_k=row(p['rwkv_k_k'][li]), k_a=row(p['rwkv_k_a'][li]))
    if li > 0:
        rp.update(mu_v=row(p['rwkv_mu_v'][li - 1]), v0=row(p['rwkv_v0'][li - 1]),
                  v1=p['rwkv_v1'][li - 1].astype(BF16), v2=p['rwkv_v2'][li - 1].astype(BF16))
    return rp


def _prepare(p):
    depth = p['g_mix'].shape[0]
    layers = []
    for i in range(depth):
        li = i // 2
        lay = dict(g_mix=p['g_mix'][i], g_ffn=p['g_ffn'][i],
                   w_up=p['ffn_w_up'][i].astype(BF16), w_down=p['ffn_w_down'][i].astype(BF16),
                   ffn_cw=p['ffn_conv_w'][i], ffn_cb=p['ffn_conv_b'][i])
        if i % 2 == 0:
            lay.update(w_in=p['w_in_ab'][li].astype(BF16), w_out=p['w_out_ab'][li].astype(BF16),
                       s5=_s5_params(p['s5_a_re'][li], p['s5_a_im'][li], p['s5_log_step'][li],
                                     p['s5_b_re'][li], p['s5_b_im'][li], p['s5_c_re'][li], p['s5_c_im'][li],
                                     p['s5_d'][li]),
                       w_glu=p['s5_w_glu'][li].astype(BF16), b_glu=p['s5_b_glu'][li])
        else:
            lay.update(w_in=p['w_in_cd'][li].astype(BF16), w_out=p['w_out_cd'][li].astype(BF16),
                       rwkv=_rwkv_params(p, li), r_k=p['rwkv_r_k'][li], ln_w=p['rwkv_ln_w'][li],
                       ln_b=p['rwkv_ln_b'][li], cd_w=p['conv_d_w'][li], cd_b=p['conv_d_b'][li],
                       cd_g=p['conv_d_ln_g'][li], cd_lb=p['conv_d_ln_b'][li])
        layers.append(lay)
    return layers


def _trunk_seq(x3, layers, g_final):
    batch, seq, d = x3.shape
    x = x3.reshape(batch * seq, d)
    rope = _rope_tables(seq, 0, True)
    attn_projs, s5r, s5i, wkvs, shifts, convs, ffns = [], [], [], [], [], [], []
    v_first = None
    for i, lay in enumerate(layers):
        if i % 2 == 0:
            proj = _norm_matmul(x, lay['g_mix'], lay['w_in'], rope=rope, rope_cols=(1, 2))
            a_out, hr, hi = _s5_seq(proj, batch, lay['s5'], lay['w_glu'], lay['b_glu'])
            b_out = _moba_seq(proj, batch)
            attn_projs.append(proj)
            s5r.append(hr)
            s5i.append(hi)
            x = _out_proj(a_out, b_out, lay['w_out'], x)
        else:
            proj = _norm_matmul(x, lay['g_mix'], lay['w_in'])
            r, lw, k, v, kk, b, g = _rwkv_pre(proj, None, lay['rwkv'], v_first, batch)
            if v_first is None:
                v_first = v
            c_out, s_fin = _wkv_seq(r, lw, k, v, kk, b, g, lay['r_k'], lay['ln_w'], lay['ln_b'], batch)
            d_out, cbuf = _convd_seq(proj, batch, lay['cd_w'], lay['cd_b'], lay['cd_g'], lay['cd_lb'])
            wkvs.append(s_fin)
            shifts.append(proj.reshape(batch, seq, -1)[:, -1, :4 * D_HALF])
            convs.append(cbuf)
            x = _out_proj(c_out, d_out, lay['w_out'], x)
        x, fbuf = _ffn_seq(x, batch, lay['g_ffn'], lay['w_up'], lay['ffn_cw'], lay['ffn_cb'], lay['w_down'])
        ffns.append(fbuf)
    y = _rmsnorm(x, g_final).reshape(batch, seq, d)
    k_all, v_all = _kv_layout(attn_projs)
    return (y, k_all, v_all, jnp.stack(s5r), jnp.stack(s5i), jnp.stack(wkvs),
            jnp.stack(shifts), jnp.stack(convs), jnp.stack(ffns))


def _trunk_step(x3, pos0, layers, g_final, cache_k, cache_v, page_table, s5_re0, s5_im0, wkv0, shift0,
                convd0, ffn0):
    bsz, _, d = x3.shape
    x = x3.reshape(bsz, d)
    rope = _rope_tables(bsz, pos0, False)
    n_pool = cache_k.shape[1]
    attn_projs, s5r, s5i, wkvs, shifts, convs, ffns = [], [], [], [], [], [], []
    v_first = None
    for i, lay in enumerate(layers):
        li = i // 2
        if i % 2 == 0:
            proj = _norm_matmul(x, lay['g_mix'], lay['w_in'], rope=rope, rope_cols=(1, 2))
            a_out, hr, hi = _s5_step(proj, s5_re0[li].reshape(bsz, -1), s5_im0[li].reshape(bsz, -1),
                                     lay['s5'], lay['w_glu'], lay['b_glu'])
            b_out = _moba_step(proj, cache_k[li].reshape(n_pool, PAGE_SIZE, D_HALF),
                               cache_v[li].reshape(n_pool, PAGE_SIZE, D_HALF), page_table)
            attn_projs.append(proj)
            s5r.append(hr)
            s5i.append(hi)
            x = _out_proj(a_out, b_out, lay['w_out'], x)
        else:
            proj = _norm_matmul(x, lay['g_mix'], lay['w_in'])
            r, lw, k, v, kk, b, g = _rwkv_pre(proj, shift0[li], lay['rwkv'], v_first, bsz)
            if v_first is None:
                v_first = v
            c_out, s_fin = _wkv_step(wkv0[li], r, lw, k, v, kk, b, g, lay['r_k'], lay['ln_w'], lay['ln_b'])
            d_out, cbuf = _convd_step(proj, convd0[li], lay['cd_w'], lay['cd_b'], lay['cd_g'], lay['cd_lb'])
            wkvs.append(s_fin)
            shifts.append(proj[:, :4 * D_HALF])
            convs.append(cbuf)
            x = _out_proj(c_out, d_out, lay['w_out'], x)
        x, fbuf = _ffn_step(x, ffn0[i], lay['g_ffn'], lay['w_up'], lay['ffn_cw'], lay['ffn_cb'], lay['w_down'])
        ffns.append(fbuf)
    y = _rmsnorm(x, g_final).reshape(bsz, 1, d)
    k_all, v_all = _kv_layout(attn_projs)
    return (y, k_all, v_all, jnp.stack(s5r), jnp.stack(s5i), jnp.stack(wkvs),
            jnp.stack(shifts), jnp.stack(convs), jnp.stack(ffns))


def kernel(x_prompt, x_sample, cache_k_moba, cache_v_moba, page_table, state_s5_re, state_s5_im,
           state_rwkv_wkv, state_rwkv_shift, state_conv_d, state_ffn_conv, g_mix, g_ffn, g_final,
           w_in_ab, w_out_ab, s5_a_re, s5_a_im, s5_log_step, s5_b_re, s5_b_im, s5_c_re, s5_c_im, s5_d,
           s5_w_glu, s5_b_glu, w_in_cd, w_out_cd, rwkv_mu_rkv, rwkv_mu_z, rwkv_w0, rwkv_w1, rwkv_w2,
           rwkv_a0, rwkv_a1, rwkv_a2, rwkv_g1, rwkv_g2, rwkv_k_k, rwkv_k_a, rwkv_r_k, rwkv_ln_w, rwkv_ln_b,
           rwkv_mu_v, rwkv_v0, rwkv_v1, rwkv_v2, conv_d_w, conv_d_b, conv_d_ln_g, conv_d_ln_b,
           ffn_w_up, ffn_conv_w, ffn_conv_b, ffn_w_down):
    p = dict(g_mix=g_mix, g_ffn=g_ffn, w_in_ab=w_in_ab, w_out_ab=w_out_ab,
             s5_a_re=s5_a_re, s5_a_im=s5_a_im, s5_log_step=s5_log_step, s5_b_re=s5_b_re, s5_b_im=s5_b_im,
             s5_c_re=s5_c_re, s5_c_im=s5_c_im, s5_d=s5_d, s5_w_glu=s5_w_glu, s5_b_glu=s5_b_glu,
             w_in_cd=w_in_cd, w_out_cd=w_out_cd, rwkv_mu_rkv=rwkv_mu_rkv, rwkv_mu_z=rwkv_mu_z,
             rwkv_w0=rwkv_w0, rwkv_w1=rwkv_w1, rwkv_w2=rwkv_w2, rwkv_a0=rwkv_a0, rwkv_a1=rwkv_a1,
             rwkv_a2=rwkv_a2, rwkv_g1=rwkv_g1, rwkv_g2=rwkv_g2, rwkv_k_k=rwkv_k_k, rwkv_k_a=rwkv_k_a,
             rwkv_r_k=rwkv_r_k.reshape(rwkv_r_k.shape[0], -1), rwkv_ln_w=rwkv_ln_w, rwkv_ln_b=rwkv_ln_b,
             rwkv_mu_v=rwkv_mu_v, rwkv_v0=rwkv_v0, rwkv_v1=rwkv_v1, rwkv_v2=rwkv_v2,
             conv_d_w=conv_d_w, conv_d_b=conv_d_b, conv_d_ln_g=conv_d_ln_g, conv_d_ln_b=conv_d_ln_b,
             ffn_w_up=ffn_w_up, ffn_conv_w=ffn_conv_w, ffn_conv_b=ffn_conv_b, ffn_w_down=ffn_w_down)
    layers = _prepare(p)
    bp, seq, _ = x_prompt.shape
    bs = x_sample.shape[0]
    n_ab = w_in_ab.shape[0]
    n_cd = w_in_cd.shape[0]
    page = cache_k_moba.shape[2]
    past_len = page_table.shape[1] * page
    (y_p, k_p, v_p, s5r_p, s5i_p, wkv_p, sh_p, cd_p, ff_p) = _trunk_seq(x_prompt, layers, g_final)
    (y_s, k_s, v_s, s5r_s, s5i_s, wkv_s, sh_s, cd_s, ff_s) = _trunk_step(
        x_sample, past_len, layers, g_final, cache_k_moba, cache_v_moba, page_table, state_s5_re, state_s5_im,
        state_rwkv_wkv, state_rwkv_shift, state_conv_d, state_ffn_conv)
    kv_p = (n_ab, bp, seq // page, page, HEADS, HEAD_DIM)
    kv_s = (n_ab, bs, 1, HEADS, HEAD_DIM)
    s5_p = (n_ab, bp, S5_GROUPS, S5_STATE)
    s5_s = (n_ab, bs, S5_GROUPS, S5_STATE)
    return (y_p, y_s, k_p.reshape(kv_p), v_p.reshape(kv_p), k_s.reshape(kv_s), v_s.reshape(kv_s),
            s5r_p.reshape(s5_p), s5i_p.reshape(s5_p), s5r_s.reshape(s5_s), s5i_s.reshape(s5_s),
            wkv_p, wkv_s, sh_p, sh_s, cd_p, cd_s, ff_p, ff_s)
```

```python
import functools
import math

import numpy as np
import jax
import jax.numpy as jnp
from jax import lax
from jax.experimental import pallas as pl
from jax.experimental.pallas import tpu as pltpu

F32 = jnp.float32
BF16 = jnp.bfloat16
HIGHEST = lax.Precision.HIGHEST

D_MODEL = 1024
D_HALF = 512
S5_GROUP = 16
S5_GROUPS = 32
S5_STATE = 64
S5_LANES = S5_GROUPS * S5_STATE
HEADS = 8
HEAD_DIM = 64
MOBA_BLOCK = 256
MOBA_TOPK = 3
PAGE_SIZE = 128
ROPE_DIM = 16
ROPE_THETA = 500000.0
RWKV_GN_EPS = 64e-5
CONV_D_WIDTH = 31
D_FF = 2816
RMS_EPS = 1e-6
LN_EPS = 1e-5
LANE = 128
SUBLANE = 8
VMEM_LIMIT = 48 * 1024 * 1024
NEG = -1e30


def _cp(*sem):
    return pltpu.CompilerParams(dimension_semantics=sem, vmem_limit_bytes=VMEM_LIMIT)


def _bdot(a, b):
    return jnp.dot(a.astype(BF16), b.astype(BF16), preferred_element_type=F32)


def _sigmoid(x):
    return 1.0 / (1.0 + jnp.exp(-x))


def _silu(x):
    return x * _sigmoid(x)


def _row_tile(m, pref):
    return pref if m % pref == 0 else m


def _rope_table_kernel(inv_ref, cos_ref, sa_ref, sb_ref, *, pos0, per_row_pos):
    rows = cos_ref.shape[0]
    lane = lax.broadcasted_iota(jnp.int32, (rows, LANE), 1) % HEAD_DIM
    if per_row_pos:
        pos = (pos0 + lax.broadcasted_iota(jnp.int32, (rows, LANE), 0)).astype(F32)
    else:
        pos = jnp.full((rows, LANE), pos0, F32)
    ang = pos * inv_ref[...]
    c, s = jnp.cos(ang), jnp.sin(ang)
    cos_ref[...] = jnp.where(lane < ROPE_DIM, c, 1.0)
    sa_ref[...] = jnp.where(lane < ROPE_DIM // 2, -s, 0.0)
    sb_ref[...] = jnp.where((lane >= ROPE_DIM // 2) & (lane < ROPE_DIM), s, 0.0)


def _rope_tables(rows, pos0, per_row_pos):
    half = ROPE_DIM // 2
    inv8 = (np.float32(1.0) / (np.float32(ROPE_THETA) ** (np.arange(half, dtype=np.float32) / np.float32(half))))
    inv = np.zeros((HEAD_DIM,), np.float32)
    inv[:half] = inv8
    inv[half:ROPE_DIM] = inv8
    inv = jnp.asarray(np.tile(inv, LANE // HEAD_DIM)[None, :])
    shp = jax.ShapeDtypeStruct((rows, LANE), F32)
    return pl.pallas_call(
        functools.partial(_rope_table_kernel, pos0=pos0, per_row_pos=per_row_pos),
        out_shape=(shp, shp, shp))(inv)


def _norm_matmul_kernel(x_ref, g_ref, w_ref, *rest, rope_cols):
    if rope_cols:
        cos_ref, sa_ref, sb_ref, o_ref = rest
    else:
        (o_ref,) = rest
    x = x_ref[...]
    ms = jnp.mean(x * x, axis=-1, keepdims=True)
    h = (x * lax.rsqrt(ms + RMS_EPS) * g_ref[...]).astype(BF16)
    tn = D_HALF
    for j in range(o_ref.shape[1] // tn):
        y = jnp.dot(h, w_ref[:, j * tn:(j + 1) * tn], preferred_element_type=F32)
        if j not in rope_cols:
            o_ref[:, j * tn:(j + 1) * tn] = y
            continue
        cos, sa, sb = cos_ref[...], sa_ref[...], sb_ref[...]
        for c in range(tn // LANE):
            yc = y[:, c * LANE:(c + 1) * LANE]
            o_ref[:, j * tn + c * LANE:j * tn + (c + 1) * LANE] = (
                yc * cos + pltpu.roll(yc, LANE - ROPE_DIM // 2, axis=1) * sa
                + pltpu.roll(yc, ROPE_DIM // 2, axis=1) * sb)


def _norm_matmul(x, g, w_bf, rope=None, rope_cols=()):
    m, d = x.shape
    n = w_bf.shape[1]
    tm = _row_tile(m, 512)
    in_specs = [pl.BlockSpec((tm, d), lambda i: (i, 0)),
                pl.BlockSpec((1, d), lambda i: (0, 0)),
                pl.BlockSpec((d, n), lambda i: (0, 0))]
    args = [x, g.reshape(1, d), w_bf]
    if rope_cols:
        nt = rope[0].shape[0] // tm
        for t in rope:
            in_specs.append(pl.BlockSpec((tm, LANE), lambda i: (i % nt, 0)))
            args.append(t)
    return pl.pallas_call(
        functools.partial(_norm_matmul_kernel, rope_cols=tuple(rope_cols)),
        grid=(m // tm,),
        in_specs=in_specs,
        out_specs=pl.BlockSpec((tm, n), lambda i: (i, 0)),
        out_shape=jax.ShapeDtypeStruct((m, n), F32),
        compiler_params=_cp("parallel"))(*args)


def _out_proj_kernel(a_ref, b_ref, wa_ref, wb_ref, res_ref, o_ref):
    o_ref[...] = res_ref[...] + (_bdot(a_ref[...], wa_ref[...]) + _bdot(b_ref[...], wb_ref[...]))


def _out_proj(a, b, w_bf, res):
    m, c = a.shape
    n = w_bf.shape[1]
    tm = _row_tile(m, 512)
    return pl.pallas_call(
        _out_proj_kernel,
        grid=(m // tm,),
        in_specs=[pl.BlockSpec((tm, c), lambda i: (i, 0)),
                  pl.BlockSpec((tm, c), lambda i: (i, 0)),
                  pl.BlockSpec((c, n), lambda i: (0, 0)),
                  pl.BlockSpec((c, n), lambda i: (1, 0)),
                  pl.BlockSpec((tm, n), lambda i: (i, 0))],
        out_specs=pl.BlockSpec((tm, n), lambda i: (i, 0)),
        out_shape=jax.ShapeDtypeStruct((m, n), F32),
        compiler_params=_cp("parallel"))(a, b, w_bf, w_bf, res)


FFN_TN = 256
FFN_HALO = SUBLANE


def _ffn_seq_kernel(x_ref, xh_ref, g_ref, wa_ref, wb_ref, cwa_ref, cwb_ref, cba_ref, cbb_ref, wd_ref,
                    o_ref, ua_ref, ub_ref, h_ref, acc_ref, a0_ref, a1_ref, b0_ref, b1_ref, *, tiles_per_seq):
    i, c = pl.program_id(0), pl.program_id(1)
    tm = x_ref.shape[0]
    first = (i % tiles_per_seq) == 0

    @pl.when(c == 0)
    def _():
        def norm(x):
            ms = jnp.mean(x * x, axis=-1, keepdims=True)
            return (x * lax.rsqrt(ms + RMS_EPS) * g_ref[...]).astype(BF16)
        h_ref[0:FFN_HALO, :] = jnp.where(first, jnp.zeros((), BF16), norm(xh_ref[...]))
        h_ref[FFN_HALO:, :] = norm(x_ref[...])
        acc_ref[...] = jnp.zeros_like(acc_ref)
        a1_ref[...] = jnp.zeros_like(a1_ref)
        b1_ref[...] = jnp.zeros_like(b1_ref)

    def trip(new_a, new_b, old_a, old_b):
        def branch(us_ref, cw_ref, cb_ref, tail_ref):
            u = us_ref[...]
            tail_ref[0] = u[tm + FFN_HALO - 2:, :]
            cw = cw_ref[...]
            y = (u * cw[2:3, :] + pltpu.roll(u, 1, axis=0) * cw[1:2, :]
                 + pltpu.roll(u, 2, axis=0) * cw[0:1, :])
            return y[FFN_HALO:, :] + cb_ref[...]

        h = h_ref[...]
        new_a[...] = jnp.dot(h, wa_ref[...], preferred_element_type=F32)
        a = branch(old_a, cwa_ref, cba_ref, ua_ref)
        new_b[...] = jnp.dot(h, wb_ref[...], preferred_element_type=F32)
        b = branch(old_b, cwb_ref, cbb_ref, ub_ref)
        gated = jnp.where(c > 0, _silu(a) * b, 0.0)
        acc_ref[...] += _bdot(gated, wd_ref[...])

    @pl.when(c % 2 == 0)
    def _():
        trip(a0_ref, b0_ref, a1_ref, b1_ref)

    @pl.when(c % 2 == 1)
    def _():
        trip(a1_ref, b1_ref, a0_ref, b0_ref)

    @pl.when(c == pl.num_programs(1) - 1)
    def _():
        o_ref[...] = x_ref[...] + acc_ref[...]


def _ffn_seq(x, batch, g, w_up_bf, cw, cb, w_down_bf):
    m, d = x.shape
    seq = m // batch
    tm = _row_tile(seq, 512)
    tps = seq // tm
    nc = D_FF // FFN_TN
    hb = tm // FFN_HALO
    up = lambda c: jnp.minimum(c, nc - 1)
    dn = lambda c: jnp.maximum(c - 1, 0)
    out, ua, ub = pl.pallas_call(
        functools.partial(_ffn_seq_kernel, tiles_per_seq=tps),
        grid=(m // tm, nc + 1),
        in_specs=[pl.BlockSpec((tm, d), lambda i, c: (i, 0)),
                  pl.BlockSpec((FFN_HALO, d), lambda i, c: (jnp.maximum(i * hb - 1, 0), 0)),
                  pl.BlockSpec((1, d), lambda i, c: (0, 0)),
                  pl.BlockSpec((d, FFN_TN), lambda i, c: (0, up(c))),
                  pl.BlockSpec((d, FFN_TN), lambda i, c: (0, nc + up(c))),
                  pl.BlockSpec((3, FFN_TN), lambda i, c: (0, dn(c))),
                  pl.BlockSpec((3, FFN_TN), lambda i, c: (0, nc + dn(c))),
                  pl.BlockSpec((1, FFN_TN), lambda i, c: (0, dn(c))),
                  pl.BlockSpec((1, FFN_TN), lambda i, c: (0, nc + dn(c))),
                  pl.BlockSpec((FFN_TN, d), lambda i, c: (dn(c), 0))],
        out_specs=[pl.BlockSpec((tm, d), lambda i, c: (i, 0)),
                   pl.BlockSpec((1, 2, FFN_TN), lambda i, c: (i, 0, dn(c))),
                   pl.BlockSpec((1, 2, FFN_TN), lambda i, c: (i, 0, dn(c)))],
        out_shape=[jax.ShapeDtypeStruct((m, d), F32),
                   jax.ShapeDtypeStruct((m // tm, 2, D_FF), F32),
                   jax.ShapeDtypeStruct((m // tm, 2, D_FF), F32)],
        scratch_shapes=[pltpu.VMEM((tm + FFN_HALO, d), BF16), pltpu.VMEM((tm, d), F32),
                        ] + [pltpu.VMEM((tm + FFN_HALO, FFN_TN), F32)] * 4,
        compiler_params=_cp("arbitrary", "arbitrary"),
    )(x, x, g.reshape(1, d), w_up_bf, w_up_bf, cw, cw, cb.reshape(1, -1), cb.reshape(1, -1), w_down_bf)
    return out, jnp.concatenate([ua[tps - 1::tps], ub[tps - 1::tps]], axis=-1)


def _ffn_step_kernel(x_ref, g_ref, wa_ref, wb_ref, cwa_ref, cwb_ref, cba_ref, cbb_ref, wd_ref,
                     bufa_ref, bufb_ref, o_ref, ua_ref, ub_ref, h_ref, acc_ref):
    c = pl.program_id(0)

    @pl.when(c == 0)
    def _():
        x = x_ref[...]
        ms = jnp.mean(x * x, axis=-1, keepdims=True)
        h_ref[...] = (x * lax.rsqrt(ms + RMS_EPS) * g_ref[...]).astype(BF16)
        acc_ref[...] = jnp.zeros_like(acc_ref)

    h = h_ref[...]

    def branch(w_ref, cw_ref, cb_ref, buf_ref, u_ref):
        u = jnp.dot(h, w_ref[...], preferred_element_type=F32)
        u_ref[...] = u
        cw = cw_ref[...]
        return u * cw[2:3, :] + buf_ref[1] * cw[1:2, :] + buf_ref[0] * cw[0:1, :] + cb_ref[...]

    a = branch(wa_ref, cwa_ref, cba_ref, bufa_ref, ua_ref)
    b = branch(wb_ref, cwb_ref, cbb_ref, bufb_ref, ub_ref)
    acc_ref[...] += _bdot(_silu(a) * b, wd_ref[...])

    @pl.when(c == pl.num_programs(0) - 1)
    def _():
        o_ref[...] = x_ref[...] + acc_ref[...]


def _ffn_step(x, buf, g, w_up_bf, cw, cb, w_down_bf):
    m, d = x.shape
    nc = D_FF // FFN_TN
    buf_t = jnp.swapaxes(buf, 0, 1)
    out, ua, ub = pl.pallas_call(
        _ffn_step_kernel,
        grid=(nc,),
        in_specs=[pl.BlockSpec((m, d), lambda c: (0, 0)),
                  pl.BlockSpec((1, d), lambda c: (0, 0)),
                  pl.BlockSpec((d, FFN_TN), lambda c: (0, c)),
                  pl.BlockSpec((d, FFN_TN), lambda c: (0, nc + c)),
                  pl.BlockSpec((3, FFN_TN), lambda c: (0, c)),
                  pl.BlockSpec((3, FFN_TN), lambda c: (0, nc + c)),
                  pl.BlockSpec((1, FFN_TN), lambda c: (0, c)),
                  pl.BlockSpec((1, FFN_TN), lambda c: (0, nc + c)),
                  pl.BlockSpec((FFN_TN, d), lambda c: (c, 0)),
                  pl.BlockSpec((2, m, FFN_TN), lambda c: (0, 0, c)),
                  pl.BlockSpec((2, m, FFN_TN), lambda c: (0, 0, nc + c))],
        out_specs=[pl.BlockSpec((m, d), lambda c: (0, 0)),
                   pl.BlockSpec((m, FFN_TN), lambda c: (0, c)),
                   pl.BlockSpec((m, FFN_TN), lambda c: (0, c))],
        out_shape=[jax.ShapeDtypeStruct((m, d), F32),
                   jax.ShapeDtypeStruct((m, D_FF), F32),
                   jax.ShapeDtypeStruct((m, D_FF), F32)],
        scratch_shapes=[pltpu.VMEM((m, d), BF16), pltpu.VMEM((m, d), F32)],
        compiler_params=_cp("arbitrary"),
    )(x, g.reshape(1, d), w_up_bf, w_up_bf, cw, cw, cb.reshape(1, -1), cb.reshape(1, -1), w_down_bf,
      buf_t, buf_t)
    u = jnp.concatenate([ua, ub], axis=-1)
    return out, jnp.stack([buf[:, 1], u], axis=1)


S5_STRIP = 512
S5_NSTRIP = S5_LANES // S5_STRIP


def _s5_param_kernel(are_ref, aim_ref, ls_ref, brt_ref, bit_ref, abr_ref, abi_ref, bbr_ref, bbi_ref):
    ar, ai = are_ref[...], aim_ref[...]
    dt = jnp.exp(ls_ref[...])
    mag = jnp.exp(ar * dt)
    abr, abi = mag * jnp.cos(ai * dt), mag * jnp.sin(ai * dt)
    inv_abs2 = 1.0 / (ar * ar + ai * ai)
    cr = ((abr - 1.0) * ar + abi * ai) * inv_abs2
    ci = (abi * ar - (abr - 1.0) * ai) * inv_abs2
    br, bi = brt_ref[...], bit_ref[...]
    abr_ref[...] = abr
    abi_ref[...] = abi
    bbr_ref[...] = cr * br - ci * bi
    bbi_ref[...] = cr * bi + ci * br


def _s5_params(a_re, a_im, log_step, b_re, b_im, c_re, c_im, d_skip):
    g, n, k = b_re.shape
    rep = lambda t: jnp.repeat(t, k, axis=0)
    brt = b_re.transpose(0, 2, 1).reshape(g * k, n)
    bit = b_im.transpose(0, 2, 1).reshape(g * k, n)
    shp = jax.ShapeDtypeStruct((g * k, n), F32)
    abr, abi, bbr, bbi = pl.pallas_call(_s5_param_kernel, out_shape=(shp, shp, shp, shp))(
        rep(a_re), rep(a_im), rep(log_step[:, None]), brt, bit)
    abr = abr[::k].reshape(1, g * n)
    abi = abi[::k].reshape(1, g * n)
    gs = S5_STRIP // n
    eye = jnp.eye(gs, dtype=F32)

    def in_blocks(t):
        t = t.reshape(S5_NSTRIP, gs, k, n)
        return jnp.einsum('cgkn,gh->cgkhn', t, eye).reshape(S5_NSTRIP, gs * k, gs * n).astype(BF16)

    def out_blocks(t):
        t = t.reshape(S5_NSTRIP, gs, k, n)
        return jnp.einsum('cgkn,gh->cgnhk', t, eye).reshape(S5_NSTRIP, gs * n, gs * k).astype(BF16)

    return dict(abr=abr, abi=abi, wbr=in_blocks(bbr), wbi=in_blocks(bbi),
                wcr=out_blocks(c_re), wci=out_blocks(c_im), d=d_skip.reshape(1, g * k))


def _s5_input(u, wbr_ref, wbi_ref, c):
    cw = S5_STRIP // (S5_STATE // S5_GROUP)
    uc = u[:, c * cw:(c + 1) * cw].astype(BF16)
    return (jnp.dot(uc, wbr_ref[c], preferred_element_type=F32),
            jnp.dot(uc, wbi_ref[c], preferred_element_type=F32))


def _s5_output(u, h_strip, wcr_ref, wci_ref, d_ref, wg_ref, bg_ref):
    ys = []
    for c in range(S5_NSTRIP):
        hr, hi = h_strip(c)
        ys.append(jnp.dot(hr.astype(BF16), wcr_ref[c], preferred_element_type=F32)
                  - jnp.dot(hi.astype(BF16), wci_ref[c], preferred_element_type=F32))
    y = jnp.concatenate(ys, axis=-1) + d_ref[...] * u
    yg = 0.5 * y * (1.0 + jnp.tanh(math.sqrt(2.0 / math.pi) * (y + 0.044715 * (y * y * y))))
    z = jnp.dot(yg.astype(BF16), wg_ref[...], preferred_element_type=F32) + bg_ref[...]
    return yg * _sigmoid(z)


def _cmul(ar, ai, br, bi):
    return ar * br - ai * bi, ar * bi + ai * br


def _s5_seq_kernel(u_ref, abr_ref, abi_ref, wbr_ref, wbi_ref, wcr_ref, wci_ref, d_ref, wg_ref, bg_ref,
                   o_ref, hro_ref, hio_ref, hr_ref, hi_ref, cr_ref, ci_ref, tab_ref):
    i = pl.program_id(1)
    tt = u_ref.shape[0]

    @pl.when(i == 0)
    def _():
        cr_ref[...] = jnp.zeros_like(cr_ref)
        ci_ref[...] = jnp.zeros_like(ci_ref)
        row = lax.broadcasted_iota(jnp.int32, (SUBLANE, S5_LANES), 0)
        p_r, p_i = [abr_ref[...]], [abi_ref[...]]
        for _ in range(SUBLANE - 1):
            nr, ni = _cmul(p_r[-1], p_i[-1], p_r[0], p_i[0])
            p_r.append(nr)
            p_i.append(ni)
        for k, s in enumerate((1, 2, 4)):
            tab_ref[2 * k] = jnp.where(row >= s, p_r[s - 1], 0.0)
            tab_ref[2 * k + 1] = jnp.where(row >= s, p_i[s - 1], 0.0)
        car_r = jnp.zeros((SUBLANE, S5_LANES), F32)
        car_i = jnp.zeros((SUBLANE, S5_LANES), F32)
        for j in range(SUBLANE):
            car_r = jnp.where(row == j, p_r[j], car_r)
            car_i = jnp.where(row == j, p_i[j], car_i)
        tab_ref[6] = car_r
        tab_ref[7] = car_i

    u = u_ref[...]
    for c in range(S5_NSTRIP):
        br, bi = _s5_input(u, wbr_ref, wbi_ref, c)
        hr_ref[:, c * S5_STRIP:(c + 1) * S5_STRIP] = br
        hi_ref[:, c * S5_STRIP:(c + 1) * S5_STRIP] = bi

    def block(r, carry):
        r8 = pl.multiple_of(r * SUBLANE, SUBLANE)
        for c in range(S5_NSTRIP):
            sl = slice(c * S5_STRIP, (c + 1) * S5_STRIP)
            xr, xi = hr_ref[pl.ds(r8, SUBLANE), sl], hi_ref[pl.ds(r8, SUBLANE), sl]
            for k, s in enumerate((1, 2, 4)):
                dr, di = _cmul(tab_ref[2 * k, :, sl], tab_ref[2 * k + 1, :, sl],
                               pltpu.roll(xr, s, axis=0), pltpu.roll(xi, s, axis=0))
                xr, xi = xr + dr, xi + di
            dr, di = _cmul(tab_ref[6, :, sl], tab_ref[7, :, sl], cr_ref[:, sl], ci_ref[:, sl])
            xr, xi = xr + dr, xi + di
            hr_ref[pl.ds(r8, SUBLANE), sl] = xr
            hi_ref[pl.ds(r8, SUBLANE), sl] = xi
            cr_ref[:, sl] = xr[SUBLANE - 1:, :]
            ci_ref[:, sl] = xi[SUBLANE - 1:, :]
        return carry

    lax.fori_loop(0, tt // SUBLANE, block, 0)

    def h_strip(c):
        sl = slice(c * S5_STRIP, (c + 1) * S5_STRIP)
        return hr_ref[:, sl], hi_ref[:, sl]

    o_ref[...] = _s5_output(u, h_strip, wcr_ref, wci_ref, d_ref, wg_ref, bg_ref)
    hro_ref[0] = cr_ref[...]
    hio_ref[0] = ci_ref[...]


def _full(shape):
    nd = len(shape)
    return pl.BlockSpec(shape, lambda *_: (0,) * nd)


def _s5_seq(proj, batch, sp, w_glu_bf, b_glu):
    m = proj.shape[0]
    seq = m // batch
    tt = _row_tile(seq, 256)
    nt = seq // tt
    params = [sp['abr'], sp['abi'], sp['wbr'], sp['wbi'], sp['wcr'], sp['wci'], sp['d'],
              w_glu_bf, b_glu.reshape(1, -1)]
    out, hr, hi = pl.pallas_call(
        _s5_seq_kernel,
        grid=(batch, nt),
        in_specs=[pl.BlockSpec((tt, D_HALF), lambda b, i: (b * nt + i, 0))] + [_full(p.shape) for p in params],
        out_specs=[pl.BlockSpec((tt, D_HALF), lambda b, i: (b * nt + i, 0)),
                   pl.BlockSpec((1, 1, S5_LANES), lambda b, i: (b, 0, 0)),
                   pl.BlockSpec((1, 1, S5_LANES), lambda b, i: (b, 0, 0))],
        out_shape=[jax.ShapeDtypeStruct((m, D_HALF), F32),
                   jax.ShapeDtypeStruct((batch, 1, S5_LANES), F32),
                   jax.ShapeDtypeStruct((batch, 1, S5_LANES), F32)],
        scratch_shapes=[pltpu.VMEM((tt, S5_LANES), F32), pltpu.VMEM((tt, S5_LANES), F32),
                        pltpu.VMEM((1, S5_LANES), F32), pltpu.VMEM((1, S5_LANES), F32),
                        pltpu.VMEM((8, SUBLANE, S5_LANES), F32)],
        compiler_params=_cp("arbitrary", "arbitrary"))(proj, *params)
    return out, hr[:, 0], hi[:, 0]


def _s5_step_kernel(u_ref, h0r_ref, h0i_ref, abr_ref, abi_ref, wbr_ref, wbi_ref, wcr_ref, wci_ref, d_ref,
                    wg_ref, bg_ref, o_ref, hro_ref, hio_ref):
    u = u_ref[...]
    for c in range(S5_NSTRIP):
        sl = slice(c * S5_STRIP, (c + 1) * S5_STRIP)
        br, bi = _s5_input(u, wbr_ref, wbi_ref, c)
        dr, di = _cmul(abr_ref[:, sl], abi_ref[:, sl], h0r_ref[:, sl], h0i_ref[:, sl])
        hro_ref[:, sl] = dr + br
        hio_ref[:, sl] = di + bi

    def h_strip(c):
        sl = slice(c * S5_STRIP, (c + 1) * S5_STRIP)
        return hro_ref[:, sl], hio_ref[:, sl]

    o_ref[...] = _s5_output(u, h_strip, wcr_ref, wci_ref, d_ref, wg_ref, bg_ref)


def _s5_step(proj, h0r, h0i, sp, w_glu_bf, b_glu):
    m = proj.shape[0]
    args = [h0r, h0i, sp['abr'], sp['abi'], sp['wbr'], sp['wbi'], sp['wcr'], sp['wci'], sp['d'],
            w_glu_bf, b_glu.reshape(1, -1)]
    return pl.pallas_call(
        _s5_step_kernel,
        grid=(1,),
        in_specs=[pl.BlockSpec((m, D_HALF), lambda i: (0, 0))] + [_full(a.shape) for a in args],
        out_specs=[_full((m, D_HALF)), _full((m, S5_LANES)), _full((m, S5_LANES))],
        out_shape=[jax.ShapeDtypeStruct((m, D_HALF), F32),
                   jax.ShapeDtypeStruct((m, S5_LANES), F32),
                   jax.ShapeDtypeStruct((m, S5_LANES), F32)],
        compiler_params=_cp("arbitrary"))(proj, *args)


HEADS_PER_TILE = LANE // HEAD_DIM
HEAD_TILES = HEADS // HEADS_PER_TILE
ATTN_SCALE = HEAD_DIM ** -0.5
NT_DIMS = (((1,), (1,)), ((), ()))
MOBA_GROUP = 4


def _moba_seq_kernel(q_ref, k_ref, v_ref, o_ref, kbf_ref, vt_ref, kmean_ref, sel_ref, *, nb):
    qi = pl.program_id(2)
    blk = MOBA_BLOCK

    @pl.when(qi == 0)
    def _():
        kbf_ref[...] = k_ref[...].astype(BF16)
        vt_ref[...] = v_ref[...].T.astype(BF16)
        for j in range(nb):
            kmean_ref[j:j + 1, :] = jnp.mean(k_ref[j * blk:(j + 1) * blk, :], axis=0, keepdims=True)

    q = q_ref[...]
    row = lax.broadcasted_iota(jnp.int32, (nb, blk), 0)
    causal = (lax.broadcasted_iota(jnp.int32, (blk, blk), 0) <= lax.broadcasted_iota(jnp.int32, (blk, blk), 1))
    q0 = pl.multiple_of(qi * blk, blk)
    head_lanes = [slice(h * HEAD_DIM, (h + 1) * HEAD_DIM) for h in range(HEADS_PER_TILE)]
    qts, state = [], []
    for h, sl in enumerate(head_lanes):
        qh = q[:, sl]
        s = lax.dot_general(kmean_ref[:, sl], qh, NT_DIMS, precision=HIGHEST, preferred_element_type=F32)
        rank = jnp.zeros((nb, blk), F32)
        for i in range(nb):
            si = s[i:i + 1, :]
            beats = jnp.logical_or(si > s, jnp.logical_and(si == s, i < row))
            rank = rank + jnp.where(jnp.logical_and(beats, i < qi), 1.0, 0.0)
        sel_ref[h] = jnp.where(jnp.logical_and(row < qi, rank < MOBA_TOPK), 1.0, 0.0)

        qt = (qh * ATTN_SCALE).T.astype(BF16)
        lg = jnp.dot(kbf_ref[pl.ds(q0, blk), sl], qt, preferred_element_type=F32)
        lg = jnp.where(causal, lg, NEG)
        m = jnp.max(lg, axis=0, keepdims=True)
        p = jnp.exp(lg - m)
        l = jnp.sum(p, axis=0, keepdims=True)
        acc = jnp.dot(vt_ref[sl, pl.ds(q0, blk)], p.astype(BF16), preferred_element_type=F32)
        qts.append(qt)
        state += [m, l, acc]

    grp = MOBA_GROUP

    def body(jj, carry):
        j0 = pl.multiple_of(jj * (grp * blk), grp * blk)
        out = []
        for h, sl in enumerate(head_lanes):
            m, l, acc = carry[3 * h:3 * h + 3]
            lg = jnp.dot(kbf_ref[pl.ds(j0, grp * blk), sl], qts[h], preferred_element_type=F32)
            lg = jnp.concatenate(
                [jnp.where(sel_ref[h, pl.ds(jj * grp + g, 1), :] > 0.0, lg[g * blk:(g + 1) * blk], NEG)
                 for g in range(grp)], axis=0)
            m_new = jnp.maximum(m, jnp.max(lg, axis=0, keepdims=True))
            alpha = jnp.exp(m - m_new)
            p = jnp.exp(lg - m_new)
            l = alpha * l + jnp.sum(p, axis=0, keepdims=True)
            acc = alpha * acc + jnp.dot(vt_ref[sl, pl.ds(j0, grp * blk)], p.astype(BF16),
                                        preferred_element_type=F32)
            out += [m_new, l, acc]
        return tuple(out)

    state = lax.fori_loop(0, (qi + grp - 1) // grp, body, tuple(state))
    for h, sl in enumerate(head_lanes):
        m, l, acc = state[3 * h:3 * h + 3]
        o_ref[:, sl] = (acc / l).T


def _moba_seq(proj, batch):
    m = proj.shape[0]
    seq = m // batch
    assert seq % (MOBA_BLOCK * MOBA_GROUP) == 0
    nb = seq // MOBA_BLOCK
    qoff, koff, voff = HEAD_TILES, 2 * HEAD_TILES, 3 * HEAD_TILES
    return pl.pallas_call(
        functools.partial(_moba_seq_kernel, nb=nb),
        grid=(batch, HEAD_TILES, nb),
        in_specs=[pl.BlockSpec((MOBA_BLOCK, LANE), lambda b, t, i: (b * nb + i, qoff + t)),
                  pl.BlockSpec((seq, LANE), lambda b, t, i: (b, koff + t)),
                  pl.BlockSpec((seq, LANE), lambda b, t, i: (b, voff + t))],
        out_specs=pl.BlockSpec((MOBA_BLOCK, LANE), lambda b, t, i: (b * nb + i, t)),
        out_shape=jax.ShapeDtypeStruct((m, D_HALF), F32),
        scratch_shapes=[pltpu.VMEM((seq, LANE), BF16), pltpu.VMEM((LANE, seq), BF16),
                        pltpu.VMEM((nb, LANE), F32), pltpu.VMEM((HEADS_PER_TILE, nb, MOBA_BLOCK), F32)],
        compiler_params=_cp("parallel", "parallel", "arbitrary"))(proj, proj, proj)


MOBA_PAGES_PER_STEP = 16
PAGES_PER_BLOCK = MOBA_BLOCK // PAGE_SIZE


def _moba_rank_kernel(pt_ref, q_ref, *rest, nblk):
    k_refs = rest[:MOBA_PAGES_PER_STEP]
    idx_ref, s_ref = rest[MOBA_PAGES_PER_STEP:]
    j = pl.program_id(1)
    bps = MOBA_PAGES_PER_STEP // PAGES_PER_BLOCK
    lane = lax.broadcasted_iota(jnp.int32, (HEADS, LANE), 1)

    @pl.when(j == 0)
    def _():
        s_ref[...] = jnp.zeros_like(s_ref)

    q = q_ref[...]
    acc = s_ref[...]
    for t in range(bps):
        per_token = sum(jnp.sum(k_refs[t * PAGES_PER_BLOCK + p][...] * q, axis=1)
                        for p in range(PAGES_PER_BLOCK))
        score = jnp.sum(per_token, axis=-1, keepdims=True) * (1.0 / MOBA_BLOCK)
        acc = acc + jnp.where(lane == j * bps + t, score, 0.0)
    s_ref[...] = acc

    @pl.when(j == pl.num_programs(1) - 1)
    def _():
        s = acc
        rank = jnp.zeros((HEADS, LANE), jnp.int32)
        for i in range(nblk):
            si = s[:, i:i + 1]
            beats = jnp.logical_or(si > s, jnp.logical_and(si == s, i < lane))
            rank = rank + beats.astype(jnp.int32)
        out = jnp.zeros((HEADS, LANE), jnp.int32)
        for slot in range(MOBA_TOPK):
            hit = jnp.logical_and(rank == slot, lane < nblk)
            idx = jnp.sum(jnp.where(hit, lane, 0), axis=-1, keepdims=True)
            out = jnp.where(lane == slot, idx, out)
        idx_ref[...] = out


def _moba_pick_kernel(pg_ref, q_ref, kn_ref, vn_ref, *rest):
    n = MOBA_TOPK * PAGES_PER_BLOCK
    k_refs, v_refs, o_ref = rest[:n], rest[n:2 * n], rest[2 * n]
    q = q_ref[...] * ATTN_SCALE
    lg_self = jnp.sum(q * kn_ref[...], axis=0, keepdims=True)
    lgs = [jnp.sum(k_ref[...] * q, axis=0, keepdims=True) for k_ref in k_refs]
    m = lg_self
    for lg in lgs:
        m = jnp.maximum(m, jnp.max(lg, axis=-1, keepdims=True))
    p_self = jnp.exp(lg_self - m)
    l = p_self
    acc = p_self * vn_ref[...]
    for lg, v_ref in zip(lgs, v_refs):
        p = jnp.exp(lg - m)
        l = l + jnp.sum(p, axis=-1, keepdims=True)
        acc = acc + jnp.sum(v_ref[...] * p, axis=-1, keepdims=True)
    o_ref[...] = acc / l


def _moba_step(proj, cache_kt, cache_vt, layer, page_table):
    bsz = proj.shape[0]
    n_pages = page_table.shape[1]
    assert n_pages % MOBA_PAGES_PER_STEP == 0 and PAGE_SIZE == LANE
    nblk = n_pages // PAGES_PER_BLOCK
    assert nblk <= LANE
    steps = n_pages // MOBA_PAGES_PER_STEP
    cols = proj.reshape(bsz, 4, HEADS, HEAD_DIM, 1)
    q_col, k_col, v_col = cols[:, 1], cols[:, 2], cols[:, 3]

    def page_spec(t):
        return pl.BlockSpec((None, None, HEADS, HEAD_DIM, PAGE_SIZE),
                            lambda b, j, pt: (layer, pt[b, j * MOBA_PAGES_PER_STEP + t], 0, 0, 0))

    idx = pl.pallas_call(
        functools.partial(_moba_rank_kernel, nblk=nblk),
        grid_spec=pltpu.PrefetchScalarGridSpec(
            num_scalar_prefetch=1,
            grid=(bsz, steps),
            in_specs=[pl.BlockSpec((None, HEADS, HEAD_DIM, 1), lambda b, j, pt: (b, 0, 0, 0))]
            + [page_spec(t) for t in range(MOBA_PAGES_PER_STEP)],
            out_specs=pl.BlockSpec((None, HEADS, LANE), lambda b, j, pt: (b, 0, 0)),
            scratch_shapes=[pltpu.VMEM((HEADS, LANE), F32)]),
        out_shape=jax.ShapeDtypeStruct((bsz, HEADS, LANE), jnp.int32),
        compiler_params=_cp("arbitrary", "arbitrary"),
    )(page_table, q_col, *([cache_kt] * MOBA_PAGES_PER_STEP))

    top = idx[:, :, :MOBA_TOPK]
    logical = top[..., None] * PAGES_PER_BLOCK + jnp.arange(PAGES_PER_BLOCK, dtype=jnp.int32)
    pages = jnp.take_along_axis(page_table, logical.reshape(bsz, -1), axis=1).reshape(-1)
    per_head = MOBA_TOPK * PAGES_PER_BLOCK

    def pick_spec(t):
        return pl.BlockSpec((None, None, None, HEAD_DIM, PAGE_SIZE),
                            lambda b, h, pg: (layer, pg[(b * HEADS + h) * per_head + t], h, 0, 0))

    tok_spec = pl.BlockSpec((None, None, HEAD_DIM, 1), lambda b, h, pg: (b, h, 0, 0))
    out = pl.pallas_call(
        _moba_pick_kernel,
        grid_spec=pltpu.PrefetchScalarGridSpec(
            num_scalar_prefetch=1,
            grid=(bsz, HEADS),
            in_specs=[tok_spec] * 3 + [pick_spec(t) for t in range(per_head)] * 2,
            out_specs=tok_spec),
        out_shape=jax.ShapeDtypeStruct((bsz, HEADS, HEAD_DIM, 1), F32),
        compiler_params=_cp("arbitrary", "arbitrary"),
    )(pages, q_col, k_col, v_col, *([cache_kt] * per_head), *([cache_vt] * per_head))
    return out.reshape(bsz, D_HALF)


def _seg_sum(x):
    seg = lax.broadcasted_iota(jnp.int32, (LANE, LANE), 0) // HEAD_DIM
    ones = jnp.where(seg == lax.broadcasted_iota(jnp.int32, (LANE, LANE), 1) // HEAD_DIM, 1.0, 0.0)
    parts = [jnp.dot(x[:, c * LANE:(c + 1) * LANE], ones, precision=HIGHEST, preferred_element_type=F32)
             for c in range(x.shape[1] // LANE)]
    return jnp.concatenate(parts, axis=-1)


def _softplus(x):
    return jnp.maximum(x, 0.0) + jnp.log(1.0 + jnp.exp(-jnp.abs(x)))


def _rwkv_mix(cat, prev, p, vfirst):
    c = D_HALF
    dlt = prev - cat
    part = lambda t, j: t[:, j * c:(j + 1) * c]
    mu, mz = p['mu'][...], p['mz'][...]
    r = part(cat, 0) + part(dlt, 0) * mu[0:1]
    k = part(cat, 1) + part(dlt, 1) * mu[1:2]
    v = part(cat, 2) + part(dlt, 2) * mu[2:3]
    z0, dz = part(cat, 3), part(dlt, 3)
    zw, za, zg = z0 + dz * mz[0:1], z0 + dz * mz[1:2], z0 + dz * mz[2:3]
    w_raw = -_softplus(-(p['w0'][...] + _bdot(jnp.tanh(_bdot(zw, p['w1'][...])), p['w2'][...]))) - 0.5
    log_decay = -jnp.exp(w_raw)
    a = _sigmoid(p['a0'][...] + _bdot(_bdot(za, p['a1'][...]), p['a2'][...]))
    g = _bdot(_sigmoid(_bdot(zg, p['g1'][...])), p['g2'][...])
    if vfirst is not None:
        zv = z0 + dz * p['mu_v'][...]
        vmix = _sigmoid(p['v0'][...] + _bdot(_bdot(zv, p['v1'][...]), p['v2'][...]))
        v = v + (vfirst - v) * vmix
    kk = k * p['k_k'][...]
    kk = kk / jnp.maximum(jnp.sqrt(_seg_sum(kk * kk)), 1e-12)
    k = k * (1.0 + (a - 1.0) * p['k_a'][...])
    return r, log_decay, k, v, kk, kk * a, g


RWKV_PARAM_NAMES = ('mu', 'mz', 'w0', 'w1', 'w2', 'a0', 'a1', 'a2', 'g1', 'g2', 'k_k', 'k_a')
RWKV_VRES_NAMES = ('mu_v', 'v0', 'v1', 'v2')
RWKV_N_OUT = 7


def _rwkv_pre_kernel(*refs, names, has_vfirst, seq_tiles):
    refs = list(refs)
    cat_ref = refs.pop(0)
    prev_ref = refs.pop(0)
    vf_ref = refs.pop(0) if has_vfirst else None
    p = {n: refs.pop(0) for n in names}
    outs = refs
    cat = cat_ref[...]
    if seq_tiles:
        first = (pl.program_id(0) % seq_tiles) == 0
        row = lax.broadcasted_iota(jnp.int32, (cat.shape[0], 1), 0)
        before = jnp.where(first, 0.0, prev_ref[SUBLANE - 1:SUBLANE, :])
        prev = jnp.where(row == 0, before, pltpu.roll(cat, 1, axis=0))
    else:
        prev = prev_ref[...]
    res = _rwkv_mix(cat, prev, p, None if vf_ref is None else vf_ref[...])
    for o_ref, val in zip(outs, res):
        o_ref[...] = val


def _rwkv_pre(proj, prev, rp, vfirst, batch):
    m = proj.shape[0]
    cw = 4 * D_HALF
    names = RWKV_PARAM_NAMES + (RWKV_VRES_NAMES if vfirst is not None else ())
    params = [rp[n] for n in names]
    if prev is None:
        seq = m // batch
        tm = _row_tile(seq, 256)
        seq_tiles = seq // tm
        hb = tm // SUBLANE
        prev_arg = proj
        prev_spec = pl.BlockSpec((SUBLANE, cw), lambda i: (jnp.maximum(i * hb - 1, 0), 0))
    else:
        tm, seq_tiles = m, 0
        prev_arg = prev
        prev_spec = pl.BlockSpec((tm, cw), lambda i: (i, 0))
    args = [proj, prev_arg]
    in_specs = [pl.BlockSpec((tm, cw), lambda i: (i, 0)), prev_spec]
    if vfirst is not None:
        args.append(vfirst)
        in_specs.append(pl.BlockSpec((tm, D_HALF), lambda i: (i, 0)))
    args += params
    in_specs += [_full(a.shape) for a in params]
    return pl.pallas_call(
        functools.partial(_rwkv_pre_kernel, names=names, has_vfirst=vfirst is not None, seq_tiles=seq_tiles),
        grid=(m // tm,),
        in_specs=in_specs,
        out_specs=[pl.BlockSpec((tm, D_HALF), lambda i: (i, 0))] * RWKV_N_OUT,
        out_shape=[jax.ShapeDtypeStruct((m, D_HALF), F32)] * RWKV_N_OUT,
        compiler_params=_cp("parallel"))(*args)


WKV_CHUNK = 64


def _bmm(a, b):
    return lax.dot_general(a.astype(BF16), b.astype(BF16), (((2,), (1,)), ((0,), (0,))),
                           preferred_element_type=F32)


def _bmm3(a, b):
    ah = a.astype(BF16)
    al = (a - ah.astype(F32)).astype(BF16)
    bh = b.astype(BF16)
    bl = (b - bh.astype(F32)).astype(BF16)
    return _bmm(ah, bh) + (_bmm(ah, bl) + _bmm(al, bh))


def _wkv_tile(r, lw, k, v, kk, b, st):
    tt = r.shape[0]
    c, n, nh = WKV_CHUNK, HEAD_DIM, HEADS_PER_TILE
    nc = tt // c
    ri = lax.broadcasted_iota(jnp.int32, (1, c, c), 1)
    ci = lax.broadcasted_iota(jnp.int32, (1, c, c), 2)
    incl, strict = ri >= ci, ri > ci
    eye = jnp.where(ri == ci, 1.0, 0.0)
    tril = jnp.where(incl[0], 1.0, 0.0)
    lw3 = lw.reshape(nc, c, LANE)
    cum = jnp.stack([jnp.dot(tril, lw3[j], precision=HIGHEST, preferred_element_type=F32) for j in range(nc)])
    tot = cum[:, c - 1:c, :]
    e_neg = jnp.exp(-cum)
    e_end = jnp.exp(tot - cum)
    e_tot = jnp.exp(tot)

    def heads(x):
        return jnp.concatenate([x[:, :, h * n:(h + 1) * n] for h in range(nh)], axis=0)

    def chunks(x):
        return x.reshape(nc, c, LANE)

    kkm = heads(chunks(kk) * jnp.exp(cum - lw3))
    rp = heads(chunks(r) * jnp.exp(cum))
    bo, ko = heads(chunks(b) * e_neg), heads(chunks(k) * e_neg)
    bend, kend = heads(chunks(b) * e_end), heads(chunks(k) * e_end)
    vh = heads(chunks(v))
    e_tot = heads(e_tot)

    qa = lax.dot_general(jnp.concatenate([kkm, rp], axis=1).astype(BF16),
                         jnp.concatenate([bo, ko], axis=1).astype(BF16),
                         (((2,), (2,)), ((0,), (0,))), preferred_element_type=F32)
    a_ub = jnp.where(strict, qa[:, :c, :c], 0.0)
    a_vk = jnp.where(strict, qa[:, :c, c:], 0.0)
    a_rb = jnp.where(incl, qa[:, c:, :c], 0.0)
    a_rk = jnp.where(incl, qa[:, c:, c:], 0.0)

    m = 2
    t_inv = eye - jnp.where((ri // m) == (ci // m), a_ub, 0.0)
    while m < c:
        off = jnp.logical_and((ri // (2 * m)) == (ci // (2 * m)), (ri // m) != (ci // m))
        t_inv = t_inv - _bmm(_bmm(t_inv, jnp.where(off, a_ub, 0.0)), t_inv)
        m *= 2

    av = _bmm(jnp.concatenate([a_vk, a_rk], axis=1), vh)
    x1 = _bmm(t_inv, jnp.concatenate([kkm, av[:, :c]], axis=2))
    x2 = _bmm(a_rb, x1)
    x3 = _bmm(jnp.swapaxes(bend, 1, 2), x1)
    rr = rp - x2[:, :, :n]
    o_loc = av[:, c:] - x2[:, :, n:]
    trans = eye * e_tot - x3[:, :, :n]
    s_loc = _bmm(jnp.swapaxes(kend, 1, 2), vh) - x3[:, :, n:]

    by_head = lambda x: x.reshape(nh, nc, x.shape[1], x.shape[2])
    rr, o_loc, trans, s_loc = by_head(rr), by_head(o_loc), by_head(trans), by_head(s_loc)
    outs = []
    for j in range(nc):
        outs.append(_bmm(rr[:, j], st) + o_loc[:, j])
        st = _bmm3(trans[:, j], st) + s_loc[:, j]
    o = jnp.stack(outs, axis=1).reshape(nh, tt, n)
    return jnp.concatenate([o[h] for h in range(nh)], axis=-1), st


WKV_TIME_TILE = 512


def _seg_mean(x):
    return _seg_sum(x) * (1.0 / HEAD_DIM)


def _wkv_seq_kernel(r_ref, lw_ref, k_ref, v_ref, kk_ref, b_ref, g_ref, rk_ref, lnw_ref, lnb_ref,
                    o_ref, so_ref, st_ref):
    i = pl.program_id(2)

    @pl.when(i == 0)
    def _():
        st_ref[...] = jnp.zeros_like(st_ref)

    r, k, v = r_ref[...], k_ref[...], v_ref[...]
    o, st = _wkv_tile(r, lw_ref[...], k, v, kk_ref[...], b_ref[...], st_ref[...])
    st_ref[...] = st
    mean = _seg_mean(o)
    var = _seg_mean(jnp.square(o - mean))
    on = (o - mean) * lax.rsqrt(var + RWKV_GN_EPS) * lnw_ref[...] + lnb_ref[...]
    bonus = _seg_sum(r * k * rk_ref[...]) * v
    o_ref[...] = (on + bonus) * g_ref[...]
    for h in range(HEADS_PER_TILE):
        so_ref[0, h] = st[h].T


def _wkv_seq(r, lw, k, v, kk, b, g, r_k, ln_w, ln_b, batch):
    m = r.shape[0]
    seq = m // batch
    tt = _row_tile(seq, WKV_TIME_TILE)
    assert tt % WKV_CHUNK == 0
    nt = seq // tt
    row_spec = pl.BlockSpec((tt, LANE), lambda bi, t, i: (bi * nt + i, t))
    par_spec = pl.BlockSpec((1, LANE), lambda bi, t, i: (0, t))
    return pl.pallas_call(
        _wkv_seq_kernel,
        grid=(batch, HEAD_TILES, nt),
        in_specs=[row_spec] * 7 + [par_spec] * 3,
        out_specs=[row_spec,
                   pl.BlockSpec((1, HEADS_PER_TILE, HEAD_DIM, HEAD_DIM), lambda bi, t, i: (bi, t, 0, 0))],
        out_shape=[jax.ShapeDtypeStruct((m, D_HALF), F32),
                   jax.ShapeDtypeStruct((batch, HEADS, HEAD_DIM, HEAD_DIM), F32)],
        scratch_shapes=[pltpu.VMEM((HEADS_PER_TILE, HEAD_DIM, HEAD_DIM), F32)],
        compiler_params=_cp("parallel", "parallel", "arbitrary"),
    )(r, lw, k, v, kk, b, g, r_k.reshape(1, -1), ln_w.reshape(1, -1), ln_b.reshape(1, -1))


def _wkv_finish(o, r, k, v, g, rk, lnw, lnb):
    mean = jnp.mean(o, axis=-1, keepdims=True)
    var = jnp.mean(jnp.square(o - mean), axis=-1, keepdims=True)
    on = (o - mean) * lax.rsqrt(var + RWKV_GN_EPS) * lnw + lnb
    bonus = jnp.sum(r * k * rk, axis=-1, keepdims=True) * v
    return (on + bonus) * g


def _wkv_step_kernel(s_ref, r_ref, lw_ref, k_ref, v_ref, kk_ref, b_ref, g_ref, rk_ref, lnw_ref, lnb_ref,
                     o_ref, so_ref):
    ri = lax.broadcasted_iota(jnp.int32, (HEAD_DIM, HEAD_DIM), 0)
    eye = jnp.where(ri == lax.broadcasted_iota(jnp.int32, (HEAD_DIM, HEAD_DIM), 1), 1.0, 0.0)
    outs = []
    for h in range(HEADS):
        sl = slice(h * HEAD_DIM, (h + 1) * HEAD_DIM)
        r, lw, k, v, kk, b, g = [ref[0][:, sl] for ref in (r_ref, lw_ref, k_ref, v_ref, kk_ref, b_ref, g_ref)]
        s = s_ref[0, h]
        sa = -jnp.sum(s * kk, axis=-1, keepdims=True)
        v_col = jnp.sum(eye * v, axis=-1, keepdims=True)
        s = s * jnp.exp(lw) + sa * b + v_col * k
        so_ref[0, h] = s
        o_col = jnp.sum(s * r, axis=-1, keepdims=True)
        o = jnp.sum(eye * o_col, axis=0, keepdims=True)
        outs.append(_wkv_finish(o, r, k, v, g, rk_ref[:, sl], lnw_ref[:, sl], lnb_ref[:, sl]))
    o_ref[0] = jnp.concatenate(outs, axis=-1)


def _wkv_step(state, r, lw, k, v, kk, b, g, r_k, ln_w, ln_b):
    bsz = r.shape[0]
    rows = [t.reshape(bsz, 1, D_HALF) for t in (r, lw, k, v, kk, b, g)]
    row_spec = pl.BlockSpec((1, 1, D_HALF), lambda i: (i, 0, 0))
    st_spec = pl.BlockSpec((1, HEADS, HEAD_DIM, HEAD_DIM), lambda i: (i, 0, 0, 0))
    out, st = pl.pallas_call(
        _wkv_step_kernel,
        grid=(bsz,),
        in_specs=[st_spec] + [row_spec] * 7 + [_full((1, D_HALF))] * 3,
        out_specs=[row_spec, st_spec],
        out_shape=[jax.ShapeDtypeStruct((bsz, 1, D_HALF), F32),
                   jax.ShapeDtypeStruct(state.shape, F32)],
        compiler_params=_cp("parallel"),
    )(state, *rows, r_k.reshape(1, -1), ln_w.reshape(1, -1), ln_b.reshape(1, -1))
    return out.reshape(bsz, D_HALF), st


CONV_HALO = 32


def _layernorm_silu(y, g, b):
    m = jnp.mean(y, axis=-1, keepdims=True)
    var = jnp.mean(jnp.square(y - m), axis=-1, keepdims=True)
    return _silu((y - m) * lax.rsqrt(var + LN_EPS) * g + b)


def _convd_seq_kernel(gv_ref, gg_ref, hv_ref, hg_ref, w_ref, b_ref, lng_ref, lnb_ref,
                      o_ref, tail_ref, ext_ref, *, seq_tiles):
    tt = gv_ref.shape[0]
    first = (pl.program_id(0) % seq_tiles) == 0
    ext_ref[0:CONV_HALO, :] = jnp.where(first, 0.0, hv_ref[...] * _sigmoid(hg_ref[...]))
    ext_ref[CONV_HALO:, :] = gv_ref[...] * _sigmoid(gg_ref[...])
    off = CONV_HALO - (CONV_D_WIDTH - 1)
    acc = jnp.zeros((tt, D_HALF), F32)
    for j in range(CONV_D_WIDTH):
        acc = acc + w_ref[j:j + 1, :] * ext_ref[off + j:off + j + tt, :]
    o_ref[...] = _layernorm_silu(acc + b_ref[...], lng_ref[...], lnb_ref[...])
    tail_ref[0] = ext_ref[tt:, :]


def _convd_seq(proj, batch, w, b, ln_g, ln_b):
    m = proj.shape[0]
    seq = m // batch
    tt = _row_tile(seq, 256)
    seq_tiles = seq // tt
    hb = tt // CONV_HALO
    vcol, gcol = 4, 5
    tile = lambda col: pl.BlockSpec((tt, D_HALF), lambda i: (i, col))
    halo = lambda col: pl.BlockSpec((CONV_HALO, D_HALF), lambda i: (jnp.maximum(i * hb - 1, 0), col))
    params = [w, b.reshape(1, -1), ln_g.reshape(1, -1), ln_b.reshape(1, -1)]
    out, tail = pl.pallas_call(
        functools.partial(_convd_seq_kernel, seq_tiles=seq_tiles),
        grid=(m // tt,),
        in_specs=[tile(vcol), tile(gcol), halo(vcol), halo(gcol)] + [_full(p.shape) for p in params],
        out_specs=[pl.BlockSpec((tt, D_HALF), lambda i: (i, 0)),
                   pl.BlockSpec((1, CONV_HALO, D_HALF), lambda i: (i, 0, 0))],
        out_shape=[jax.ShapeDtypeStruct((m, D_HALF), F32),
                   jax.ShapeDtypeStruct((m // tt, CONV_HALO, D_HALF), F32)],
        scratch_shapes=[pltpu.VMEM((tt + CONV_HALO, D_HALF), F32)],
        compiler_params=_cp("parallel"))(proj, proj, proj, proj, *params)
    return out, tail[seq_tiles - 1::seq_tiles, CONV_HALO - (CONV_D_WIDTH - 1):]


def _convd_step_kernel(gv_ref, gg_ref, buf_ref, w_ref, b_ref, lng_ref, lnb_ref, o_ref, u_ref):
    u = gv_ref[...] * _sigmoid(gg_ref[...])
    u_ref[...] = u
    acc = u * w_ref[CONV_D_WIDTH - 1:CONV_D_WIDTH, :]
    for j in range(CONV_D_WIDTH - 1):
        acc = acc + w_ref[j:j + 1, :] * buf_ref[j]
    o_ref[...] = _layernorm_silu(acc + b_ref[...], lng_ref[...], lnb_ref[...])


def _convd_step(proj, buf, w, b, ln_g, ln_b):
    m = proj.shape[0]
    buf_t = jnp.swapaxes(buf, 0, 1)
    params = [w, b.reshape(1, -1), ln_g.reshape(1, -1), ln_b.reshape(1, -1)]
    out, u = pl.pallas_call(
        _convd_step_kernel,
        grid=(1,),
        in_specs=[pl.BlockSpec((m, D_HALF), lambda i: (0, 4)), pl.BlockSpec((m, D_HALF), lambda i: (0, 5)),
                  _full(buf_t.shape)] + [_full(p.shape) for p in params],
        out_specs=[_full((m, D_HALF)), _full((m, D_HALF))],
        out_shape=[jax.ShapeDtypeStruct((m, D_HALF), F32)] * 2,
        compiler_params=_cp("arbitrary"))(proj, proj, buf_t, *params)
    return out, jnp.concatenate([buf[:, 1:], u[:, None]], axis=1)


def _rmsnorm_kernel(x_ref, g_ref, o_ref):
    x = x_ref[...]
    ms = jnp.mean(x * x, axis=-1, keepdims=True)
    o_ref[...] = x * lax.rsqrt(ms + RMS_EPS) * g_ref[...]


def _rmsnorm(x, g):
    m, d = x.shape
    tm = _row_tile(m, 512)
    return pl.pallas_call(
        _rmsnorm_kernel,
        grid=(m // tm,),
        in_specs=[pl.BlockSpec((tm, d), lambda i: (i, 0)), _full((1, d))],
        out_specs=pl.BlockSpec((tm, d), lambda i: (i, 0)),
        out_shape=jax.ShapeDtypeStruct((m, d), F32),
        compiler_params=_cp("parallel"))(x, g.reshape(1, d))


def _kv_layout_kernel(*refs, n_layers, paged):
    k_refs, v_refs = refs[:n_layers], refs[n_layers:2 * n_layers]
    ko_ref, vo_ref = refs[2 * n_layers:]
    layer = pl.program_id(0)

    def emit(src_ref, dst_ref):
        x = src_ref[...]
        if not paged:
            dst_ref[...] = x.reshape(dst_ref.shape)
            return
        for pg in range(dst_ref.shape[0]):
            rows = x[pg * PAGE_SIZE:(pg + 1) * PAGE_SIZE]
            for h in range(HEADS):
                dst_ref[pg, h] = rows[:, h * HEAD_DIM:(h + 1) * HEAD_DIM].T

    for l in range(n_layers):
        @pl.when(layer == l)
        def _():
            emit(k_refs[l], ko_ref)
            emit(v_refs[l], vo_ref)


def _kv_layout(projs, paged):
    n = len(projs)
    m = projs[0].shape[0]
    tm = _row_tile(m, 512)
    nt = m // tm

    def col_spec(l, col):
        return pl.BlockSpec((tm, D_HALF), lambda layer, i: (jnp.where(layer == l, i, 0), col))

    if paged:
        ppt = tm // PAGE_SIZE
        out_spec = pl.BlockSpec((ppt, HEADS, HEAD_DIM, PAGE_SIZE), lambda layer, i: (layer * nt + i, 0, 0, 0))
        shp = jax.ShapeDtypeStruct((n * m // PAGE_SIZE, HEADS, HEAD_DIM, PAGE_SIZE), F32)
    else:
        out_spec = pl.BlockSpec((tm, HEADS, HEAD_DIM), lambda layer, i: (layer * nt + i, 0, 0))
        shp = jax.ShapeDtypeStruct((n * m, HEADS, HEAD_DIM), F32)
    return pl.pallas_call(
        functools.partial(_kv_layout_kernel, n_layers=n, paged=paged),
        grid=(n, nt),
        in_specs=[col_spec(l, 2) for l in range(n)] + [col_spec(l, 3) for l in range(n)],
        out_specs=[out_spec, out_spec],
        out_shape=[shp, shp],
        compiler_params=_cp("parallel", "parallel"))(*projs, *projs)


def _rwkv_params(p, li):
    row = lambda t: t.reshape(1, -1)
    rp = dict(mu=p['rwkv_mu_rkv'][li], mz=p['rwkv_mu_z'][li], w0=row(p['rwkv_w0'][li]),
              w1=p['rwkv_w1'][li].astype(BF16), w2=p['rwkv_w2'][li].astype(BF16),
              a0=row(p['rwkv_a0'][li]), a1=p['rwkv_a1'][li].astype(BF16), a2=p['rwkv_a2'][li].astype(BF16),
              g1=p['rwkv_g1'][li].astype(BF16), g2=p['rwkv_g2'][li].astype(BF16),
              k_k=row(p['rwkv_k_k'][li]), k_a=row(p['rwkv_k_a'][li]))
    if li > 0:
        rp.update(mu_v=row(p['rwkv_mu_v'][li - 1]), v0=row(p['rwkv_v0'][li - 1]),
                  v1=p['rwkv_v1'][li - 1].astype(BF16), v2=p['rwkv_v2'][li - 1].astype(BF16))
    return rp


def _prepare(p):
    depth = p['g_mix'].shape[0]
    layers = []
    for i in range(depth):
        li = i // 2
        lay = dict(g_mix=p['g_mix'][i], g_ffn=p['g_ffn'][i],
                   w_up=p['ffn_w_up'][i].astype(BF16), w_down=p['ffn_w_down'][i].astype(BF16),
                   ffn_cw=p['ffn_conv_w'][i], ffn_cb=p['ffn_conv_b'][i])
        if i % 2 == 0:
            lay.update(w_in=p['w_in_ab'][li].astype(BF16), w_out=p['w_out_ab'][li].astype(BF16),
                       s5=_s5_params(p['s5_a_re'][li], p['s5_a_im'][li], p['s5_log_step'][li],
                                     p['s5_b_re'][li], p['s5_b_im'][li], p['s5_c_re'][li], p['s5_c_im'][li],
                                     p['s5_d'][li]),
                       w_glu=p['s5_w_glu'][li].astype(BF16), b_glu=p['s5_b_glu'][li])
        else:
            lay.update(w_in=p['w_in_cd'][li].astype(BF16), w_out=p['w_out_cd'][li].astype(BF16),
                       rwkv=_rwkv_params(p, li), r_k=p['rwkv_r_k'][li], ln_w=p['rwkv_ln_w'][li],
                       ln_b=p['rwkv_ln_b'][li], cd_w=p['conv_d_w'][li], cd_b=p['conv_d_b'][li],
                       cd_g=p['conv_d_ln_g'][li], cd_lb=p['conv_d_ln_b'][li])
        layers.append(lay)
    return layers


def _trunk_seq(x3, layers, g_final):
    batch, seq, d = x3.shape
    x = x3.reshape(batch * seq, d)
    rope = _rope_tables(seq, 0, True)
    attn_projs, s5r, s5i, wkvs, shifts, convs, ffns = [], [], [], [], [], [], []
    v_first = None
    for i, lay in enumerate(layers):
        if i % 2 == 0:
            proj = _norm_matmul(x, lay['g_mix'], lay['w_in'], rope=rope, rope_cols=(1, 2))
            a_out, hr, hi = _s5_seq(proj, batch, lay['s5'], lay['w_glu'], lay['b_glu'])
            b_out = _moba_seq(proj, batch)
            attn_projs.append(proj)
            s5r.append(hr)
            s5i.append(hi)
            x = _out_proj(a_out, b_out, lay['w_out'], x)
        else:
            proj = _norm_matmul(x, lay['g_mix'], lay['w_in'])
            r, lw, k, v, kk, b, g = _rwkv_pre(proj, None, lay['rwkv'], v_first, batch)
            if v_first is None:
                v_first = v
            c_out, s_fin = _wkv_seq(r, lw, k, v, kk, b, g, lay['r_k'], lay['ln_w'], lay['ln_b'], batch)
            d_out, cbuf = _convd_seq(proj, batch, lay['cd_w'], lay['cd_b'], lay['cd_g'], lay['cd_lb'])
            wkvs.append(s_fin)
            shifts.append(proj.reshape(batch, seq, -1)[:, -1, :4 * D_HALF])
            convs.append(cbuf)
            x = _out_proj(c_out, d_out, lay['w_out'], x)
        x, fbuf = _ffn_seq(x, batch, lay['g_ffn'], lay['w_up'], lay['ffn_cw'], lay['ffn_cb'], lay['w_down'])
        ffns.append(fbuf)
    y = _rmsnorm(x, g_final).reshape(batch, seq, d)
    assert seq % PAGE_SIZE == 0
    k_all, v_all = _kv_layout(attn_projs, paged=True)
    return (y, k_all, v_all, jnp.stack(s5r), jnp.stack(s5i), jnp.stack(wkvs),
            jnp.stack(shifts), jnp.stack(convs), jnp.stack(ffns))


def _trunk_step(x3, pos0, layers, g_final, cache_k, cache_v, page_table, s5_re0, s5_im0, wkv0, shift0,
                convd0, ffn0):
    bsz, _, d = x3.shape
    x = x3.reshape(bsz, d)
    rope = _rope_tables(bsz, pos0, False)
    cache_kt = jnp.transpose(cache_k, (0, 1, 3, 4, 2))
    cache_vt = jnp.transpose(cache_v, (0, 1, 3, 4, 2))
    attn_projs, s5r, s5i, wkvs, shifts, convs, ffns = [], [], [], [], [], [], []
    v_first = None
    for i, lay in enumerate(layers):
        li = i // 2
        if i % 2 == 0:
            proj = _norm_matmul(x, lay['g_mix'], lay['w_in'], rope=rope, rope_cols=(1, 2))
            a_out, hr, hi = _s5_step(proj, s5_re0[li].reshape(bsz, -1), s5_im0[li].reshape(bsz, -1),
                                     lay['s5'], lay['w_glu'], lay['b_glu'])
            b_out = _moba_step(proj, cache_kt, cache_vt, li, page_table)
            attn_projs.append(proj)
            s5r.append(hr)
            s5i.append(hi)
            x = _out_proj(a_out, b_out, lay['w_out'], x)
        else:
            proj = _norm_matmul(x, lay['g_mix'], lay['w_in'])
            r, lw, k, v, kk, b, g = _rwkv_pre(proj, shift0[li], lay['rwkv'], v_first, bsz)
            if v_first is None:
                v_first = v
            c_out, s_fin = _wkv_step(wkv0[li], r, lw, k, v, kk, b, g, lay['r_k'], lay['ln_w'], lay['ln_b'])
            d_out, cbuf = _convd_step(proj, convd0[li], lay['cd_w'], lay['cd_b'], lay['cd_g'], lay['cd_lb'])
            wkvs.append(s_fin)
            shifts.append(proj[:, :4 * D_HALF])
            convs.append(cbuf)
            x = _out_proj(c_out, d_out, lay['w_out'], x)
        x, fbuf = _ffn_step(x, ffn0[i], lay['g_ffn'], lay['w_up'], lay['ffn_cw'], lay['ffn_cb'], lay['w_down'])
        ffns.append(fbuf)
    y = _rmsnorm(x, g_final).reshape(bsz, 1, d)
    k_all, v_all = _kv_layout(attn_projs, paged=False)
    return (y, k_all, v_all, jnp.stack(s5r), jnp.stack(s5i), jnp.stack(wkvs),
            jnp.stack(shifts), jnp.stack(convs), jnp.stack(ffns))


def kernel(x_prompt, x_sample, cache_k_moba, cache_v_moba, page_table, state_s5_re, state_s5_im,
           state_rwkv_wkv, state_rwkv_shift, state_conv_d, state_ffn_conv, g_mix, g_ffn, g_final,
           w_in_ab, w_out_ab, s5_a_re, s5_a_im, s5_log_step, s5_b_re, s5_b_im, s5_c_re, s5_c_im, s5_d,
           s5_w_glu, s5_b_glu, w_in_cd, w_out_cd, rwkv_mu_rkv, rwkv_mu_z, rwkv_w0, rwkv_w1, rwkv_w2,
           rwkv_a0, rwkv_a1, rwkv_a2, rwkv_g1, rwkv_g2, rwkv_k_k, rwkv_k_a, rwkv_r_k, rwkv_ln_w, rwkv_ln_b,
           rwkv_mu_v, rwkv_v0, rwkv_v1, rwkv_v2, conv_d_w, conv_d_b, conv_d_ln_g, conv_d_ln_b,
           ffn_w_up, ffn_conv_w, ffn_conv_b, ffn_w_down):
    p = dict(g_mix=g_mix, g_ffn=g_ffn, w_in_ab=w_in_ab, w_out_ab=w_out_ab,
             s5_a_re=s5_a_re, s5_a_im=s5_a_im, s5_log_step=s5_log_step, s5_b_re=s5_b_re, s5_b_im=s5_b_im,
             s5_c_re=s5_c_re, s5_c_im=s5_c_im, s5_d=s5_d, s5_w_glu=s5_w_glu, s5_b_glu=s5_b_glu,
             w_in_cd=w_in_cd, w_out_cd=w_out_cd, rwkv_mu_rkv=rwkv_mu_rkv, rwkv_mu_z=rwkv_mu_z,
             rwkv_w0=rwkv_w0, rwkv_w1=rwkv_w1, rwkv_w2=rwkv_w2, rwkv_a0=rwkv_a0, rwkv_a1=rwkv_a1,
             rwkv_a2=rwkv_a2, rwkv_g1=rwkv_g1, rwkv_g2=rwkv_g2, rwkv_k_k=rwkv_k_k, rwkv_k_a=rwkv_k_a,
             rwkv_r_k=rwkv_r_k.reshape(rwkv_r_k.shape[0], -1), rwkv_ln_w=rwkv_ln_w, rwkv_ln_b=rwkv_ln_b,
             rwkv_mu_v=rwkv_mu_v, rwkv_v0=rwkv_v0, rwkv_v1=rwkv_v1, rwkv_v2=rwkv_v2,
             conv_d_w=conv_d_w, conv_d_b=conv_d_b, conv_d_ln_g=conv_d_ln_g, conv_d_ln_b=conv_d_ln_b,
             ffn_w_up=ffn_w_up, ffn_conv_w=ffn_conv_w, ffn_conv_b=ffn_conv_b, ffn_w_down=ffn_w_down)
    layers = _prepare(p)
    bp, seq, _ = x_prompt.shape
    bs = x_sample.shape[0]
    n_ab = w_in_ab.shape[0]
    n_cd = w_in_cd.shape[0]
    page = cache_k_moba.shape[2]
    past_len = page_table.shape[1] * page
    (y_p, k_p, v_p, s5r_p, s5i_p, wkv_p, sh_p, cd_p, ff_p) = _trunk_seq(x_prompt, layers, g_final)
    (y_s, k_s, v_s, s5r_s, s5i_s, wkv_s, sh_s, cd_s, ff_s) = _trunk_step(
        x_sample, past_len, layers, g_final, cache_k_moba, cache_v_moba, page_table, state_s5_re, state_s5_im,
        state_rwkv_wkv, state_rwkv_shift, state_conv_d, state_ffn_conv)
    kv_s = (n_ab, bs, 1, HEADS, HEAD_DIM)
    s5_p = (n_ab, bp, S5_GROUPS, S5_STATE)
    s5_s = (n_ab, bs, S5_GROUPS, S5_STATE)

    def pages_out(t):
        t = t.reshape(n_ab, bp, seq // page, HEADS, HEAD_DIM, page)
        return jnp.transpose(t, (0, 1, 2, 5, 3, 4))

    return (y_p, y_s, pages_out(k_p), pages_out(v_p), k_s.reshape(kv_s), v_s.reshape(kv_s),
            s5r_p.reshape(s5_p), s5i_p.reshape(s5_p), s5r_s.reshape(s5_s), s5i_s.reshape(s5_s),
            wkv_p, wkv_s, sh_p, sh_s, cd_p, cd_s, ff_p, ff_s)
```

```python
import functools
import math

import numpy as np
import jax
import jax.numpy as jnp
from jax import lax
from jax.experimental import pallas as pl
from jax.experimental.pallas import tpu as pltpu

F32 = jnp.float32
BF16 = jnp.bfloat16
HIGHEST = lax.Precision.HIGHEST

D_MODEL = 1024
D_HALF = 512
S5_GROUP = 16
S5_GROUPS = 32
S5_STATE = 64
S5_LANES = S5_GROUPS * S5_STATE
HEADS = 8
HEAD_DIM = 64
MOBA_BLOCK = 256
MOBA_TOPK = 3
PAGE_SIZE = 128
ROPE_DIM = 16
ROPE_THETA = 500000.0
RWKV_GN_EPS = 64e-5
CONV_D_WIDTH = 31
D_FF = 2816
RMS_EPS = 1e-6
LN_EPS = 1e-5
LANE = 128
SUBLANE = 8
VMEM_LIMIT = 48 * 1024 * 1024
NEG = -1e30


def _cp(*sem):
    return pltpu.CompilerParams(dimension_semantics=sem, vmem_limit_bytes=VMEM_LIMIT)


def _bdot(a, b):
    return jnp.dot(a.astype(BF16), b.astype(BF16), preferred_element_type=F32)


def _sigmoid(x):
    return 1.0 / (1.0 + jnp.exp(-x))


def _silu(x):
    return x * _sigmoid(x)


def _row_tile(m, pref):
    return pref if m % pref == 0 else m


def _rope_table_kernel(inv_ref, cos_ref, sa_ref, sb_ref, *, pos0, per_row_pos):
    rows = cos_ref.shape[0]
    lane = lax.broadcasted_iota(jnp.int32, (rows, LANE), 1) % HEAD_DIM
    if per_row_pos:
        pos = (pos0 + lax.broadcasted_iota(jnp.int32, (rows, LANE), 0)).astype(F32)
    else:
        pos = jnp.full((rows, LANE), pos0, F32)
    ang = pos * inv_ref[...]
    c, s = jnp.cos(ang), jnp.sin(ang)
    cos_ref[...] = jnp.where(lane < ROPE_DIM, c, 1.0)
    sa_ref[...] = jnp.where(lane < ROPE_DIM // 2, -s, 0.0)
    sb_ref[...] = jnp.where((lane >= ROPE_DIM // 2) & (lane < ROPE_DIM), s, 0.0)


def _rope_tables(rows, pos0, per_row_pos):
    half = ROPE_DIM // 2
    inv8 = (np.float32(1.0) / (np.float32(ROPE_THETA) ** (np.arange(half, dtype=np.float32) / np.float32(half))))
    inv = np.zeros((HEAD_DIM,), np.float32)
    inv[:half] = inv8
    inv[half:ROPE_DIM] = inv8
    inv = jnp.asarray(np.tile(inv, LANE // HEAD_DIM)[None, :])
    shp = jax.ShapeDtypeStruct((rows, LANE), F32)
    return pl.pallas_call(
        functools.partial(_rope_table_kernel, pos0=pos0, per_row_pos=per_row_pos),
        out_shape=(shp, shp, shp))(inv)


def _norm_matmul_kernel(x_ref, g_ref, w_ref, *rest, rope_cols):
    if rope_cols:
        cos_ref, sa_ref, sb_ref, o_ref = rest
    else:
        (o_ref,) = rest
    x = x_ref[...]
    ms = jnp.mean(x * x, axis=-1, keepdims=True)
    h = (x * lax.rsqrt(ms + RMS_EPS) * g_ref[...]).astype(BF16)
    tn = D_HALF
    for j in range(o_ref.shape[1] // tn):
        y = jnp.dot(h, w_ref[:, j * tn:(j + 1) * tn], preferred_element_type=F32)
        if j not in rope_cols:
            o_ref[:, j * tn:(j + 1) * tn] = y
            continue
        cos, sa, sb = cos_ref[...], sa_ref[...], sb_ref[...]
        for c in range(tn // LANE):
            yc = y[:, c * LANE:(c + 1) * LANE]
            o_ref[:, j * tn + c * LANE:j * tn + (c + 1) * LANE] = (
                yc * cos + pltpu.roll(yc, LANE - ROPE_DIM // 2, axis=1) * sa
                + pltpu.roll(yc, ROPE_DIM // 2, axis=1) * sb)


def _norm_matmul(x, g, w_bf, rope=None, rope_cols=()):
    m, d = x.shape
    n = w_bf.shape[1]
    tm = _row_tile(m, 512)
    in_specs = [pl.BlockSpec((tm, d), lambda i: (i, 0)),
                pl.BlockSpec((1, d), lambda i: (0, 0)),
                pl.BlockSpec((d, n), lambda i: (0, 0))]
    args = [x, g.reshape(1, d), w_bf]
    if rope_cols:
        nt = rope[0].shape[0] // tm
        for t in rope:
            in_specs.append(pl.BlockSpec((tm, LANE), lambda i: (i % nt, 0)))
            args.append(t)
    return pl.pallas_call(
        functools.partial(_norm_matmul_kernel, rope_cols=tuple(rope_cols)),
        grid=(m // tm,),
        in_specs=in_specs,
        out_specs=pl.BlockSpec((tm, n), lambda i: (i, 0)),
        out_shape=jax.ShapeDtypeStruct((m, n), F32),
        compiler_params=_cp("parallel"))(*args)


def _out_proj_kernel(a_ref, b_ref, wa_ref, wb_ref, res_ref, o_ref):
    o_ref[...] = res_ref[...] + (_bdot(a_ref[...], wa_ref[...]) + _bdot(b_ref[...], wb_ref[...]))


def _out_proj(a, b, w_bf, res):
    m, c = a.shape
    n = w_bf.shape[1]
    tm = _row_tile(m, 512)
    return pl.pallas_call(
        _out_proj_kernel,
        grid=(m // tm,),
        in_specs=[pl.BlockSpec((tm, c), lambda i: (i, 0)),
                  pl.BlockSpec((tm, c), lambda i: (i, 0)),
                  pl.BlockSpec((c, n), lambda i: (0, 0)),
                  pl.BlockSpec((c, n), lambda i: (1, 0)),
                  pl.BlockSpec((tm, n), lambda i: (i, 0))],
        out_specs=pl.BlockSpec((tm, n), lambda i: (i, 0)),
        out_shape=jax.ShapeDtypeStruct((m, n), F32),
        compiler_params=_cp("parallel"))(a, b, w_bf, w_bf, res)


FFN_TN = 256
FFN_HALO = SUBLANE


def _ffn_seq_kernel(x_ref, xh_ref, g_ref, wa_ref, wb_ref, cwa_ref, cwb_ref, cba_ref, cbb_ref, wd_ref,
                    o_ref, ua_ref, ub_ref, h_ref, acc_ref, *, tiles_per_seq):
    i, c = pl.program_id(0), pl.program_id(1)
    tm = x_ref.shape[0]
    first = (i % tiles_per_seq) == 0

    @pl.when(c == 0)
    def _():
        def norm(x):
            ms = jnp.mean(x * x, axis=-1, keepdims=True)
            return (x * lax.rsqrt(ms + RMS_EPS) * g_ref[...]).astype(BF16)
        h_ref[0:FFN_HALO, :] = jnp.where(first, jnp.zeros((), BF16), norm(xh_ref[...]))
        h_ref[FFN_HALO:, :] = norm(x_ref[...])
        acc_ref[...] = jnp.zeros_like(acc_ref)

    h = h_ref[...]

    def branch(w_ref, cw_ref, cb_ref, tail_ref):
        u = jnp.dot(h, w_ref[...], preferred_element_type=F32)
        tail_ref[0] = u[tm + FFN_HALO - 2:, :]
        cw = cw_ref[...]
        y = (u * cw[2:3, :] + pltpu.roll(u, 1, axis=0) * cw[1:2, :]
             + pltpu.roll(u, 2, axis=0) * cw[0:1, :])
        return y[FFN_HALO:, :] + cb_ref[...]

    a = branch(wa_ref, cwa_ref, cba_ref, ua_ref)
    b = branch(wb_ref, cwb_ref, cbb_ref, ub_ref)
    acc_ref[...] += _bdot(_silu(a) * b, wd_ref[...])

    @pl.when(c == pl.num_programs(1) - 1)
    def _():
        o_ref[...] = x_ref[...] + acc_ref[...]


def _ffn_seq(x, batch, g, w_up_bf, cw, cb, w_down_bf):
    m, d = x.shape
    seq = m // batch
    tm = _row_tile(seq, 512)
    tps = seq // tm
    nc = D_FF // FFN_TN
    hb = tm // FFN_HALO
    out, ua, ub = pl.pallas_call(
        functools.partial(_ffn_seq_kernel, tiles_per_seq=tps),
        grid=(m // tm, nc),
        in_specs=[pl.BlockSpec((tm, d), lambda i, c: (i, 0)),
                  pl.BlockSpec((FFN_HALO, d), lambda i, c: (jnp.maximum(i * hb - 1, 0), 0)),
                  pl.BlockSpec((1, d), lambda i, c: (0, 0)),
                  pl.BlockSpec((d, FFN_TN), lambda i, c: (0, c)),
                  pl.BlockSpec((d, FFN_TN), lambda i, c: (0, nc + c)),
                  pl.BlockSpec((3, FFN_TN), lambda i, c: (0, c)),
                  pl.BlockSpec((3, FFN_TN), lambda i, c: (0, nc + c)),
                  pl.BlockSpec((1, FFN_TN), lambda i, c: (0, c)),
                  pl.BlockSpec((1, FFN_TN), lambda i, c: (0, nc + c)),
                  pl.BlockSpec((FFN_TN, d), lambda i, c: (c, 0))],
        out_specs=[pl.BlockSpec((tm, d), lambda i, c: (i, 0)),
                   pl.BlockSpec((1, 2, FFN_TN), lambda i, c: (i, 0, c)),
                   pl.BlockSpec((1, 2, FFN_TN), lambda i, c: (i, 0, c))],
        out_shape=[jax.ShapeDtypeStruct((m, d), F32),
                   jax.ShapeDtypeStruct((m // tm, 2, D_FF), F32),
                   jax.ShapeDtypeStruct((m // tm, 2, D_FF), F32)],
        scratch_shapes=[pltpu.VMEM((tm + FFN_HALO, d), BF16), pltpu.VMEM((tm, d), F32)],
        compiler_params=_cp("arbitrary", "arbitrary"),
    )(x, x, g.reshape(1, d), w_up_bf, w_up_bf, cw, cw, cb.reshape(1, -1), cb.reshape(1, -1), w_down_bf)
    return out, jnp.concatenate([ua[tps - 1::tps], ub[tps - 1::tps]], axis=-1)


def _ffn_step_kernel(x_ref, g_ref, wa_ref, wb_ref, cwa_ref, cwb_ref, cba_ref, cbb_ref, wd_ref,
                     bufa_ref, bufb_ref, o_ref, ua_ref, ub_ref, h_ref, acc_ref):
    c = pl.program_id(0)

    @pl.when(c == 0)
    def _():
        x = x_ref[...]
        ms = jnp.mean(x * x, axis=-1, keepdims=True)
        h_ref[...] = (x * lax.rsqrt(ms + RMS_EPS) * g_ref[...]).astype(BF16)
        acc_ref[...] = jnp.zeros_like(acc_ref)

    h = h_ref[...]

    def branch(w_ref, cw_ref, cb_ref, buf_ref, u_ref):
        u = jnp.dot(h, w_ref[...], preferred_element_type=F32)
        u_ref[...] = u
        cw = cw_ref[...]
        return u * cw[2:3, :] + buf_ref[1] * cw[1:2, :] + buf_ref[0] * cw[0:1, :] + cb_ref[...]

    a = branch(wa_ref, cwa_ref, cba_ref, bufa_ref, ua_ref)
    b = branch(wb_ref, cwb_ref, cbb_ref, bufb_ref, ub_ref)
    acc_ref[...] += _bdot(_silu(a) * b, wd_ref[...])

    @pl.when(c == pl.num_programs(0) - 1)
    def _():
        o_ref[...] = x_ref[...] + acc_ref[...]


def _ffn_step(x, buf, g, w_up_bf, cw, cb, w_down_bf):
    m, d = x.shape
    nc = D_FF // FFN_TN
    buf_t = jnp.swapaxes(buf, 0, 1)
    out, ua, ub = pl.pallas_call(
        _ffn_step_kernel,
        grid=(nc,),
        in_specs=[pl.BlockSpec((m, d), lambda c: (0, 0)),
                  pl.BlockSpec((1, d), lambda c: (0, 0)),
                  pl.BlockSpec((d, FFN_TN), lambda c: (0, c)),
                  pl.BlockSpec((d, FFN_TN), lambda c: (0, nc + c)),
                  pl.BlockSpec((3, FFN_TN), lambda c: (0, c)),
                  pl.BlockSpec((3, FFN_TN), lambda c: (0, nc + c)),
                  pl.BlockSpec((1, FFN_TN), lambda c: (0, c)),
                  pl.BlockSpec((1, FFN_TN), lambda c: (0, nc + c)),
                  pl.BlockSpec((FFN_TN, d), lambda c: (c, 0)),
                  pl.BlockSpec((2, m, FFN_TN), lambda c: (0, 0, c)),
                  pl.BlockSpec((2, m, FFN_TN), lambda c: (0, 0, nc + c))],
        out_specs=[pl.BlockSpec((m, d), lambda c: (0, 0)),
                   pl.BlockSpec((m, FFN_TN), lambda c: (0, c)),
                   pl.BlockSpec((m, FFN_TN), lambda c: (0, c))],
        out_shape=[jax.ShapeDtypeStruct((m, d), F32),
                   jax.ShapeDtypeStruct((m, D_FF), F32),
                   jax.ShapeDtypeStruct((m, D_FF), F32)],
        scratch_shapes=[pltpu.VMEM((m, d), BF16), pltpu.VMEM((m, d), F32)],
        compiler_params=_cp("arbitrary"),
    )(x, g.reshape(1, d), w_up_bf, w_up_bf, cw, cw, cb.reshape(1, -1), cb.reshape(1, -1), w_down_bf,
      buf_t, buf_t)
    u = jnp.concatenate([ua, ub], axis=-1)
    return out, jnp.stack([buf[:, 1], u], axis=1)


S5_STRIP = 512
S5_NSTRIP = S5_LANES // S5_STRIP


def _s5_param_kernel(are_ref, aim_ref, ls_ref, brt_ref, bit_ref, abr_ref, abi_ref, bbr_ref, bbi_ref):
    ar, ai = are_ref[...], aim_ref[...]
    dt = jnp.exp(ls_ref[...])
    mag = jnp.exp(ar * dt)
    abr, abi = mag * jnp.cos(ai * dt), mag * jnp.sin(ai * dt)
    inv_abs2 = 1.0 / (ar * ar + ai * ai)
    cr = ((abr - 1.0) * ar + abi * ai) * inv_abs2
    ci = (abi * ar - (abr - 1.0) * ai) * inv_abs2
    br, bi = brt_ref[...], bit_ref[...]
    abr_ref[...] = abr
    abi_ref[...] = abi
    bbr_ref[...] = cr * br - ci * bi
    bbi_ref[...] = cr * bi + ci * br


def _s5_params(a_re, a_im, log_step, b_re, b_im, c_re, c_im, d_skip):
    g, n, k = b_re.shape
    rep = lambda t: jnp.repeat(t, k, axis=0)
    brt = b_re.transpose(0, 2, 1).reshape(g * k, n)
    bit = b_im.transpose(0, 2, 1).reshape(g * k, n)
    shp = jax.ShapeDtypeStruct((g * k, n), F32)
    abr, abi, bbr, bbi = pl.pallas_call(_s5_param_kernel, out_shape=(shp, shp, shp, shp))(
        rep(a_re), rep(a_im), rep(log_step[:, None]), brt, bit)
    abr = abr[::k].reshape(1, g * n)
    abi = abi[::k].reshape(1, g * n)
    gs = S5_STRIP // n
    eye = jnp.eye(gs, dtype=F32)

    def in_blocks(t):
        t = t.reshape(S5_NSTRIP, gs, k, n)
        return jnp.einsum('cgkn,gh->cgkhn', t, eye).reshape(S5_NSTRIP, gs * k, gs * n).astype(BF16)

    def out_blocks(t):
        t = t.reshape(S5_NSTRIP, gs, k, n)
        return jnp.einsum('cgkn,gh->cgnhk', t, eye).reshape(S5_NSTRIP, gs * n, gs * k).astype(BF16)

    return dict(abr=abr, abi=abi, wbr=in_blocks(bbr), wbi=in_blocks(bbi),
                wcr=out_blocks(c_re), wci=out_blocks(c_im), d=d_skip.reshape(1, g * k))


def _s5_input(u, wbr_ref, wbi_ref, c):
    cw = S5_STRIP // (S5_STATE // S5_GROUP)
    uc = u[:, c * cw:(c + 1) * cw].astype(BF16)
    return (jnp.dot(uc, wbr_ref[c], preferred_element_type=F32),
            jnp.dot(uc, wbi_ref[c], preferred_element_type=F32))


def _s5_output(u, h_strip, wcr_ref, wci_ref, d_ref, wg_ref, bg_ref):
    ys = []
    for c in range(S5_NSTRIP):
        hr, hi = h_strip(c)
        ys.append(jnp.dot(hr.astype(BF16), wcr_ref[c], preferred_element_type=F32)
                  - jnp.dot(hi.astype(BF16), wci_ref[c], preferred_element_type=F32))
    y = jnp.concatenate(ys, axis=-1) + d_ref[...] * u
    yg = 0.5 * y * (1.0 + jnp.tanh(math.sqrt(2.0 / math.pi) * (y + 0.044715 * (y * y * y))))
    z = jnp.dot(yg.astype(BF16), wg_ref[...], preferred_element_type=F32) + bg_ref[...]
    return yg * _sigmoid(z)


def _cmul(ar, ai, br, bi):
    return ar * br - ai * bi, ar * bi + ai * br


def _s5_seq_kernel(u_ref, abr_ref, abi_ref, wbr_ref, wbi_ref, wcr_ref, wci_ref, d_ref, wg_ref, bg_ref,
                   o_ref, hro_ref, hio_ref, hr_ref, hi_ref, cr_ref, ci_ref, tab_ref):
    i = pl.program_id(1)
    tt = u_ref.shape[0]

    @pl.when(i == 0)
    def _():
        cr_ref[...] = jnp.zeros_like(cr_ref)
        ci_ref[...] = jnp.zeros_like(ci_ref)
        row = lax.broadcasted_iota(jnp.int32, (SUBLANE, S5_LANES), 0)
        p_r, p_i = [abr_ref[...]], [abi_ref[...]]
        for _ in range(SUBLANE - 1):
            nr, ni = _cmul(p_r[-1], p_i[-1], p_r[0], p_i[0])
            p_r.append(nr)
            p_i.append(ni)
        for k, s in enumerate((1, 2, 4)):
            tab_ref[2 * k] = jnp.where(row >= s, p_r[s - 1], 0.0)
            tab_ref[2 * k + 1] = jnp.where(row >= s, p_i[s - 1], 0.0)
        car_r = jnp.zeros((SUBLANE, S5_LANES), F32)
        car_i = jnp.zeros((SUBLANE, S5_LANES), F32)
        for j in range(SUBLANE):
            car_r = jnp.where(row == j, p_r[j], car_r)
            car_i = jnp.where(row == j, p_i[j], car_i)
        tab_ref[6] = car_r
        tab_ref[7] = car_i

    u = u_ref[...]
    for c in range(S5_NSTRIP):
        br, bi = _s5_input(u, wbr_ref, wbi_ref, c)
        hr_ref[:, c * S5_STRIP:(c + 1) * S5_STRIP] = br
        hi_ref[:, c * S5_STRIP:(c + 1) * S5_STRIP] = bi

    def block(r, carry):
        r8 = pl.multiple_of(r * SUBLANE, SUBLANE)
        for c in range(S5_NSTRIP):
            sl = slice(c * S5_STRIP, (c + 1) * S5_STRIP)
            xr, xi = hr_ref[pl.ds(r8, SUBLANE), sl], hi_ref[pl.ds(r8, SUBLANE), sl]
            for k, s in enumerate((1, 2, 4)):
                dr, di = _cmul(tab_ref[2 * k, :, sl], tab_ref[2 * k + 1, :, sl],
                               pltpu.roll(xr, s, axis=0), pltpu.roll(xi, s, axis=0))
                xr, xi = xr + dr, xi + di
            dr, di = _cmul(tab_ref[6, :, sl], tab_ref[7, :, sl], cr_ref[:, sl], ci_ref[:, sl])
            xr, xi = xr + dr, xi + di
            hr_ref[pl.ds(r8, SUBLANE), sl] = xr
            hi_ref[pl.ds(r8, SUBLANE), sl] = xi
            cr_ref[:, sl] = xr[SUBLANE - 1:, :]
            ci_ref[:, sl] = xi[SUBLANE - 1:, :]
        return carry

    lax.fori_loop(0, tt // SUBLANE, block, 0)

    def h_strip(c):
        sl = slice(c * S5_STRIP, (c + 1) * S5_STRIP)
        return hr_ref[:, sl], hi_ref[:, sl]

    o_ref[...] = _s5_output(u, h_strip, wcr_ref, wci_ref, d_ref, wg_ref, bg_ref)
    hro_ref[0] = cr_ref[...]
    hio_ref[0] = ci_ref[...]


def _full(shape):
    nd = len(shape)
    return pl.BlockSpec(shape, lambda *_: (0,) * nd)


def _s5_seq(proj, batch, sp, w_glu_bf, b_glu):
    m = proj.shape[0]
    seq = m // batch
    tt = _row_tile(seq, 256)
    nt = seq // tt
    params = [sp['abr'], sp['abi'], sp['wbr'], sp['wbi'], sp['wcr'], sp['wci'], sp['d'],
              w_glu_bf, b_glu.reshape(1, -1)]
    out, hr, hi = pl.pallas_call(
        _s5_seq_kernel,
        grid=(batch, nt),
        in_specs=[pl.BlockSpec((tt, D_HALF), lambda b, i: (b * nt + i, 0))] + [_full(p.shape) for p in params],
        out_specs=[pl.BlockSpec((tt, D_HALF), lambda b, i: (b * nt + i, 0)),
                   pl.BlockSpec((1, 1, S5_LANES), lambda b, i: (b, 0, 0)),
                   pl.BlockSpec((1, 1, S5_LANES), lambda b, i: (b, 0, 0))],
        out_shape=[jax.ShapeDtypeStruct((m, D_HALF), F32),
                   jax.ShapeDtypeStruct((batch, 1, S5_LANES), F32),
                   jax.ShapeDtypeStruct((batch, 1, S5_LANES), F32)],
        scratch_shapes=[pltpu.VMEM((tt, S5_LANES), F32), pltpu.VMEM((tt, S5_LANES), F32),
                        pltpu.VMEM((1, S5_LANES), F32), pltpu.VMEM((1, S5_LANES), F32),
                        pltpu.VMEM((8, SUBLANE, S5_LANES), F32)],
        compiler_params=_cp("arbitrary", "arbitrary"))(proj, *params)
    return out, hr[:, 0], hi[:, 0]


def _s5_step_kernel(u_ref, h0r_ref, h0i_ref, abr_ref, abi_ref, wbr_ref, wbi_ref, wcr_ref, wci_ref, d_ref,
                    wg_ref, bg_ref, o_ref, hro_ref, hio_ref):
    u = u_ref[...]
    for c in range(S5_NSTRIP):
        sl = slice(c * S5_STRIP, (c + 1) * S5_STRIP)
        br, bi = _s5_input(u, wbr_ref, wbi_ref, c)
        dr, di = _cmul(abr_ref[:, sl], abi_ref[:, sl], h0r_ref[:, sl], h0i_ref[:, sl])
        hro_ref[:, sl] = dr + br
        hio_ref[:, sl] = di + bi

    def h_strip(c):
        sl = slice(c * S5_STRIP, (c + 1) * S5_STRIP)
        return hro_ref[:, sl], hio_ref[:, sl]

    o_ref[...] = _s5_output(u, h_strip, wcr_ref, wci_ref, d_ref, wg_ref, bg_ref)


def _s5_step(proj, h0r, h0i, sp, w_glu_bf, b_glu):
    m = proj.shape[0]
    args = [h0r, h0i, sp['abr'], sp['abi'], sp['wbr'], sp['wbi'], sp['wcr'], sp['wci'], sp['d'],
            w_glu_bf, b_glu.reshape(1, -1)]
    return pl.pallas_call(
        _s5_step_kernel,
        grid=(1,),
        in_specs=[pl.BlockSpec((m, D_HALF), lambda i: (0, 0))] + [_full(a.shape) for a in args],
        out_specs=[_full((m, D_HALF)), _full((m, S5_LANES)), _full((m, S5_LANES))],
        out_shape=[jax.ShapeDtypeStruct((m, D_HALF), F32),
                   jax.ShapeDtypeStruct((m, S5_LANES), F32),
                   jax.ShapeDtypeStruct((m, S5_LANES), F32)],
        compiler_params=_cp("arbitrary"))(proj, *args)


HEADS_PER_TILE = LANE // HEAD_DIM
HEAD_TILES = HEADS // HEADS_PER_TILE
ATTN_SCALE = HEAD_DIM ** -0.5
NT_DIMS = (((1,), (1,)), ((), ()))
MOBA_GROUP = 4


def _moba_seq_kernel(q_ref, k_ref, v_ref, o_ref, kbf_ref, vt_ref, kmean_ref, sel_ref, *, nb):
    qi = pl.program_id(2)
    blk = MOBA_BLOCK

    @pl.when(qi == 0)
    def _():
        kbf_ref[...] = k_ref[...].astype(BF16)
        vt_ref[...] = v_ref[...].T.astype(BF16)
        for j in range(nb):
            kmean_ref[j:j + 1, :] = jnp.mean(k_ref[j * blk:(j + 1) * blk, :], axis=0, keepdims=True)

    q = q_ref[...]
    row = lax.broadcasted_iota(jnp.int32, (nb, blk), 0)
    causal = (lax.broadcasted_iota(jnp.int32, (blk, blk), 0) <= lax.broadcasted_iota(jnp.int32, (blk, blk), 1))
    q0 = pl.multiple_of(qi * blk, blk)
    head_lanes = [slice(h * HEAD_DIM, (h + 1) * HEAD_DIM) for h in range(HEADS_PER_TILE)]
    qts, state = [], []
    for h, sl in enumerate(head_lanes):
        qh = q[:, sl]
        s = lax.dot_general(kmean_ref[:, sl], qh, NT_DIMS, precision=HIGHEST, preferred_element_type=F32)
        rank = jnp.zeros((nb, blk), F32)
        for i in range(nb):
            si = s[i:i + 1, :]
            beats = jnp.logical_or(si > s, jnp.logical_and(si == s, i < row))
            rank = rank + jnp.where(jnp.logical_and(beats, i < qi), 1.0, 0.0)
        sel_ref[h] = jnp.where(jnp.logical_and(row < qi, rank < MOBA_TOPK), 1.0, 0.0)

        qt = (qh * ATTN_SCALE).T.astype(BF16)
        lg = jnp.dot(kbf_ref[pl.ds(q0, blk), sl], qt, preferred_element_type=F32)
        lg = jnp.where(causal, lg, NEG)
        m = jnp.max(lg, axis=0, keepdims=True)
        p = jnp.exp(lg - m)
        l = jnp.sum(p, axis=0, keepdims=True)
        acc = jnp.dot(vt_ref[sl, pl.ds(q0, blk)], p.astype(BF16), preferred_element_type=F32)
        qts.append(qt)
        state += [m, l, acc]

    grp = MOBA_GROUP
    zero = jnp.zeros_like(qts[0])
    qt_both = jnp.concatenate([jnp.concatenate([qts[0], zero], axis=1),
                               jnp.concatenate([zero, qts[1]], axis=1)], axis=0)

    def body(jj, carry):
        j0 = pl.multiple_of(jj * (grp * blk), grp * blk)
        lg_both = jnp.dot(kbf_ref[pl.ds(j0, grp * blk), :], qt_both, preferred_element_type=F32)
        out = []
        for h, sl in enumerate(head_lanes):
            m, l, acc = carry[3 * h:3 * h + 3]
            lg = lg_both[:, h * blk:(h + 1) * blk]
            lg = jnp.concatenate(
                [jnp.where(sel_ref[h, pl.ds(jj * grp + g, 1), :] > 0.0, lg[g * blk:(g + 1) * blk], NEG)
                 for g in range(grp)], axis=0)
            m_new = jnp.maximum(m, jnp.max(lg, axis=0, keepdims=True))
            alpha = jnp.exp(m - m_new)
            p = jnp.exp(lg - m_new)
            l = alpha * l + jnp.sum(p, axis=0, keepdims=True)
            acc = alpha * acc + jnp.dot(vt_ref[sl, pl.ds(j0, grp * blk)], p.astype(BF16),
                                        preferred_element_type=F32)
            out += [m_new, l, acc]
        return tuple(out)

    state = lax.fori_loop(0, (qi + grp - 1) // grp, body, tuple(state))
    for h, sl in enumerate(head_lanes):
        m, l, acc = state[3 * h:3 * h + 3]
        o_ref[:, sl] = (acc / l).T


def _moba_seq(proj, batch):
    m = proj.shape[0]
    seq = m // batch
    assert seq % (MOBA_BLOCK * MOBA_GROUP) == 0
    nb = seq // MOBA_BLOCK
    qoff, koff, voff = HEAD_TILES, 2 * HEAD_TILES, 3 * HEAD_TILES
    return pl.pallas_call(
        functools.partial(_moba_seq_kernel, nb=nb),
        grid=(batch, HEAD_TILES, nb),
        in_specs=[pl.BlockSpec((MOBA_BLOCK, LANE), lambda b, t, i: (b * nb + i, qoff + t)),
                  pl.BlockSpec((seq, LANE), lambda b, t, i: (b, koff + t)),
                  pl.BlockSpec((seq, LANE), lambda b, t, i: (b, voff + t))],
        out_specs=pl.BlockSpec((MOBA_BLOCK, LANE), lambda b, t, i: (b * nb + i, t)),
        out_shape=jax.ShapeDtypeStruct((m, D_HALF), F32),
        scratch_shapes=[pltpu.VMEM((seq, LANE), BF16), pltpu.VMEM((LANE, seq), BF16),
                        pltpu.VMEM((nb, LANE), F32), pltpu.VMEM((HEADS_PER_TILE, nb, MOBA_BLOCK), F32)],
        compiler_params=_cp("parallel", "parallel", "arbitrary"))(proj, proj, proj)


MOBA_PAGES_PER_STEP = 16
PAGES_PER_BLOCK = MOBA_BLOCK // PAGE_SIZE


def _moba_rank_kernel(pt_ref, q_ref, *rest, nblk):
    k_refs = rest[:MOBA_PAGES_PER_STEP]
    idx_ref, s_ref = rest[MOBA_PAGES_PER_STEP:]
    j = pl.program_id(1)
    bps = MOBA_PAGES_PER_STEP // PAGES_PER_BLOCK
    lane = lax.broadcasted_iota(jnp.int32, (HEADS, LANE), 1)

    @pl.when(j == 0)
    def _():
        s_ref[...] = jnp.zeros_like(s_ref)

    q = q_ref[...]
    acc = s_ref[...]
    for t in range(bps):
        per_token = sum(jnp.sum(k_refs[t * PAGES_PER_BLOCK + p][...] * q, axis=1)
                        for p in range(PAGES_PER_BLOCK))
        score = jnp.sum(per_token, axis=-1, keepdims=True) * (1.0 / MOBA_BLOCK)
        acc = acc + jnp.where(lane == j * bps + t, score, 0.0)
    s_ref[...] = acc

    @pl.when(j == pl.num_programs(1) - 1)
    def _():
        s = acc
        rank = jnp.zeros((HEADS, LANE), jnp.int32)
        for i in range(nblk):
            si = s[:, i:i + 1]
            beats = jnp.logical_or(si > s, jnp.logical_and(si == s, i < lane))
            rank = rank + beats.astype(jnp.int32)
        out = jnp.zeros((HEADS, LANE), jnp.int32)
        for slot in range(MOBA_TOPK):
            hit = jnp.logical_and(rank == slot, lane < nblk)
            idx = jnp.sum(jnp.where(hit, lane, 0), axis=-1, keepdims=True)
            out = jnp.where(lane == slot, idx, out)
        idx_ref[...] = out


def _moba_pick_kernel(pg_ref, q_ref, kn_ref, vn_ref, *rest):
    n = MOBA_TOPK * PAGES_PER_BLOCK
    k_refs, v_refs, o_ref = rest[:n], rest[n:2 * n], rest[2 * n]
    q = q_ref[...] * ATTN_SCALE
    lg_self = jnp.sum(q * kn_ref[...], axis=0, keepdims=True)
    lgs = [jnp.sum(k_ref[...] * q, axis=0, keepdims=True) for k_ref in k_refs]
    m = lg_self
    for lg in lgs:
        m = jnp.maximum(m, jnp.max(lg, axis=-1, keepdims=True))
    p_self = jnp.exp(lg_self - m)
    l = p_self
    acc = p_self * vn_ref[...]
    for lg, v_ref in zip(lgs, v_refs):
        p = jnp.exp(lg - m)
        l = l + jnp.sum(p, axis=-1, keepdims=True)
        acc = acc + jnp.sum(v_ref[...] * p, axis=-1, keepdims=True)
    o_ref[...] = acc / l


def _moba_step(proj, cache_kt, cache_vt, layer, page_table):
    bsz = proj.shape[0]
    n_pages = page_table.shape[1]
    assert n_pages % MOBA_PAGES_PER_STEP == 0 and PAGE_SIZE == LANE
    nblk = n_pages // PAGES_PER_BLOCK
    assert nblk <= LANE
    steps = n_pages // MOBA_PAGES_PER_STEP
    cols = proj.reshape(bsz, 4, HEADS, HEAD_DIM, 1)
    q_col, k_col, v_col = cols[:, 1], cols[:, 2], cols[:, 3]

    def page_spec(t):
        return pl.BlockSpec((None, None, HEADS, HEAD_DIM, PAGE_SIZE),
                            lambda b, j, pt: (layer, pt[b, j * MOBA_PAGES_PER_STEP + t], 0, 0, 0))

    idx = pl.pallas_call(
        functools.partial(_moba_rank_kernel, nblk=nblk),
        grid_spec=pltpu.PrefetchScalarGridSpec(
            num_scalar_prefetch=1,
            grid=(bsz, steps),
            in_specs=[pl.BlockSpec((None, HEADS, HEAD_DIM, 1), lambda b, j, pt: (b, 0, 0, 0))]
            + [page_spec(t) for t in range(MOBA_PAGES_PER_STEP)],
            out_specs=pl.BlockSpec((None, HEADS, LANE), lambda b, j, pt: (b, 0, 0)),
            scratch_shapes=[pltpu.VMEM((HEADS, LANE), F32)]),
        out_shape=jax.ShapeDtypeStruct((bsz, HEADS, LANE), jnp.int32),
        compiler_params=_cp("arbitrary", "arbitrary"),
    )(page_table, q_col, *([cache_kt] * MOBA_PAGES_PER_STEP))

    top = idx[:, :, :MOBA_TOPK]
    logical = top[..., None] * PAGES_PER_BLOCK + jnp.arange(PAGES_PER_BLOCK, dtype=jnp.int32)
    pages = jnp.take_along_axis(page_table, logical.reshape(bsz, -1), axis=1).reshape(-1)
    per_head = MOBA_TOPK * PAGES_PER_BLOCK

    def pick_spec(t):
        return pl.BlockSpec((None, None, None, HEAD_DIM, PAGE_SIZE),
                            lambda b, h, pg: (layer, pg[(b * HEADS + h) * per_head + t], h, 0, 0))

    tok_spec = pl.BlockSpec((None, None, HEAD_DIM, 1), lambda b, h, pg: (b, h, 0, 0))
    out = pl.pallas_call(
        _moba_pick_kernel,
        grid_spec=pltpu.PrefetchScalarGridSpec(
            num_scalar_prefetch=1,
            grid=(bsz, HEADS),
            in_specs=[tok_spec] * 3 + [pick_spec(t) for t in range(per_head)] * 2,
            out_specs=tok_spec),
        out_shape=jax.ShapeDtypeStruct((bsz, HEADS, HEAD_DIM, 1), F32),
        compiler_params=_cp("arbitrary", "arbitrary"),
    )(pages, q_col, k_col, v_col, *([cache_kt] * per_head), *([cache_vt] * per_head))
    return out.reshape(bsz, D_HALF)


def _seg_sum(x):
    seg = lax.broadcasted_iota(jnp.int32, (LANE, LANE), 0) // HEAD_DIM
    ones = jnp.where(seg == lax.broadcasted_iota(jnp.int32, (LANE, LANE), 1) // HEAD_DIM, 1.0, 0.0).astype(BF16)
    hi = x.astype(BF16)
    r1 = x - hi.astype(F32)
    mid = r1.astype(BF16)
    lo = (r1 - mid.astype(F32)).astype(BF16)
    parts = []
    for c in range(x.shape[1] // LANE):
        sl = slice(c * LANE, (c + 1) * LANE)
        d = lambda p: jnp.dot(p[:, sl], ones, preferred_element_type=F32)
        parts.append(d(hi) + (d(mid) + d(lo)))
    return jnp.concatenate(parts, axis=-1)


def _softplus(x):
    return jnp.maximum(x, 0.0) + jnp.log(1.0 + jnp.exp(-jnp.abs(x)))


def _rwkv_mix(cat, prev, p, vfirst):
    c = D_HALF
    dlt = prev - cat
    part = lambda t, j: t[:, j * c:(j + 1) * c]
    mu, mz = p['mu'][...], p['mz'][...]
    r = part(cat, 0) + part(dlt, 0) * mu[0:1]
    k = part(cat, 1) + part(dlt, 1) * mu[1:2]
    v = part(cat, 2) + part(dlt, 2) * mu[2:3]
    z0, dz = part(cat, 3), part(dlt, 3)
    zw, za, zg = z0 + dz * mz[0:1], z0 + dz * mz[1:2], z0 + dz * mz[2:3]
    w_raw = -_softplus(-(p['w0'][...] + _bdot(jnp.tanh(_bdot(zw, p['w1'][...])), p['w2'][...]))) - 0.5
    log_decay = -jnp.exp(w_raw)
    a = _sigmoid(p['a0'][...] + _bdot(_bdot(za, p['a1'][...]), p['a2'][...]))
    g = _bdot(_sigmoid(_bdot(zg, p['g1'][...])), p['g2'][...])
    if vfirst is not None:
        zv = z0 + dz * p['mu_v'][...]
        vmix = _sigmoid(p['v0'][...] + _bdot(_bdot(zv, p['v1'][...]), p['v2'][...]))
        v = v + (vfirst - v) * vmix
    kk = k * p['k_k'][...]
    kk = kk / jnp.maximum(jnp.sqrt(_seg_sum(kk * kk)), 1e-12)
    k = k * (1.0 + (a - 1.0) * p['k_a'][...])
    return r, log_decay, k, v, kk, kk * a, g


RWKV_PARAM_NAMES = ('mu', 'mz', 'w0', 'w1', 'w2', 'a0', 'a1', 'a2', 'g1', 'g2', 'k_k', 'k_a')
RWKV_VRES_NAMES = ('mu_v', 'v0', 'v1', 'v2')
RWKV_N_OUT = 7


def _rwkv_pre_kernel(*refs, names, has_vfirst, seq_tiles):
    refs = list(refs)
    cat_ref = refs.pop(0)
    prev_ref = refs.pop(0)
    vf_ref = refs.pop(0) if has_vfirst else None
    p = {n: refs.pop(0) for n in names}
    outs = refs
    cat = cat_ref[...]
    if seq_tiles:
        first = (pl.program_id(0) % seq_tiles) == 0
        row = lax.broadcasted_iota(jnp.int32, (cat.shape[0], 1), 0)
        before = jnp.where(first, 0.0, prev_ref[SUBLANE - 1:SUBLANE, :])
        prev = jnp.where(row == 0, before, pltpu.roll(cat, 1, axis=0))
    else:
        prev = prev_ref[...]
    res = _rwkv_mix(cat, prev, p, None if vf_ref is None else vf_ref[...])
    for o_ref, val in zip(outs, res):
        o_ref[...] = val


def _rwkv_pre(proj, prev, rp, vfirst, batch):
    m = proj.shape[0]
    cw = 4 * D_HALF
    names = RWKV_PARAM_NAMES + (RWKV_VRES_NAMES if vfirst is not None else ())
    params = [rp[n] for n in names]
    if prev is None:
        seq = m // batch
        tm = _row_tile(seq, 256)
        seq_tiles = seq // tm
        hb = tm // SUBLANE
        prev_arg = proj
        prev_spec = pl.BlockSpec((SUBLANE, cw), lambda i: (jnp.maximum(i * hb - 1, 0), 0))
    else:
        tm, seq_tiles = m, 0
        prev_arg = prev
        prev_spec = pl.BlockSpec((tm, cw), lambda i: (i, 0))
    args = [proj, prev_arg]
    in_specs = [pl.BlockSpec((tm, cw), lambda i: (i, 0)), prev_spec]
    if vfirst is not None:
        args.append(vfirst)
        in_specs.append(pl.BlockSpec((tm, D_HALF), lambda i: (i, 0)))
    args += params
    in_specs += [_full(a.shape) for a in params]
    return pl.pallas_call(
        functools.partial(_rwkv_pre_kernel, names=names, has_vfirst=vfirst is not None, seq_tiles=seq_tiles),
        grid=(m // tm,),
        in_specs=in_specs,
        out_specs=[pl.BlockSpec((tm, D_HALF), lambda i: (i, 0))] * RWKV_N_OUT,
        out_shape=[jax.ShapeDtypeStruct((m, D_HALF), F32)] * RWKV_N_OUT,
        compiler_params=_cp("parallel"))(*args)


WKV_CHUNK = 64


def _bmm(a, b):
    return lax.dot_general(a.astype(BF16), b.astype(BF16), (((2,), (1,)), ((0,), (0,))),
                           preferred_element_type=F32)


def _bmm3(a, b):
    ah = a.astype(BF16)
    al = (a - ah.astype(F32)).astype(BF16)
    bh = b.astype(BF16)
    bl = (b - bh.astype(F32)).astype(BF16)
    return _bmm(ah, bh) + (_bmm(ah, bl) + _bmm(al, bh))


def _wkv_tile(r, lw, k, v, kk, b, st):
    tt = r.shape[0]
    c, n, nh = WKV_CHUNK, HEAD_DIM, HEADS_PER_TILE
    nc = tt // c
    ri = lax.broadcasted_iota(jnp.int32, (1, c, c), 1)
    ci = lax.broadcasted_iota(jnp.int32, (1, c, c), 2)
    incl, strict = ri >= ci, ri > ci
    eye = jnp.where(ri == ci, 1.0, 0.0)
    tril = jnp.where(incl[0], 1.0, 0.0)
    lw3 = lw.reshape(nc, c, LANE)
    cum = jnp.stack([jnp.dot(tril, lw3[j], precision=HIGHEST, preferred_element_type=F32) for j in range(nc)])
    tot = cum[:, c - 1:c, :]
    e_neg = jnp.exp(-cum)
    e_end = jnp.exp(tot - cum)
    e_tot = jnp.exp(tot)

    def heads(x):
        return jnp.concatenate([x[:, :, h * n:(h + 1) * n] for h in range(nh)], axis=0)

    def chunks(x):
        return x.reshape(nc, c, LANE)

    kkm = heads(chunks(kk) * jnp.exp(cum - lw3))
    rp = heads(chunks(r) * jnp.exp(cum))
    bo, ko = heads(chunks(b) * e_neg), heads(chunks(k) * e_neg)
    bend, kend = heads(chunks(b) * e_end), heads(chunks(k) * e_end)
    vh = heads(chunks(v))
    e_tot = heads(e_tot)

    qa = lax.dot_general(jnp.concatenate([kkm, rp], axis=1).astype(BF16),
                         jnp.concatenate([bo, ko], axis=1).astype(BF16),
                         (((2,), (2,)), ((0,), (0,))), preferred_element_type=F32)
    a_ub = jnp.where(strict, qa[:, :c, :c], 0.0)
    a_vk = jnp.where(strict, qa[:, :c, c:], 0.0)
    a_rb = jnp.where(incl, qa[:, c:, :c], 0.0)
    a_rk = jnp.where(incl, qa[:, c:, c:], 0.0)

    m = 2
    t_inv = eye - jnp.where((ri // m) == (ci // m), a_ub, 0.0)
    while m < c:
        off = jnp.logical_and((ri // (2 * m)) == (ci // (2 * m)), (ri // m) != (ci // m))
        t_inv = t_inv - _bmm(_bmm(t_inv, jnp.where(off, a_ub, 0.0)), t_inv)
        m *= 2

    av = _bmm(jnp.concatenate([a_vk, a_rk], axis=1), vh)
    x1 = _bmm(t_inv, jnp.concatenate([kkm, av[:, :c]], axis=2))
    x2 = _bmm(a_rb, x1)
    x3 = _bmm(jnp.swapaxes(bend, 1, 2), x1)
    rr = rp - x2[:, :, :n]
    o_loc = av[:, c:] - x2[:, :, n:]
    trans = eye * e_tot - x3[:, :, :n]
    s_loc = _bmm(jnp.swapaxes(kend, 1, 2), vh) - x3[:, :, n:]

    by_head = lambda x: x.reshape(nh, nc, x.shape[1], x.shape[2])
    rr, o_loc, trans, s_loc = by_head(rr), by_head(o_loc), by_head(trans), by_head(s_loc)
    outs = []
    for j in range(nc):
        outs.append(_bmm(rr[:, j], st) + o_loc[:, j])
        st = _bmm3(trans[:, j], st) + s_loc[:, j]
    o = jnp.stack(outs, axis=1).reshape(nh, tt, n)
    return jnp.concatenate([o[h] for h in range(nh)], axis=-1), st


WKV_TIME_TILE = 512


def _seg_mean(x):
    return _seg_sum(x) * (1.0 / HEAD_DIM)


def _wkv_seq_kernel(r_ref, lw_ref, k_ref, v_ref, kk_ref, b_ref, g_ref, rk_ref, lnw_ref, lnb_ref,
                    o_ref, so_ref, st_ref):
    i = pl.program_id(2)

    @pl.when(i == 0)
    def _():
        st_ref[...] = jnp.zeros_like(st_ref)

    r, k, v = r_ref[...], k_ref[...], v_ref[...]
    o, st = _wkv_tile(r, lw_ref[...], k, v, kk_ref[...], b_ref[...], st_ref[...])
    st_ref[...] = st
    mean = _seg_mean(o)
    var = _seg_mean(jnp.square(o - mean))
    on = (o - mean) * lax.rsqrt(var + RWKV_GN_EPS) * lnw_ref[...] + lnb_ref[...]
    bonus = _seg_sum(r * k * rk_ref[...]) * v
    o_ref[...] = (on + bonus) * g_ref[...]
    for h in range(HEADS_PER_TILE):
        so_ref[0, h] = st[h].T


def _wkv_seq(r, lw, k, v, kk, b, g, r_k, ln_w, ln_b, batch):
    m = r.shape[0]
    seq = m // batch
    tt = _row_tile(seq, WKV_TIME_TILE)
    assert tt % WKV_CHUNK == 0
    nt = seq // tt
    row_spec = pl.BlockSpec((tt, LANE), lambda bi, t, i: (bi * nt + i, t))
    par_spec = pl.BlockSpec((1, LANE), lambda bi, t, i: (0, t))
    return pl.pallas_call(
        _wkv_seq_kernel,
        grid=(batch, HEAD_TILES, nt),
        in_specs=[row_spec] * 7 + [par_spec] * 3,
        out_specs=[row_spec,
                   pl.BlockSpec((1, HEADS_PER_TILE, HEAD_DIM, HEAD_DIM), lambda bi, t, i: (bi, t, 0, 0))],
        out_shape=[jax.ShapeDtypeStruct((m, D_HALF), F32),
                   jax.ShapeDtypeStruct((batch, HEADS, HEAD_DIM, HEAD_DIM), F32)],
        scratch_shapes=[pltpu.VMEM((HEADS_PER_TILE, HEAD_DIM, HEAD_DIM), F32)],
        compiler_params=_cp("parallel", "parallel", "arbitrary"),
    )(r, lw, k, v, kk, b, g, r_k.reshape(1, -1), ln_w.reshape(1, -1), ln_b.reshape(1, -1))


def _wkv_finish(o, r, k, v, g, rk, lnw, lnb):
    mean = jnp.mean(o, axis=-1, keepdims=True)
    var = jnp.mean(jnp.square(o - mean), axis=-1, keepdims=True)
    on = (o - mean) * lax.rsqrt(var + RWKV_GN_EPS) * lnw + lnb
    bonus = jnp.sum(r * k * rk, axis=-1, keepdims=True) * v
    return (on + bonus) * g


def _wkv_step_kernel(s_ref, r_ref, lw_ref, k_ref, v_ref, kk_ref, b_ref, g_ref, rk_ref, lnw_ref, lnb_ref,
                     o_ref, so_ref):
    ri = lax.broadcasted_iota(jnp.int32, (HEAD_DIM, HEAD_DIM), 0)
    eye = jnp.where(ri == lax.broadcasted_iota(jnp.int32, (HEAD_DIM, HEAD_DIM), 1), 1.0, 0.0)
    outs = []
    for h in range(HEADS):
        sl = slice(h * HEAD_DIM, (h + 1) * HEAD_DIM)
        r, lw, k, v, kk, b, g = [ref[0][:, sl] for ref in (r_ref, lw_ref, k_ref, v_ref, kk_ref, b_ref, g_ref)]
        s = s_ref[0, h]
        sa = -jnp.sum(s * kk, axis=-1, keepdims=True)
        v_col = jnp.sum(eye * v, axis=-1, keepdims=True)
        s = s * jnp.exp(lw) + sa * b + v_col * k
        so_ref[0, h] = s
        o_col = jnp.sum(s * r, axis=-1, keepdims=True)
        o = jnp.sum(eye * o_col, axis=0, keepdims=True)
        outs.append(_wkv_finish(o, r, k, v, g, rk_ref[:, sl], lnw_ref[:, sl], lnb_ref[:, sl]))
    o_ref[0] = jnp.concatenate(outs, axis=-1)


def _wkv_step(state, r, lw, k, v, kk, b, g, r_k, ln_w, ln_b):
    bsz = r.shape[0]
    rows = [t.reshape(bsz, 1, D_HALF) for t in (r, lw, k, v, kk, b, g)]
    row_spec = pl.BlockSpec((1, 1, D_HALF), lambda i: (i, 0, 0))
    st_spec = pl.BlockSpec((1, HEADS, HEAD_DIM, HEAD_DIM), lambda i: (i, 0, 0, 0))
    out, st = pl.pallas_call(
        _wkv_step_kernel,
        grid=(bsz,),
        in_specs=[st_spec] + [row_spec] * 7 + [_full((1, D_HALF))] * 3,
        out_specs=[row_spec, st_spec],
        out_shape=[jax.ShapeDtypeStruct((bsz, 1, D_HALF), F32),
                   jax.ShapeDtypeStruct(state.shape, F32)],
        compiler_params=_cp("parallel"),
    )(state, *rows, r_k.reshape(1, -1), ln_w.reshape(1, -1), ln_b.reshape(1, -1))
    return out.reshape(bsz, D_HALF), st


CONV_HALO = 32


def _layernorm_silu(y, g, b):
    m = jnp.mean(y, axis=-1, keepdims=True)
    var = jnp.mean(jnp.square(y - m), axis=-1, keepdims=True)
    return _silu((y - m) * lax.rsqrt(var + LN_EPS) * g + b)


def _convd_seq_kernel(gv_ref, gg_ref, hv_ref, hg_ref, w_ref, b_ref, lng_ref, lnb_ref,
                      o_ref, tail_ref, ext_ref, *, seq_tiles):
    tt = gv_ref.shape[0]
    first = (pl.program_id(0) % seq_tiles) == 0
    ext_ref[0:CONV_HALO, :] = jnp.where(first, 0.0, hv_ref[...] * _sigmoid(hg_ref[...]))
    ext_ref[CONV_HALO:, :] = gv_ref[...] * _sigmoid(gg_ref[...])
    off = CONV_HALO - (CONV_D_WIDTH - 1)
    acc = jnp.zeros((tt, D_HALF), F32)
    for j in range(CONV_D_WIDTH):
        acc = acc + w_ref[j:j + 1, :] * ext_ref[off + j:off + j + tt, :]
    o_ref[...] = _layernorm_silu(acc + b_ref[...], lng_ref[...], lnb_ref[...])
    tail_ref[0] = ext_ref[tt:, :]


def _convd_seq(proj, batch, w, b, ln_g, ln_b):
    m = proj.shape[0]
    seq = m // batch
    tt = _row_tile(seq, 256)
    seq_tiles = seq // tt
    hb = tt // CONV_HALO
    vcol, gcol = 4, 5
    tile = lambda col: pl.BlockSpec((tt, D_HALF), lambda i: (i, col))
    halo = lambda col: pl.BlockSpec((CONV_HALO, D_HALF), lambda i: (jnp.maximum(i * hb - 1, 0), col))
    params = [w, b.reshape(1, -1), ln_g.reshape(1, -1), ln_b.reshape(1, -1)]
    out, tail = pl.pallas_call(
        functools.partial(_convd_seq_kernel, seq_tiles=seq_tiles),
        grid=(m // tt,),
        in_specs=[tile(vcol), tile(gcol), halo(vcol), halo(gcol)] + [_full(p.shape) for p in params],
        out_specs=[pl.BlockSpec((tt, D_HALF), lambda i: (i, 0)),
                   pl.BlockSpec((1, CONV_HALO, D_HALF), lambda i: (i, 0, 0))],
        out_shape=[jax.ShapeDtypeStruct((m, D_HALF), F32),
                   jax.ShapeDtypeStruct((m // tt, CONV_HALO, D_HALF), F32)],
        scratch_shapes=[pltpu.VMEM((tt + CONV_HALO, D_HALF), F32)],
        compiler_params=_cp("parallel"))(proj, proj, proj, proj, *params)
    return out, tail[seq_tiles - 1::seq_tiles, CONV_HALO - (CONV_D_WIDTH - 1):]


def _convd_step_kernel(gv_ref, gg_ref, buf_ref, w_ref, b_ref, lng_ref, lnb_ref, o_ref, u_ref):
    u = gv_ref[...] * _sigmoid(gg_ref[...])
    u_ref[...] = u
    acc = u * w_ref[CONV_D_WIDTH - 1:CONV_D_WIDTH, :]
    for j in range(CONV_D_WIDTH - 1):
        acc = acc + w_ref[j:j + 1, :] * buf_ref[j]
    o_ref[...] = _layernorm_silu(acc + b_ref[...], lng_ref[...], lnb_ref[...])


def _convd_step(proj, buf, w, b, ln_g, ln_b):
    m = proj.shape[0]
    buf_t = jnp.swapaxes(buf, 0, 1)
    params = [w, b.reshape(1, -1), ln_g.reshape(1, -1), ln_b.reshape(1, -1)]
    out, u = pl.pallas_call(
        _convd_step_kernel,
        grid=(1,),
        in_specs=[pl.BlockSpec((m, D_HALF), lambda i: (0, 4)), pl.BlockSpec((m, D_HALF), lambda i: (0, 5)),
                  _full(buf_t.shape)] + [_full(p.shape) for p in params],
        out_specs=[_full((m, D_HALF)), _full((m, D_HALF))],
        out_shape=[jax.ShapeDtypeStruct((m, D_HALF), F32)] * 2,
        compiler_params=_cp("arbitrary"))(proj, proj, buf_t, *params)
    return out, jnp.concatenate([buf[:, 1:], u[:, None]], axis=1)


def _rmsnorm_kernel(x_ref, g_ref, o_ref):
    x = x_ref[...]
    ms = jnp.mean(x * x, axis=-1, keepdims=True)
    o_ref[...] = x * lax.rsqrt(ms + RMS_EPS) * g_ref[...]


def _rmsnorm(x, g):
    m, d = x.shape
    tm = _row_tile(m, 512)
    return pl.pallas_call(
        _rmsnorm_kernel,
        grid=(m // tm,),
        in_specs=[pl.BlockSpec((tm, d), lambda i: (i, 0)), _full((1, d))],
        out_specs=pl.BlockSpec((tm, d), lambda i: (i, 0)),
        out_shape=jax.ShapeDtypeStruct((m, d), F32),
        compiler_params=_cp("parallel"))(x, g.reshape(1, d))


def _kv_layout_kernel(*refs, n_layers, paged):
    k_refs, v_refs = refs[:n_layers], refs[n_layers:2 * n_layers]
    ko_ref, vo_ref = refs[2 * n_layers:]
    layer = pl.program_id(0)

    def emit(src_ref, dst_ref):
        x = src_ref[...]
        if not paged:
            dst_ref[...] = x.reshape(dst_ref.shape)
            return
        for pg in range(dst_ref.shape[0]):
            rows = x[pg * PAGE_SIZE:(pg + 1) * PAGE_SIZE]
            for h in range(HEADS):
                dst_ref[pg, h] = rows[:, h * HEAD_DIM:(h + 1) * HEAD_DIM].T

    for l in range(n_layers):
        @pl.when(layer == l)
        def _():
            emit(k_refs[l], ko_ref)
            emit(v_refs[l], vo_ref)


def _kv_layout(projs, paged):
    n = len(projs)
    m = projs[0].shape[0]
    tm = _row_tile(m, 512)
    nt = m // tm

    def col_spec(l, col):
        return pl.BlockSpec((tm, D_HALF), lambda layer, i: (jnp.where(layer == l, i, 0), col))

    if paged:
        ppt = tm // PAGE_SIZE
        out_spec = pl.BlockSpec((ppt, HEADS, HEAD_DIM, PAGE_SIZE), lambda layer, i: (layer * nt + i, 0, 0, 0))
        shp = jax.ShapeDtypeStruct((n * m // PAGE_SIZE, HEADS, HEAD_DIM, PAGE_SIZE), F32)
    else:
        out_spec = pl.BlockSpec((tm, HEADS, HEAD_DIM), lambda layer, i: (layer * nt + i, 0, 0))
        shp = jax.ShapeDtypeStruct((n * m, HEADS, HEAD_DIM), F32)
    return pl.pallas_call(
        functools.partial(_kv_layout_kernel, n_layers=n, paged=paged),
        grid=(n, nt),
        in_specs=[col_spec(l, 2) for l in range(n)] + [col_spec(l, 3) for l in range(n)],
        out_specs=[out_spec, out_spec],
        out_shape=[shp, shp],
        compiler_params=_cp("parallel", "parallel"))(*projs, *projs)


def _rwkv_params(p, li):
    row = lambda t: t.reshape(1, -1)
    rp = dict(mu=p['rwkv_mu_rkv'][li], mz=p['rwkv_mu_z'][li], w0=row(p['rwkv_w0'][li]),
              w1=p['rwkv_w1'][li].astype(BF16), w2=p['rwkv_w2'][li].astype(BF16),
              a0=row(p['rwkv_a0'][li]), a1=p['rwkv_a1'][li].astype(BF16), a2=p['rwkv_a2'][li].astype(BF16),
              g1=p['rwkv_g1'][li].astype(BF16), g2=p['rwkv_g2'][li].astype(BF16),
              k_k=row(p['rwkv_k_k'][li]), k_a=row(p['rwkv_k_a'][li]))
    if li > 0:
        rp.update(mu_v=row(p['rwkv_mu_v'][li - 1]), v0=row(p['rwkv_v0'][li - 1]),
                  v1=p['rwkv_v1'][li - 1].astype(BF16), v2=p['rwkv_v2'][li - 1].astype(BF16))
    return rp


def _prepare(p):
    depth = p['g_mix'].shape[0]
    layers = []
    for i in range(depth):
        li = i // 2
        lay = dict(g_mix=p['g_mix'][i], g_ffn=p['g_ffn'][i],
                   w_up=p['ffn_w_up'][i].astype(BF16), w_down=p['ffn_w_down'][i].astype(BF16),
                   ffn_cw=p['ffn_conv_w'][i], ffn_cb=p['ffn_conv_b'][i])
        if i % 2 == 0:
            lay.update(w_in=p['w_in_ab'][li].astype(BF16), w_out=p['w_out_ab'][li].astype(BF16),
                       s5=_s5_params(p['s5_a_re'][li], p['s5_a_im'][li], p['s5_log_step'][li],
                                     p['s5_b_re'][li], p['s5_b_im'][li], p['s5_c_re'][li], p['s5_c_im'][li],
                                     p['s5_d'][li]),
                       w_glu=p['s5_w_glu'][li].astype(BF16), b_glu=p['s5_b_glu'][li])
        else:
            lay.update(w_in=p['w_in_cd'][li].astype(BF16), w_out=p['w_out_cd'][li].astype(BF16),
                       rwkv=_rwkv_params(p, li), r_k=p['rwkv_r_k'][li], ln_w=p['rwkv_ln_w'][li],
                       ln_b=p['rwkv_ln_b'][li], cd_w=p['conv_d_w'][li], cd_b=p['conv_d_b'][li],
                       cd_g=p['conv_d_ln_g'][li], cd_lb=p['conv_d_ln_b'][li])
        layers.append(lay)
    return layers


def _trunk_seq(x3, layers, g_final):
    batch, seq, d = x3.shape
    x = x3.reshape(batch * seq, d)
    rope = _rope_tables(seq, 0, True)
    attn_projs, s5r, s5i, wkvs, shifts, convs, ffns = [], [], [], [], [], [], []
    v_first = None
    for i, lay in enumerate(layers):
        if i % 2 == 0:
            proj = _norm_matmul(x, lay['g_mix'], lay['w_in'], rope=rope, rope_cols=(1, 2))
            a_out, hr, hi = _s5_seq(proj, batch, lay['s5'], lay['w_glu'], lay['b_glu'])
            b_out = _moba_seq(proj, batch)
            attn_projs.append(proj)
            s5r.append(hr)
            s5i.append(hi)
            x = _out_proj(a_out, b_out, lay['w_out'], x)
        else:
            proj = _norm_matmul(x, lay['g_mix'], lay['w_in'])
            r, lw, k, v, kk, b, g = _rwkv_pre(proj, None, lay['rwkv'], v_first, batch)
            if v_first is None:
                v_first = v
            c_out, s_fin = _wkv_seq(r, lw, k, v, kk, b, g, lay['r_k'], lay['ln_w'], lay['ln_b'], batch)
            d_out, cbuf = _convd_seq(proj, batch, lay['cd_w'], lay['cd_b'], lay['cd_g'], lay['cd_lb'])
            wkvs.append(s_fin)
            shifts.append(proj.reshape(batch, seq, -1)[:, -1, :4 * D_HALF])
            convs.append(cbuf)
            x = _out_proj(c_out, d_out, lay['w_out'], x)
        x, fbuf = _ffn_seq(x, batch, lay['g_ffn'], lay['w_up'], lay['ffn_cw'], lay['ffn_cb'], lay['w_down'])
        ffns.append(fbuf)
    y = _rmsnorm(x, g_final).reshape(batch, seq, d)
    assert seq % PAGE_SIZE == 0
    k_all, v_all = _kv_layout(attn_projs, paged=True)
    return (y, k_all, v_all, jnp.stack(s5r), jnp.stack(s5i), jnp.stack(wkvs),
            jnp.stack(shifts), jnp.stack(convs), jnp.stack(ffns))


def _trunk_step(x3, pos0, layers, g_final, cache_k, cache_v, page_table, s5_re0, s5_im0, wkv0, shift0,
                convd0, ffn0):
    bsz, _, d = x3.shape
    x = x3.reshape(bsz, d)
    rope = _rope_tables(bsz, pos0, False)
    cache_kt = jnp.transpose(cache_k, (0, 1, 3, 4, 2))
    cache_vt = jnp.transpose(cache_v, (0, 1, 3, 4, 2))
    attn_projs, s5r, s5i, wkvs, shifts, convs, ffns = [], [], [], [], [], [], []
    v_first = None
    for i, lay in enumerate(layers):
        li = i // 2
        if i % 2 == 0:
            proj = _norm_matmul(x, lay['g_mix'], lay['w_in'], rope=rope, rope_cols=(1, 2))
            a_out, hr, hi = _s5_step(proj, s5_re0[li].reshape(bsz, -1), s5_im0[li].reshape(bsz, -1),
                                     lay['s5'], lay['w_glu'], lay['b_glu'])
            b_out = _moba_step(proj, cache_kt, cache_vt, li, page_table)
            attn_projs.append(proj)
            s5r.append(hr)
            s5i.append(hi)
            x = _out_proj(a_out, b_out, lay['w_out'], x)
        else:
            proj = _norm_matmul(x, lay['g_mix'], lay['w_in'])
            r, lw, k, v, kk, b, g = _rwkv_pre(proj, shift0[li], lay['rwkv'], v_first, bsz)
            if v_first is None:
                v_first = v
            c_out, s_fin = _wkv_step(wkv0[li], r, lw, k, v, kk, b, g, lay['r_k'], lay['ln_w'], lay['ln_b'])
            d_out, cbuf = _convd_step(proj, convd0[li], lay['cd_w'], lay['cd_b'], lay['cd_g'], lay['cd_lb'])
            wkvs.append(s_fin)
            shifts.append(proj[:, :4 * D_HALF])
            convs.append(cbuf)
            x = _out_proj(c_out, d_out, lay['w_out'], x)
        x, fbuf = _ffn_step(x, ffn0[i], lay['g_ffn'], lay['w_up'], lay['ffn_cw'], lay['ffn_cb'], lay['w_down'])
        ffns.append(fbuf)
    y = _rmsnorm(x, g_final).reshape(bsz, 1, d)
    k_all, v_all = _kv_layout(attn_projs, paged=False)
    return (y, k_all, v_all, jnp.stack(s5r), jnp.stack(s5i), jnp.stack(wkvs),
            jnp.stack(shifts), jnp.stack(convs), jnp.stack(ffns))


def kernel(x_prompt, x_sample, cache_k_moba, cache_v_moba, page_table, state_s5_re, state_s5_im,
           state_rwkv_wkv, state_rwkv_shift, state_conv_d, state_ffn_conv, g_mix, g_ffn, g_final,
           w_in_ab, w_out_ab, s5_a_re, s5_a_im, s5_log_step, s5_b_re, s5_b_im, s5_c_re, s5_c_im, s5_d,
           s5_w_glu, s5_b_glu, w_in_cd, w_out_cd, rwkv_mu_rkv, rwkv_mu_z, rwkv_w0, rwkv_w1, rwkv_w2,
           rwkv_a0, rwkv_a1, rwkv_a2, rwkv_g1, rwkv_g2, rwkv_k_k, rwkv_k_a, rwkv_r_k, rwkv_ln_w, rwkv_ln_b,
           rwkv_mu_v, rwkv_v0, rwkv_v1, rwkv_v2, conv_d_w, conv_d_b, conv_d_ln_g, conv_d_ln_b,
           ffn_w_up, ffn_conv_w, ffn_conv_b, ffn_w_down):
    p = dict(g_mix=g_mix, g_ffn=g_ffn, w_in_ab=w_in_ab, w_out_ab=w_out_ab,
             s5_a_re=s5_a_re, s5_a_im=s5_a_im, s5_log_step=s5_log_step, s5_b_re=s5_b_re, s5_b_im=s5_b_im,
             s5_c_re=s5_c_re, s5_c_im=s5_c_im, s5_d=s5_d, s5_w_glu=s5_w_glu, s5_b_glu=s5_b_glu,
             w_in_cd=w_in_cd, w_out_cd=w_out_cd, rwkv_mu_rkv=rwkv_mu_rkv, rwkv_mu_z=rwkv_mu_z,
             rwkv_w0=rwkv_w0, rwkv_w1=rwkv_w1, rwkv_w2=rwkv_w2, rwkv_a0=rwkv_a0, rwkv_a1=rwkv_a1,
             rwkv_a2=rwkv_a2, rwkv_g1=rwkv_g1, rwkv_g2=rwkv_g2, rwkv_k_k=rwkv_k_k, rwkv_k_a=rwkv_k_a,
             rwkv_r_k=rwkv_r_k.reshape(rwkv_r_k.shape[0], -1), rwkv_ln_w=rwkv_ln_w, rwkv_ln_b=rwkv_ln_b,
             rwkv_mu_v=rwkv_mu_v, rwkv_v0=rwkv_v0, rwkv_v1=rwkv_v1, rwkv_v2=rwkv_v2,
             conv_d_w=conv_d_w, conv_d_b=conv_d_b, conv_d_ln_g=conv_d_ln_g, conv_d_ln_b=conv_d_ln_b,
             ffn_w_up=ffn_w_up, ffn_conv_w=ffn_conv_w, ffn_conv_b=ffn_conv_b, ffn_w_down=ffn_w_down)
    layers = _prepare(p)
    bp, seq, _ = x_prompt.shape
    bs = x_sample.shape[0]
    n_ab = w_in_ab.shape[0]
    n_cd = w_in_cd.shape[0]
    page = cache_k_moba.shape[2]
    past_len = page_table.shape[1] * page
    (y_p, k_p, v_p, s5r_p, s5i_p, wkv_p, sh_p, cd_p, ff_p) = _trunk_seq(x_prompt, layers, g_final)
    (y_s, k_s, v_s, s5r_s, s5i_s, wkv_s, sh_s, cd_s, ff_s) = _trunk_step(
        x_sample, past_len, layers, g_final, cache_k_moba, cache_v_moba, page_table, state_s5_re, state_s5_im,
        state_rwkv_wkv, state_rwkv_shift, state_conv_d, state_ffn_conv)
    kv_s = (n_ab, bs, 1, HEADS, HEAD_DIM)
    s5_p = (n_ab, bp, S5_GROUPS, S5_STATE)
    s5_s = (n_ab, bs, S5_GROUPS, S5_STATE)

    def pages_out(t):
        t = t.reshape(n_ab, bp, seq // page, HEADS, HEAD_DIM, page)
        return jnp.transpose(t, (0, 1, 2, 5, 3, 4))

    return (y_p, y_s, pages_out(k_p), pages_out(v_p), k_s.reshape(kv_s), v_s.reshape(kv_s),
            s5r_p.reshape(s5_p), s5i_p.reshape(s5_p), s5r_s.reshape(s5_s), s5i_s.reshape(s5_s),
            wkv_p, wkv_s, sh_p, sh_s, cd_p, cd_s, ff_p, ff_s)
```

```python
import functools
import math

import numpy as np
import jax
import jax.numpy as jnp
from jax import lax
from jax.experimental import pallas as pl
from jax.experimental.pallas import tpu as pltpu

F32 = jnp.float32
BF16 = jnp.bfloat16
HIGHEST = lax.Precision.HIGHEST

D_MODEL = 1024
D_HALF = 512
S5_GROUP = 16
S5_GROUPS = 32
S5_STATE = 64
S5_LANES = S5_GROUPS * S5_STATE
HEADS = 8
HEAD_DIM = 64
MOBA_BLOCK = 256
MOBA_TOPK = 3
PAGE_SIZE = 128
ROPE_DIM = 16
ROPE_THETA = 500000.0
RWKV_GN_EPS = 64e-5
CONV_D_WIDTH = 31
D_FF = 2816
RMS_EPS = 1e-6
LN_EPS = 1e-5
LANE = 128
SUBLANE = 8
VMEM_LIMIT = 48 * 1024 * 1024
NEG = -1e30


def _cp(*sem):
    return pltpu.CompilerParams(dimension_semantics=sem, vmem_limit_bytes=VMEM_LIMIT)


def _bdot(a, b):
    return jnp.dot(a.astype(BF16), b.astype(BF16), preferred_element_type=F32)


def _sigmoid(x):
    return 1.0 / (1.0 + jnp.exp(-x))


def _silu(x):
    return x * _sigmoid(x)


def _row_tile(m, pref):
    return pref if m % pref == 0 else m


def _rope_table_kernel(inv_ref, cos_ref, sa_ref, sb_ref, *, pos0, per_row_pos):
    rows = cos_ref.shape[0]
    lane = lax.broadcasted_iota(jnp.int32, (rows, LANE), 1) % HEAD_DIM
    if per_row_pos:
        pos = (pos0 + lax.broadcasted_iota(jnp.int32, (rows, LANE), 0)).astype(F32)
    else:
        pos = jnp.full((rows, LANE), pos0, F32)
    ang = pos * inv_ref[...]
    c, s = jnp.cos(ang), jnp.sin(ang)
    cos_ref[...] = jnp.where(lane < ROPE_DIM, c, 1.0)
    sa_ref[...] = jnp.where(lane < ROPE_DIM // 2, -s, 0.0)
    sb_ref[...] = jnp.where((lane >= ROPE_DIM // 2) & (lane < ROPE_DIM), s, 0.0)


def _rope_tables(rows, pos0, per_row_pos):
    half = ROPE_DIM // 2
    inv8 = (np.float32(1.0) / (np.float32(ROPE_THETA) ** (np.arange(half, dtype=np.float32) / np.float32(half))))
    inv = np.zeros((HEAD_DIM,), np.float32)
    inv[:half] = inv8
    inv[half:ROPE_DIM] = inv8
    inv = jnp.asarray(np.tile(inv, LANE // HEAD_DIM)[None, :])
    shp = jax.ShapeDtypeStruct((rows, LANE), F32)
    return pl.pallas_call(
        functools.partial(_rope_table_kernel, pos0=pos0, per_row_pos=per_row_pos),
        out_shape=(shp, shp, shp))(inv)


def _norm_matmul_kernel(x_ref, g_ref, w_ref, *rest, rope_cols):
    if rope_cols:
        cos_ref, sa_ref, sb_ref, o_ref = rest
    else:
        (o_ref,) = rest
    x = x_ref[...]
    ms = jnp.mean(x * x, axis=-1, keepdims=True)
    h = (x * lax.rsqrt(ms + RMS_EPS) * g_ref[...]).astype(BF16)
    tn = D_HALF
    for j in range(o_ref.shape[1] // tn):
        y = jnp.dot(h, w_ref[:, j * tn:(j + 1) * tn], preferred_element_type=F32)
        if j not in rope_cols:
            o_ref[:, j * tn:(j + 1) * tn] = y
            continue
        cos, sa, sb = cos_ref[...], sa_ref[...], sb_ref[...]
        for c in range(tn // LANE):
            yc = y[:, c * LANE:(c + 1) * LANE]
            o_ref[:, j * tn + c * LANE:j * tn + (c + 1) * LANE] = (
                yc * cos + pltpu.roll(yc, LANE - ROPE_DIM // 2, axis=1) * sa
                + pltpu.roll(yc, ROPE_DIM // 2, axis=1) * sb)


def _norm_matmul(x, g, w_bf, rope=None, rope_cols=()):
    m, d = x.shape
    n = w_bf.shape[1]
    tm = _row_tile(m, 512)
    in_specs = [pl.BlockSpec((tm, d), lambda i: (i, 0)),
                pl.BlockSpec((1, d), lambda i: (0, 0)),
                pl.BlockSpec((d, n), lambda i: (0, 0))]
    args = [x, g.reshape(1, d), w_bf]
    if rope_cols:
        nt = rope[0].shape[0] // tm
        for t in rope:
            in_specs.append(pl.BlockSpec((tm, LANE), lambda i: (i % nt, 0)))
            args.append(t)
    return pl.pallas_call(
        functools.partial(_norm_matmul_kernel, rope_cols=tuple(rope_cols)),
        grid=(m // tm,),
        in_specs=in_specs,
        out_specs=pl.BlockSpec((tm, n), lambda i: (i, 0)),
        out_shape=jax.ShapeDtypeStruct((m, n), F32),
        compiler_params=_cp("parallel"))(*args)


def _out_proj_kernel(a_ref, b_ref, wa_ref, wb_ref, res_ref, o_ref):
    o_ref[...] = res_ref[...] + (_bdot(a_ref[...], wa_ref[...]) + _bdot(b_ref[...], wb_ref[...]))


def _out_proj(a, b, w_bf, res):
    m, c = a.shape
    n = w_bf.shape[1]
    tm = _row_tile(m, 512)
    return pl.pallas_call(
        _out_proj_kernel,
        grid=(m // tm,),
        in_specs=[pl.BlockSpec((tm, c), lambda i: (i, 0)),
                  pl.BlockSpec((tm, c), lambda i: (i, 0)),
                  pl.BlockSpec((c, n), lambda i: (0, 0)),
                  pl.BlockSpec((c, n), lambda i: (1, 0)),
                  pl.BlockSpec((tm, n), lambda i: (i, 0))],
        out_specs=pl.BlockSpec((tm, n), lambda i: (i, 0)),
        out_shape=jax.ShapeDtypeStruct((m, n), F32),
        compiler_params=_cp("parallel"))(a, b, w_bf, w_bf, res)


FFN_TN = 256
FFN_TM = 1024
FFN_HALO = SUBLANE


def _ffn_seq_kernel(x_ref, xh_ref, g_ref, wa_ref, wb_ref, cwa_ref, cwb_ref, cba_ref, cbb_ref, wd_ref,
                    o_ref, ua_ref, ub_ref, h_ref, acc_ref, *, tiles_per_seq):
    i, c = pl.program_id(0), pl.program_id(1)
    tm = x_ref.shape[0]
    first = (i % tiles_per_seq) == 0

    @pl.when(c == 0)
    def _():
        def norm(x):
            ms = jnp.mean(x * x, axis=-1, keepdims=True)
            return (x * lax.rsqrt(ms + RMS_EPS) * g_ref[...]).astype(BF16)
        h_ref[0:FFN_HALO, :] = jnp.where(first, jnp.zeros((), BF16), norm(xh_ref[...]))
        h_ref[FFN_HALO:, :] = norm(x_ref[...])
        acc_ref[...] = jnp.zeros_like(acc_ref)

    h = h_ref[...]

    def branch(w_ref, cw_ref, cb_ref, tail_ref):
        u = jnp.dot(h, w_ref[...], preferred_element_type=F32)
        tail_ref[0] = u[tm + FFN_HALO - 2:, :]
        cw = cw_ref[...]
        y = (u * cw[2:3, :] + pltpu.roll(u, 1, axis=0) * cw[1:2, :]
             + pltpu.roll(u, 2, axis=0) * cw[0:1, :])
        return y[FFN_HALO:, :] + cb_ref[...]

    a = branch(wa_ref, cwa_ref, cba_ref, ua_ref)
    b = branch(wb_ref, cwb_ref, cbb_ref, ub_ref)
    acc_ref[...] += _bdot(_silu(a) * b, wd_ref[...])

    @pl.when(c == pl.num_programs(1) - 1)
    def _():
        o_ref[...] = x_ref[...] + acc_ref[...]


def _ffn_seq(x, batch, g, w_up_bf, cw, cb, w_down_bf):
    m, d = x.shape
    seq = m // batch
    tm = _row_tile(seq, FFN_TM)
    tps = seq // tm
    nc = D_FF // FFN_TN
    hb = tm // FFN_HALO
    out, ua, ub = pl.pallas_call(
        functools.partial(_ffn_seq_kernel, tiles_per_seq=tps),
        grid=(m // tm, nc),
        in_specs=[pl.BlockSpec((tm, d), lambda i, c: (i, 0)),
                  pl.BlockSpec((FFN_HALO, d), lambda i, c: (jnp.maximum(i * hb - 1, 0), 0)),
                  pl.BlockSpec((1, d), lambda i, c: (0, 0)),
                  pl.BlockSpec((d, FFN_TN), lambda i, c: (0, c)),
                  pl.BlockSpec((d, FFN_TN), lambda i, c: (0, nc + c)),
                  pl.BlockSpec((3, FFN_TN), lambda i, c: (0, c)),
                  pl.BlockSpec((3, FFN_TN), lambda i, c: (0, nc + c)),
                  pl.BlockSpec((1, FFN_TN), lambda i, c: (0, c)),
                  pl.BlockSpec((1, FFN_TN), lambda i, c: (0, nc + c)),
                  pl.BlockSpec((FFN_TN, d), lambda i, c: (c, 0))],
        out_specs=[pl.BlockSpec((tm, d), lambda i, c: (i, 0)),
                   pl.BlockSpec((1, 2, FFN_TN), lambda i, c: (i, 0, c)),
                   pl.BlockSpec((1, 2, FFN_TN), lambda i, c: (i, 0, c))],
        out_shape=[jax.ShapeDtypeStruct((m, d), F32),
                   jax.ShapeDtypeStruct((m // tm, 2, D_FF), F32),
                   jax.ShapeDtypeStruct((m // tm, 2, D_FF), F32)],
        scratch_shapes=[pltpu.VMEM((tm + FFN_HALO, d), BF16), pltpu.VMEM((tm, d), F32)],
        compiler_params=_cp("arbitrary", "arbitrary"),
    )(x, x, g.reshape(1, d), w_up_bf, w_up_bf, cw, cw, cb.reshape(1, -1), cb.reshape(1, -1), w_down_bf)
    return out, jnp.concatenate([ua[tps - 1::tps], ub[tps - 1::tps]], axis=-1)


def _ffn_step_kernel(x_ref, g_ref, wa_ref, wb_ref, cwa_ref, cwb_ref, cba_ref, cbb_ref, wd_ref,
                     bufa_ref, bufb_ref, o_ref, ua_ref, ub_ref, h_ref, acc_ref):
    c = pl.program_id(0)

    @pl.when(c == 0)
    def _():
        x = x_ref[...]
        ms = jnp.mean(x * x, axis=-1, keepdims=True)
        h_ref[...] = (x * lax.rsqrt(ms + RMS_EPS) * g_ref[...]).astype(BF16)
        acc_ref[...] = jnp.zeros_like(acc_ref)

    h = h_ref[...]

    def branch(w_ref, cw_ref, cb_ref, buf_ref, u_ref):
        u = jnp.dot(h, w_ref[...], preferred_element_type=F32)
        u_ref[...] = u
        cw = cw_ref[...]
        return u * cw[2:3, :] + buf_ref[1] * cw[1:2, :] + buf_ref[0] * cw[0:1, :] + cb_ref[...]

    a = branch(wa_ref, cwa_ref, cba_ref, bufa_ref, ua_ref)
    b = branch(wb_ref, cwb_ref, cbb_ref, bufb_ref, ub_ref)
    acc_ref[...] += _bdot(_silu(a) * b, wd_ref[...])

    @pl.when(c == pl.num_programs(0) - 1)
    def _():
        o_ref[...] = x_ref[...] + acc_ref[...]


def _ffn_step(x, buf, g, w_up_bf, cw, cb, w_down_bf):
    m, d = x.shape
    nc = D_FF // FFN_TN
    buf_t = jnp.swapaxes(buf, 0, 1)
    out, ua, ub = pl.pallas_call(
        _ffn_step_kernel,
        grid=(nc,),
        in_specs=[pl.BlockSpec((m, d), lambda c: (0, 0)),
                  pl.BlockSpec((1, d), lambda c: (0, 0)),
                  pl.BlockSpec((d, FFN_TN), lambda c: (0, c)),
                  pl.BlockSpec((d, FFN_TN), lambda c: (0, nc + c)),
                  pl.BlockSpec((3, FFN_TN), lambda c: (0, c)),
                  pl.BlockSpec((3, FFN_TN), lambda c: (0, nc + c)),
                  pl.BlockSpec((1, FFN_TN), lambda c: (0, c)),
                  pl.BlockSpec((1, FFN_TN), lambda c: (0, nc + c)),
                  pl.BlockSpec((FFN_TN, d), lambda c: (c, 0)),
                  pl.BlockSpec((2, m, FFN_TN), lambda c: (0, 0, c)),
                  pl.BlockSpec((2, m, FFN_TN), lambda c: (0, 0, nc + c))],
        out_specs=[pl.BlockSpec((m, d), lambda c: (0, 0)),
                   pl.BlockSpec((m, FFN_TN), lambda c: (0, c)),
                   pl.BlockSpec((m, FFN_TN), lambda c: (0, c))],
        out_shape=[jax.ShapeDtypeStruct((m, d), F32),
                   jax.ShapeDtypeStruct((m, D_FF), F32),
                   jax.ShapeDtypeStruct((m, D_FF), F32)],
        scratch_shapes=[pltpu.VMEM((m, d), BF16), pltpu.VMEM((m, d), F32)],
        compiler_params=_cp("arbitrary"),
    )(x, g.reshape(1, d), w_up_bf, w_up_bf, cw, cw, cb.reshape(1, -1), cb.reshape(1, -1), w_down_bf,
      buf_t, buf_t)
    u = jnp.concatenate([ua, ub], axis=-1)
    return out, jnp.stack([buf[:, 1], u], axis=1)


S5_STRIP = 512
S5_NSTRIP = S5_LANES // S5_STRIP
S5_TIME_TILE = 512


def _s5_param_kernel(are_ref, aim_ref, ls_ref, brt_ref, bit_ref, abr_ref, abi_ref, bbr_ref, bbi_ref):
    ar, ai = are_ref[...], aim_ref[...]
    dt = jnp.exp(ls_ref[...])
    mag = jnp.exp(ar * dt)
    abr, abi = mag * jnp.cos(ai * dt), mag * jnp.sin(ai * dt)
    inv_abs2 = 1.0 / (ar * ar + ai * ai)
    cr = ((abr - 1.0) * ar + abi * ai) * inv_abs2
    ci = (abi * ar - (abr - 1.0) * ai) * inv_abs2
    br, bi = brt_ref[...], bit_ref[...]
    abr_ref[...] = abr
    abi_ref[...] = abi
    bbr_ref[...] = cr * br - ci * bi
    bbi_ref[...] = cr * bi + ci * br


def _s5_params(a_re, a_im, log_step, b_re, b_im, c_re, c_im, d_skip):
    g, n, k = b_re.shape
    rep = lambda t: jnp.repeat(t, k, axis=0)
    brt = b_re.transpose(0, 2, 1).reshape(g * k, n)
    bit = b_im.transpose(0, 2, 1).reshape(g * k, n)
    shp = jax.ShapeDtypeStruct((g * k, n), F32)
    abr, abi, bbr, bbi = pl.pallas_call(_s5_param_kernel, out_shape=(shp, shp, shp, shp))(
        rep(a_re), rep(a_im), rep(log_step[:, None]), brt, bit)
    abr = abr[::k].reshape(1, g * n)
    abi = abi[::k].reshape(1, g * n)
    gs = S5_STRIP // n
    eye = jnp.eye(gs, dtype=F32)

    def in_blocks(t):
        t = t.reshape(S5_NSTRIP, gs, k, n)
        return jnp.einsum('cgkn,gh->cgkhn', t, eye).reshape(S5_NSTRIP, gs * k, gs * n).astype(BF16)

    def out_blocks(t):
        t = t.reshape(S5_NSTRIP, gs, k, n)
        return jnp.einsum('cgkn,gh->cgnhk', t, eye).reshape(S5_NSTRIP, gs * n, gs * k).astype(BF16)

    return dict(abr=abr, abi=abi, wbr=in_blocks(bbr), wbi=in_blocks(bbi),
                wcr=out_blocks(c_re), wci=out_blocks(c_im), d=d_skip.reshape(1, g * k))


def _s5_input(u, wbr_ref, wbi_ref, c):
    cw = S5_STRIP // (S5_STATE // S5_GROUP)
    uc = u[:, c * cw:(c + 1) * cw].astype(BF16)
    return (jnp.dot(uc, wbr_ref[c], preferred_element_type=F32),
            jnp.dot(uc, wbi_ref[c], preferred_element_type=F32))


def _s5_output(u, h_strip, wcr_ref, wci_ref, d_ref, wg_ref, bg_ref):
    ys = []
    for c in range(S5_NSTRIP):
        hr, hi = h_strip(c)
        ys.append(jnp.dot(hr.astype(BF16), wcr_ref[c], preferred_element_type=F32)
                  - jnp.dot(hi.astype(BF16), wci_ref[c], preferred_element_type=F32))
    y = jnp.concatenate(ys, axis=-1) + d_ref[...] * u
    yg = 0.5 * y * (1.0 + jnp.tanh(math.sqrt(2.0 / math.pi) * (y + 0.044715 * (y * y * y))))
    z = jnp.dot(yg.astype(BF16), wg_ref[...], preferred_element_type=F32) + bg_ref[...]
    return yg * _sigmoid(z)


def _cmul(ar, ai, br, bi):
    return ar * br - ai * bi, ar * bi + ai * br


def _s5_seq_kernel(u_ref, abr_ref, abi_ref, wbr_ref, wbi_ref, wcr_ref, wci_ref, d_ref, wg_ref, bg_ref,
                   o_ref, hro_ref, hio_ref, hr_ref, hi_ref, cr_ref, ci_ref, tab_ref):
    i = pl.program_id(1)
    tt = u_ref.shape[0]

    @pl.when(i == 0)
    def _():
        cr_ref[...] = jnp.zeros_like(cr_ref)
        ci_ref[...] = jnp.zeros_like(ci_ref)
        row = lax.broadcasted_iota(jnp.int32, (SUBLANE, S5_LANES), 0)
        p_r, p_i = [abr_ref[...]], [abi_ref[...]]
        for _ in range(SUBLANE - 1):
            nr, ni = _cmul(p_r[-1], p_i[-1], p_r[0], p_i[0])
            p_r.append(nr)
            p_i.append(ni)
        for k, s in enumerate((1, 2, 4)):
            tab_ref[2 * k] = jnp.where(row >= s, p_r[s - 1], 0.0)
            tab_ref[2 * k + 1] = jnp.where(row >= s, p_i[s - 1], 0.0)
        car_r = jnp.zeros((SUBLANE, S5_LANES), F32)
        car_i = jnp.zeros((SUBLANE, S5_LANES), F32)
        for j in range(SUBLANE):
            car_r = jnp.where(row == j, p_r[j], car_r)
            car_i = jnp.where(row == j, p_i[j], car_i)
        tab_ref[6] = car_r
        tab_ref[7] = car_i

    u = u_ref[...]
    for c in range(S5_NSTRIP):
        br, bi = _s5_input(u, wbr_ref, wbi_ref, c)
        hr_ref[:, c * S5_STRIP:(c + 1) * S5_STRIP] = br
        hi_ref[:, c * S5_STRIP:(c + 1) * S5_STRIP] = bi

    def block(r, carry):
        r8 = pl.multiple_of(r * SUBLANE, SUBLANE)
        for c in range(S5_NSTRIP):
            sl = slice(c * S5_STRIP, (c + 1) * S5_STRIP)
            xr, xi = hr_ref[pl.ds(r8, SUBLANE), sl], hi_ref[pl.ds(r8, SUBLANE), sl]
            for k, s in enumerate((1, 2, 4)):
                dr, di = _cmul(tab_ref[2 * k, :, sl], tab_ref[2 * k + 1, :, sl],
                               pltpu.roll(xr, s, axis=0), pltpu.roll(xi, s, axis=0))
                xr, xi = xr + dr, xi + di
            dr, di = _cmul(tab_ref[6, :, sl], tab_ref[7, :, sl], cr_ref[:, sl], ci_ref[:, sl])
            xr, xi = xr + dr, xi + di
            hr_ref[pl.ds(r8, SUBLANE), sl] = xr
            hi_ref[pl.ds(r8, SUBLANE), sl] = xi
            cr_ref[:, sl] = xr[SUBLANE - 1:, :]
            ci_ref[:, sl] = xi[SUBLANE - 1:, :]
        return carry

    lax.fori_loop(0, tt // SUBLANE, block, 0)

    def h_strip(c):
        sl = slice(c * S5_STRIP, (c + 1) * S5_STRIP)
        return hr_ref[:, sl], hi_ref[:, sl]

    o_ref[...] = _s5_output(u, h_strip, wcr_ref, wci_ref, d_ref, wg_ref, bg_ref)
    hro_ref[0] = cr_ref[...]
    hio_ref[0] = ci_ref[...]


def _full(shape):
    nd = len(shape)
    return pl.BlockSpec(shape, lambda *_: (0,) * nd)


def _s5_seq(proj, batch, sp, w_glu_bf, b_glu):
    m = proj.shape[0]
    seq = m // batch
    tt = _row_tile(seq, S5_TIME_TILE)
    nt = seq // tt
    params = [sp['abr'], sp['abi'], sp['wbr'], sp['wbi'], sp['wcr'], sp['wci'], sp['d'],
              w_glu_bf, b_glu.reshape(1, -1)]
    out, hr, hi = pl.pallas_call(
        _s5_seq_kernel,
        grid=(batch, nt),
        in_specs=[pl.BlockSpec((tt, D_HALF), lambda b, i: (b * nt + i, 0))] + [_full(p.shape) for p in params],
        out_specs=[pl.BlockSpec((tt, D_HALF), lambda b, i: (b * nt + i, 0)),
                   pl.BlockSpec((1, 1, S5_LANES), lambda b, i: (b, 0, 0)),
                   pl.BlockSpec((1, 1, S5_LANES), lambda b, i: (b, 0, 0))],
        out_shape=[jax.ShapeDtypeStruct((m, D_HALF), F32),
                   jax.ShapeDtypeStruct((batch, 1, S5_LANES), F32),
                   jax.ShapeDtypeStruct((batch, 1, S5_LANES), F32)],
        scratch_shapes=[pltpu.VMEM((tt, S5_LANES), F32), pltpu.VMEM((tt, S5_LANES), F32),
                        pltpu.VMEM((1, S5_LANES), F32), pltpu.VMEM((1, S5_LANES), F32),
                        pltpu.VMEM((8, SUBLANE, S5_LANES), F32)],
        compiler_params=_cp("arbitrary", "arbitrary"))(proj, *params)
    return out, hr[:, 0], hi[:, 0]


def _s5_step_kernel(u_ref, h0r_ref, h0i_ref, abr_ref, abi_ref, wbr_ref, wbi_ref, wcr_ref, wci_ref, d_ref,
                    wg_ref, bg_ref, o_ref, hro_ref, hio_ref):
    u = u_ref[...]
    for c in range(S5_NSTRIP):
        sl = slice(c * S5_STRIP, (c + 1) * S5_STRIP)
        br, bi = _s5_input(u, wbr_ref, wbi_ref, c)
        dr, di = _cmul(abr_ref[:, sl], abi_ref[:, sl], h0r_ref[:, sl], h0i_ref[:, sl])
        hro_ref[:, sl] = dr + br
        hio_ref[:, sl] = di + bi

    def h_strip(c):
        sl = slice(c * S5_STRIP, (c + 1) * S5_STRIP)
        return hro_ref[:, sl], hio_ref[:, sl]

    o_ref[...] = _s5_output(u, h_strip, wcr_ref, wci_ref, d_ref, wg_ref, bg_ref)


def _s5_step(proj, h0r, h0i, sp, w_glu_bf, b_glu):
    m = proj.shape[0]
    args = [h0r, h0i, sp['abr'], sp['abi'], sp['wbr'], sp['wbi'], sp['wcr'], sp['wci'], sp['d'],
            w_glu_bf, b_glu.reshape(1, -1)]
    return pl.pallas_call(
        _s5_step_kernel,
        grid=(1,),
        in_specs=[pl.BlockSpec((m, D_HALF), lambda i: (0, 0))] + [_full(a.shape) for a in args],
        out_specs=[_full((m, D_HALF)), _full((m, S5_LANES)), _full((m, S5_LANES))],
        out_shape=[jax.ShapeDtypeStruct((m, D_HALF), F32),
                   jax.ShapeDtypeStruct((m, S5_LANES), F32),
                   jax.ShapeDtypeStruct((m, S5_LANES), F32)],
        compiler_params=_cp("arbitrary"))(proj, *args)


HEADS_PER_TILE = LANE // HEAD_DIM
HEAD_TILES = HEADS // HEADS_PER_TILE
ATTN_SCALE = HEAD_DIM ** -0.5
NT_DIMS = (((1,), (1,)), ((), ()))
MOBA_GROUP = 4


def _moba_seq_kernel(q_ref, k_ref, v_ref, o_ref, kbf_ref, vt_ref, kmean_ref, sel_ref, *, nb):
    qi = pl.program_id(2)
    blk = MOBA_BLOCK

    @pl.when(qi == 0)
    def _():
        kbf_ref[...] = k_ref[...].astype(BF16)
        vt_ref[...] = v_ref[...].T.astype(BF16)
        for j in range(nb):
            kmean_ref[j:j + 1, :] = jnp.mean(k_ref[j * blk:(j + 1) * blk, :], axis=0, keepdims=True)

    q = q_ref[...]
    row = lax.broadcasted_iota(jnp.int32, (nb, blk), 0)
    causal = (lax.broadcasted_iota(jnp.int32, (blk, blk), 0) <= lax.broadcasted_iota(jnp.int32, (blk, blk), 1))
    q0 = pl.multiple_of(qi * blk, blk)
    head_lanes = [slice(h * HEAD_DIM, (h + 1) * HEAD_DIM) for h in range(HEADS_PER_TILE)]
    qts, state = [], []
    for h, sl in enumerate(head_lanes):
        qh = q[:, sl]
        s = lax.dot_general(kmean_ref[:, sl], qh, NT_DIMS, precision=HIGHEST, preferred_element_type=F32)
        rank = jnp.zeros((nb, blk), F32)
        for i in range(nb):
            si = s[i:i + 1, :]
            beats = jnp.logical_or(si > s, jnp.logical_and(si == s, i < row))
            rank = rank + jnp.where(jnp.logical_and(beats, i < qi), 1.0, 0.0)
        sel_ref[h] = jnp.where(jnp.logical_and(row < qi, rank < MOBA_TOPK), 1.0, 0.0)

        qt = (qh * ATTN_SCALE).T.astype(BF16)
        lg = jnp.dot(kbf_ref[pl.ds(q0, blk), sl], qt, preferred_element_type=F32)
        lg = jnp.where(causal, lg, NEG)
        m = jnp.max(lg, axis=0, keepdims=True)
        p = jnp.exp(lg - m)
        l = jnp.sum(p, axis=0, keepdims=True)
        acc = jnp.dot(vt_ref[sl, pl.ds(q0, blk)], p.astype(BF16), preferred_element_type=F32)
        qts.append(qt)
        state += [m, l, acc]

    grp = MOBA_GROUP
    zero = jnp.zeros_like(qts[0])
    qt_both = jnp.concatenate([jnp.concatenate([qts[0], zero], axis=1),
                               jnp.concatenate([zero, qts[1]], axis=1)], axis=0)

    def body(jj, carry):
        j0 = pl.multiple_of(jj * (grp * blk), grp * blk)
        lg_both = jnp.dot(kbf_ref[pl.ds(j0, grp * blk), :], qt_both, preferred_element_type=F32)
        out = []
        for h, sl in enumerate(head_lanes):
            m, l, acc = carry[3 * h:3 * h + 3]
            lg = lg_both[:, h * blk:(h + 1) * blk]
            lg = jnp.concatenate(
                [jnp.where(sel_ref[h, pl.ds(jj * grp + g, 1), :] > 0.0, lg[g * blk:(g + 1) * blk], NEG)
                 for g in range(grp)], axis=0)
            m_new = jnp.maximum(m, jnp.max(lg, axis=0, keepdims=True))
            alpha = jnp.exp(m - m_new)
            p = jnp.exp(lg - m_new)
            l = alpha * l + jnp.sum(p, axis=0, keepdims=True)
            acc = alpha * acc + jnp.dot(vt_ref[sl, pl.ds(j0, grp * blk)], p.astype(BF16),
                                        preferred_element_type=F32)
            out += [m_new, l, acc]
        return tuple(out)

    state = lax.fori_loop(0, (qi + grp - 1) // grp, body, tuple(state))
    for h, sl in enumerate(head_lanes):
        m, l, acc = state[3 * h:3 * h + 3]
        o_ref[:, sl] = (acc / l).T


def _moba_seq(proj, batch):
    m = proj.shape[0]
    seq = m // batch
    assert seq % (MOBA_BLOCK * MOBA_GROUP) == 0
    nb = seq // MOBA_BLOCK
    qoff, koff, voff = HEAD_TILES, 2 * HEAD_TILES, 3 * HEAD_TILES
    return pl.pallas_call(
        functools.partial(_moba_seq_kernel, nb=nb),
        grid=(batch, HEAD_TILES, nb),
        in_specs=[pl.BlockSpec((MOBA_BLOCK, LANE), lambda b, t, i: (b * nb + i, qoff + t)),
                  pl.BlockSpec((seq, LANE), lambda b, t, i: (b, koff + t)),
                  pl.BlockSpec((seq, LANE), lambda b, t, i: (b, voff + t))],
        out_specs=pl.BlockSpec((MOBA_BLOCK, LANE), lambda b, t, i: (b * nb + i, t)),
        out_shape=jax.ShapeDtypeStruct((m, D_HALF), F32),
        scratch_shapes=[pltpu.VMEM((seq, LANE), BF16), pltpu.VMEM((LANE, seq), BF16),
                        pltpu.VMEM((nb, LANE), F32), pltpu.VMEM((HEADS_PER_TILE, nb, MOBA_BLOCK), F32)],
        compiler_params=_cp("parallel", "parallel", "arbitrary"))(proj, proj, proj)


MOBA_PAGES_PER_STEP = 32
PAGES_PER_BLOCK = MOBA_BLOCK // PAGE_SIZE


def _moba_rank_kernel(pt_ref, q_ref, *rest, nblk):
    k_refs = rest[:MOBA_PAGES_PER_STEP]
    idx_ref, s_ref = rest[MOBA_PAGES_PER_STEP:]
    j = pl.program_id(1)
    bps = MOBA_PAGES_PER_STEP // PAGES_PER_BLOCK
    lane = lax.broadcasted_iota(jnp.int32, (HEADS, LANE), 1)

    @pl.when(j == 0)
    def _():
        s_ref[...] = jnp.zeros_like(s_ref)

    q = q_ref[...]
    acc = s_ref[...]
    for t in range(bps):
        per_token = sum(jnp.sum(k_refs[t * PAGES_PER_BLOCK + p][...] * q, axis=1)
                        for p in range(PAGES_PER_BLOCK))
        score = jnp.sum(per_token, axis=-1, keepdims=True) * (1.0 / MOBA_BLOCK)
        acc = acc + jnp.where(lane == j * bps + t, score, 0.0)
    s_ref[...] = acc

    @pl.when(j == pl.num_programs(1) - 1)
    def _():
        s = acc
        rank = jnp.zeros((HEADS, LANE), jnp.int32)
        for i in range(nblk):
            si = s[:, i:i + 1]
            beats = jnp.logical_or(si > s, jnp.logical_and(si == s, i < lane))
            rank = rank + beats.astype(jnp.int32)
        out = jnp.zeros((HEADS, LANE), jnp.int32)
        for slot in range(MOBA_TOPK):
            hit = jnp.logical_and(rank == slot, lane < nblk)
            idx = jnp.sum(jnp.where(hit, lane, 0), axis=-1, keepdims=True)
            out = jnp.where(lane == slot, idx, out)
        idx_ref[...] = out


def _moba_pick_kernel(pg_ref, q_ref, kn_ref, vn_ref, *rest):
    n = MOBA_TOPK * PAGES_PER_BLOCK
    k_refs, v_refs, o_ref = rest[:n], rest[n:2 * n], rest[2 * n]
    q = q_ref[...] * ATTN_SCALE
    lg_self = jnp.sum(q * kn_ref[...], axis=0, keepdims=True)
    lgs = [jnp.sum(k_ref[...] * q, axis=0, keepdims=True) for k_ref in k_refs]
    m = lg_self
    for lg in lgs:
        m = jnp.maximum(m, jnp.max(lg, axis=-1, keepdims=True))
    p_self = jnp.exp(lg_self - m)
    l = p_self
    acc = p_self * vn_ref[...]
    for lg, v_ref in zip(lgs, v_refs):
        p = jnp.exp(lg - m)
        l = l + jnp.sum(p, axis=-1, keepdims=True)
        acc = acc + jnp.sum(v_ref[...] * p, axis=-1, keepdims=True)
    o_ref[...] = acc / l


def _moba_step(proj, cache_kt, cache_vt, layer, page_table):
    bsz = proj.shape[0]
    n_pages = page_table.shape[1]
    assert n_pages % MOBA_PAGES_PER_STEP == 0 and PAGE_SIZE == LANE
    nblk = n_pages // PAGES_PER_BLOCK
    assert nblk <= LANE
    steps = n_pages // MOBA_PAGES_PER_STEP
    cols = proj.reshape(bsz, 4, HEADS, HEAD_DIM, 1)
    q_col, k_col, v_col = cols[:, 1], cols[:, 2], cols[:, 3]

    def page_spec(t):
        return pl.BlockSpec((None, None, HEADS, HEAD_DIM, PAGE_SIZE),
                            lambda b, j, pt: (layer, pt[b, j * MOBA_PAGES_PER_STEP + t], 0, 0, 0))

    idx = pl.pallas_call(
        functools.partial(_moba_rank_kernel, nblk=nblk),
        grid_spec=pltpu.PrefetchScalarGridSpec(
            num_scalar_prefetch=1,
            grid=(bsz, steps),
            in_specs=[pl.BlockSpec((None, HEADS, HEAD_DIM, 1), lambda b, j, pt: (b, 0, 0, 0))]
            + [page_spec(t) for t in range(MOBA_PAGES_PER_STEP)],
            out_specs=pl.BlockSpec((None, HEADS, LANE), lambda b, j, pt: (b, 0, 0)),
            scratch_shapes=[pltpu.VMEM((HEADS, LANE), F32)]),
        out_shape=jax.ShapeDtypeStruct((bsz, HEADS, LANE), jnp.int32),
        compiler_params=_cp("arbitrary", "arbitrary"),
    )(page_table, q_col, *([cache_kt] * MOBA_PAGES_PER_STEP))

    top = idx[:, :, :MOBA_TOPK]
    logical = top[..., None] * PAGES_PER_BLOCK + jnp.arange(PAGES_PER_BLOCK, dtype=jnp.int32)
    pages = jnp.take_along_axis(page_table, logical.reshape(bsz, -1), axis=1).reshape(-1)
    per_head = MOBA_TOPK * PAGES_PER_BLOCK

    def pick_spec(t):
        return pl.BlockSpec((None, None, None, HEAD_DIM, PAGE_SIZE),
                            lambda b, h, pg: (layer, pg[(b * HEADS + h) * per_head + t], h, 0, 0))

    tok_spec = pl.BlockSpec((None, None, HEAD_DIM, 1), lambda b, h, pg: (b, h, 0, 0))
    out = pl.pallas_call(
        _moba_pick_kernel,
        grid_spec=pltpu.PrefetchScalarGridSpec(
            num_scalar_prefetch=1,
            grid=(bsz, HEADS),
            in_specs=[tok_spec] * 3 + [pick_spec(t) for t in range(per_head)] * 2,
            out_specs=tok_spec),
        out_shape=jax.ShapeDtypeStruct((bsz, HEADS, HEAD_DIM, 1), F32),
        compiler_params=_cp("arbitrary", "arbitrary"),
    )(pages, q_col, k_col, v_col, *([cache_kt] * per_head), *([cache_vt] * per_head))
    return out.reshape(bsz, D_HALF)


def _seg_sum(x):
    seg = lax.broadcasted_iota(jnp.int32, (LANE, LANE), 0) // HEAD_DIM
    ones = jnp.where(seg == lax.broadcasted_iota(jnp.int32, (LANE, LANE), 1) // HEAD_DIM, 1.0, 0.0).astype(BF16)
    hi = x.astype(BF16)
    r1 = x - hi.astype(F32)
    mid = r1.astype(BF16)
    lo = (r1 - mid.astype(F32)).astype(BF16)
    parts = []
    for c in range(x.shape[1] // LANE):
        sl = slice(c * LANE, (c + 1) * LANE)
        d = lambda p: jnp.dot(p[:, sl], ones, preferred_element_type=F32)
        parts.append(d(hi) + (d(mid) + d(lo)))
    return jnp.concatenate(parts, axis=-1)


def _softplus(x):
    return jnp.maximum(x, 0.0) + jnp.log(1.0 + jnp.exp(-jnp.abs(x)))


def _rwkv_mix(cat, prev, p, vfirst):
    c = D_HALF
    dlt = prev - cat
    part = lambda t, j: t[:, j * c:(j + 1) * c]
    mu, mz = p['mu'][...], p['mz'][...]
    r = part(cat, 0) + part(dlt, 0) * mu[0:1]
    k = part(cat, 1) + part(dlt, 1) * mu[1:2]
    v = part(cat, 2) + part(dlt, 2) * mu[2:3]
    z0, dz = part(cat, 3), part(dlt, 3)
    zw, za, zg = z0 + dz * mz[0:1], z0 + dz * mz[1:2], z0 + dz * mz[2:3]
    w_raw = -_softplus(-(p['w0'][...] + _bdot(jnp.tanh(_bdot(zw, p['w1'][...])), p['w2'][...]))) - 0.5
    log_decay = -jnp.exp(w_raw)
    a = _sigmoid(p['a0'][...] + _bdot(_bdot(za, p['a1'][...]), p['a2'][...]))
    g = _bdot(_sigmoid(_bdot(zg, p['g1'][...])), p['g2'][...])
    if vfirst is not None:
        zv = z0 + dz * p['mu_v'][...]
        vmix = _sigmoid(p['v0'][...] + _bdot(_bdot(zv, p['v1'][...]), p['v2'][...]))
        v = v + (vfirst - v) * vmix
    kk = k * p['k_k'][...]
    kk = kk / jnp.maximum(jnp.sqrt(_seg_sum(kk * kk)), 1e-12)
    k = k * (1.0 + (a - 1.0) * p['k_a'][...])
    return r, log_decay, k, v, kk, kk * a, g


RWKV_PARAM_NAMES = ('mu', 'mz', 'w0', 'w1', 'w2', 'a0', 'a1', 'a2', 'g1', 'g2', 'k_k', 'k_a')
RWKV_VRES_NAMES = ('mu_v', 'v0', 'v1', 'v2')
RWKV_N_OUT = 7


def _rwkv_pre_kernel(*refs, names, has_vfirst, seq_tiles):
    refs = list(refs)
    cat_ref = refs.pop(0)
    prev_ref = refs.pop(0)
    vf_ref = refs.pop(0) if has_vfirst else None
    p = {n: refs.pop(0) for n in names}
    outs = refs
    cat = cat_ref[...]
    if seq_tiles:
        first = (pl.program_id(0) % seq_tiles) == 0
        row = lax.broadcasted_iota(jnp.int32, (cat.shape[0], 1), 0)
        before = jnp.where(first, 0.0, prev_ref[SUBLANE - 1:SUBLANE, :])
        prev = jnp.where(row == 0, before, pltpu.roll(cat, 1, axis=0))
    else:
        prev = prev_ref[...]
    res = _rwkv_mix(cat, prev, p, None if vf_ref is None else vf_ref[...])
    for o_ref, val in zip(outs, res):
        o_ref[...] = val


def _rwkv_pre(proj, prev, rp, vfirst, batch):
    m = proj.shape[0]
    cw = 4 * D_HALF
    names = RWKV_PARAM_NAMES + (RWKV_VRES_NAMES if vfirst is not None else ())
    params = [rp[n] for n in names]
    if prev is None:
        seq = m // batch
        tm = _row_tile(seq, 256)
        seq_tiles = seq // tm
        hb = tm // SUBLANE
        prev_arg = proj
        prev_spec = pl.BlockSpec((SUBLANE, cw), lambda i: (jnp.maximum(i * hb - 1, 0), 0))
    else:
        tm, seq_tiles = m, 0
        prev_arg = prev
        prev_spec = pl.BlockSpec((tm, cw), lambda i: (i, 0))
    args = [proj, prev_arg]
    in_specs = [pl.BlockSpec((tm, cw), lambda i: (i, 0)), prev_spec]
    if vfirst is not None:
        args.append(vfirst)
        in_specs.append(pl.BlockSpec((tm, D_HALF), lambda i: (i, 0)))
    args += params
    in_specs += [_full(a.shape) for a in params]
    return pl.pallas_call(
        functools.partial(_rwkv_pre_kernel, names=names, has_vfirst=vfirst is not None, seq_tiles=seq_tiles),
        grid=(m // tm,),
        in_specs=in_specs,
        out_specs=[pl.BlockSpec((tm, D_HALF), lambda i: (i, 0))] * RWKV_N_OUT,
        out_shape=[jax.ShapeDtypeStruct((m, D_HALF), F32)] * RWKV_N_OUT,
        compiler_params=_cp("parallel"))(*args)


WKV_CHUNK = 64


def _bmm(a, b):
    return lax.dot_general(a.astype(BF16), b.astype(BF16), (((2,), (1,)), ((0,), (0,))),
                           preferred_element_type=F32)


def _bmm3(a, b):
    ah = a.astype(BF16)
    al = (a - ah.astype(F32)).astype(BF16)
    bh = b.astype(BF16)
    bl = (b - bh.astype(F32)).astype(BF16)
    return _bmm(ah, bh) + (_bmm(ah, bl) + _bmm(al, bh))


def _wkv_tile(r, lw, k, v, kk, b, st):
    tt = r.shape[0]
    c, n, nh = WKV_CHUNK, HEAD_DIM, HEADS_PER_TILE
    nc = tt // c
    ri = lax.broadcasted_iota(jnp.int32, (1, c, c), 1)
    ci = lax.broadcasted_iota(jnp.int32, (1, c, c), 2)
    incl, strict = ri >= ci, ri > ci
    eye = jnp.where(ri == ci, 1.0, 0.0)
    tril = jnp.where(incl[0], 1.0, 0.0)
    lw3 = lw.reshape(nc, c, LANE)
    cum = jnp.stack([jnp.dot(tril, lw3[j], precision=HIGHEST, preferred_element_type=F32) for j in range(nc)])
    tot = cum[:, c - 1:c, :]
    e_neg = jnp.exp(-cum)
    e_end = jnp.exp(tot - cum)
    e_tot = jnp.exp(tot)

    def heads(x):
        return jnp.concatenate([x[:, :, h * n:(h + 1) * n] for h in range(nh)], axis=0)

    def chunks(x):
        return x.reshape(nc, c, LANE)

    kkm = heads(chunks(kk) * jnp.exp(cum - lw3))
    rp = heads(chunks(r) * jnp.exp(cum))
    bo, ko = heads(chunks(b) * e_neg), heads(chunks(k) * e_neg)
    bend, kend = heads(chunks(b) * e_end), heads(chunks(k) * e_end)
    vh = heads(chunks(v))
    e_tot = heads(e_tot)

    qa = lax.dot_general(jnp.concatenate([kkm, rp], axis=1).astype(BF16),
                         jnp.concatenate([bo, ko], axis=1).astype(BF16),
                         (((2,), (2,)), ((0,), (0,))), preferred_element_type=F32)
    a_ub = jnp.where(strict, qa[:, :c, :c], 0.0)
    a_vk = jnp.where(strict, qa[:, :c, c:], 0.0)
    a_rb = jnp.where(incl, qa[:, c:, :c], 0.0)
    a_rk = jnp.where(incl, qa[:, c:, c:], 0.0)

    m = 2
    t_inv = eye - jnp.where((ri // m) == (ci // m), a_ub, 0.0)
    while m < c:
        off = jnp.logical_and((ri // (2 * m)) == (ci // (2 * m)), (ri // m) != (ci // m))
        t_inv = t_inv - _bmm(_bmm(t_inv, jnp.where(off, a_ub, 0.0)), t_inv)
        m *= 2

    av = _bmm(jnp.concatenate([a_vk, a_rk], axis=1), vh)
    x1 = _bmm(t_inv, jnp.concatenate([kkm, av[:, :c]], axis=2))
    x2 = _bmm(a_rb, x1)
    x3 = _bmm(jnp.swapaxes(bend, 1, 2), x1)
    rr = rp - x2[:, :, :n]
    o_loc = av[:, c:] - x2[:, :, n:]
    trans = eye * e_tot - x3[:, :, :n]
    s_loc = _bmm(jnp.swapaxes(kend, 1, 2), vh) - x3[:, :, n:]

    by_head = lambda x: x.reshape(nh, nc, x.shape[1], x.shape[2])
    rr, o_loc, trans, s_loc = by_head(rr), by_head(o_loc), by_head(trans), by_head(s_loc)
    outs = []
    for j in range(nc):
        outs.append(_bmm(rr[:, j], st) + o_loc[:, j])
        st = _bmm3(trans[:, j], st) + s_loc[:, j]
    o = jnp.stack(outs, axis=1).reshape(nh, tt, n)
    return jnp.concatenate([o[h] for h in range(nh)], axis=-1), st


WKV_TIME_TILE = 1024


def _seg_mean(x):
    return _seg_sum(x) * (1.0 / HEAD_DIM)


def _wkv_seq_kernel(r_ref, lw_ref, k_ref, v_ref, kk_ref, b_ref, g_ref, rk_ref, lnw_ref, lnb_ref,
                    o_ref, so_ref, st_ref):
    i = pl.program_id(2)

    @pl.when(i == 0)
    def _():
        st_ref[...] = jnp.zeros_like(st_ref)

    r, k, v = r_ref[...], k_ref[...], v_ref[...]
    o, st = _wkv_tile(r, lw_ref[...], k, v, kk_ref[...], b_ref[...], st_ref[...])
    st_ref[...] = st
    mean = _seg_mean(o)
    var = _seg_mean(jnp.square(o - mean))
    on = (o - mean) * lax.rsqrt(var + RWKV_GN_EPS) * lnw_ref[...] + lnb_ref[...]
    bonus = _seg_sum(r * k * rk_ref[...]) * v
    o_ref[...] = (on + bonus) * g_ref[...]
    for h in range(HEADS_PER_TILE):
        so_ref[0, h] = st[h].T


def _wkv_seq(r, lw, k, v, kk, b, g, r_k, ln_w, ln_b, batch):
    m = r.shape[0]
    seq = m // batch
    tt = _row_tile(seq, WKV_TIME_TILE)
    assert tt % WKV_CHUNK == 0
    nt = seq // tt
    row_spec = pl.BlockSpec((tt, LANE), lambda bi, t, i: (bi * nt + i, t))
    par_spec = pl.BlockSpec((1, LANE), lambda bi, t, i: (0, t))
    return pl.pallas_call(
        _wkv_seq_kernel,
        grid=(batch, HEAD_TILES, nt),
        in_specs=[row_spec] * 7 + [par_spec] * 3,
        out_specs=[row_spec,
                   pl.BlockSpec((1, HEADS_PER_TILE, HEAD_DIM, HEAD_DIM), lambda bi, t, i: (bi, t, 0, 0))],
        out_shape=[jax.ShapeDtypeStruct((m, D_HALF), F32),
                   jax.ShapeDtypeStruct((batch, HEADS, HEAD_DIM, HEAD_DIM), F32)],
        scratch_shapes=[pltpu.VMEM((HEADS_PER_TILE, HEAD_DIM, HEAD_DIM), F32)],
        compiler_params=_cp("parallel", "parallel", "arbitrary"),
    )(r, lw, k, v, kk, b, g, r_k.reshape(1, -1), ln_w.reshape(1, -1), ln_b.reshape(1, -1))


def _wkv_finish(o, r, k, v, g, rk, lnw, lnb):
    mean = jnp.mean(o, axis=-1, keepdims=True)
    var = jnp.mean(jnp.square(o - mean), axis=-1, keepdims=True)
    on = (o - mean) * lax.rsqrt(var + RWKV_GN_EPS) * lnw + lnb
    bonus = jnp.sum(r * k * rk, axis=-1, keepdims=True) * v
    return (on + bonus) * g


def _wkv_step_kernel(s_ref, r_ref, lw_ref, k_ref, v_ref, kk_ref, b_ref, g_ref, rk_ref, lnw_ref, lnb_ref,
                     o_ref, so_ref):
    ri = lax.broadcasted_iota(jnp.int32, (HEAD_DIM, HEAD_DIM), 0)
    eye = jnp.where(ri == lax.broadcasted_iota(jnp.int32, (HEAD_DIM, HEAD_DIM), 1), 1.0, 0.0)
    outs = []
    for h in range(HEADS):
        sl = slice(h * HEAD_DIM, (h + 1) * HEAD_DIM)
        r, lw, k, v, kk, b, g = [ref[0][:, sl] for ref in (r_ref, lw_ref, k_ref, v_ref, kk_ref, b_ref, g_ref)]
        s = s_ref[0, h]
        sa = -jnp.sum(s * kk, axis=-1, keepdims=True)
        v_col = jnp.sum(eye * v, axis=-1, keepdims=True)
        s = s * jnp.exp(lw) + sa * b + v_col * k
        so_ref[0, h] = s
        o_col = jnp.sum(s * r, axis=-1, keepdims=True)
        o = jnp.sum(eye * o_col, axis=0, keepdims=True)
        outs.append(_wkv_finish(o, r, k, v, g, rk_ref[:, sl], lnw_ref[:, sl], lnb_ref[:, sl]))
    o_ref[0] = jnp.concatenate(outs, axis=-1)


def _wkv_step(state, r, lw, k, v, kk, b, g, r_k, ln_w, ln_b):
    bsz = r.shape[0]
    rows = [t.reshape(bsz, 1, D_HALF) for t in (r, lw, k, v, kk, b, g)]
    row_spec = pl.BlockSpec((1, 1, D_HALF), lambda i: (i, 0, 0))
    st_spec = pl.BlockSpec((1, HEADS, HEAD_DIM, HEAD_DIM), lambda i: (i, 0, 0, 0))
    out, st = pl.pallas_call(
        _wkv_step_kernel,
        grid=(bsz,),
        in_specs=[st_spec] + [row_spec] * 7 + [_full((1, D_HALF))] * 3,
        out_specs=[row_spec, st_spec],
        out_shape=[jax.ShapeDtypeStruct((bsz, 1, D_HALF), F32),
                   jax.ShapeDtypeStruct(state.shape, F32)],
        compiler_params=_cp("parallel"),
    )(state, *rows, r_k.reshape(1, -1), ln_w.reshape(1, -1), ln_b.reshape(1, -1))
    return out.reshape(bsz, D_HALF), st


CONV_HALO = 32


def _layernorm_silu(y, g, b):
    m = jnp.mean(y, axis=-1, keepdims=True)
    var = jnp.mean(jnp.square(y - m), axis=-1, keepdims=True)
    return _silu((y - m) * lax.rsqrt(var + LN_EPS) * g + b)


def _convd_seq_kernel(gv_ref, gg_ref, hv_ref, hg_ref, w_ref, b_ref, lng_ref, lnb_ref,
                      o_ref, tail_ref, ext_ref, *, seq_tiles):
    tt = gv_ref.shape[0]
    first = (pl.program_id(0) % seq_tiles) == 0
    ext_ref[0:CONV_HALO, :] = jnp.where(first, 0.0, hv_ref[...] * _sigmoid(hg_ref[...]))
    ext_ref[CONV_HALO:, :] = gv_ref[...] * _sigmoid(gg_ref[...])
    off = CONV_HALO - (CONV_D_WIDTH - 1)
    acc = jnp.zeros((tt, D_HALF), F32)
    for j in range(CONV_D_WIDTH):
        acc = acc + w_ref[j:j + 1, :] * ext_ref[off + j:off + j + tt, :]
    o_ref[...] = _layernorm_silu(acc + b_ref[...], lng_ref[...], lnb_ref[...])
    tail_ref[0] = ext_ref[tt:, :]


def _convd_seq(proj, batch, w, b, ln_g, ln_b):
    m = proj.shape[0]
    seq = m // batch
    tt = _row_tile(seq, 256)
    seq_tiles = seq // tt
    hb = tt // CONV_HALO
    vcol, gcol = 4, 5
    tile = lambda col: pl.BlockSpec((tt, D_HALF), lambda i: (i, col))
    halo = lambda col: pl.BlockSpec((CONV_HALO, D_HALF), lambda i: (jnp.maximum(i * hb - 1, 0), col))
    params = [w, b.reshape(1, -1), ln_g.reshape(1, -1), ln_b.reshape(1, -1)]
    out, tail = pl.pallas_call(
        functools.partial(_convd_seq_kernel, seq_tiles=seq_tiles),
        grid=(m // tt,),
        in_specs=[tile(vcol), tile(gcol), halo(vcol), halo(gcol)] + [_full(p.shape) for p in params],
        out_specs=[pl.BlockSpec((tt, D_HALF), lambda i: (i, 0)),
                   pl.BlockSpec((1, CONV_HALO, D_HALF), lambda i: (i, 0, 0))],
        out_shape=[jax.ShapeDtypeStruct((m, D_HALF), F32),
                   jax.ShapeDtypeStruct((m // tt, CONV_HALO, D_HALF), F32)],
        scratch_shapes=[pltpu.VMEM((tt + CONV_HALO, D_HALF), F32)],
        compiler_params=_cp("parallel"))(proj, proj, proj, proj, *params)
    return out, tail[seq_tiles - 1::seq_tiles, CONV_HALO - (CONV_D_WIDTH - 1):]


def _convd_step_kernel(gv_ref, gg_ref, buf_ref, w_ref, b_ref, lng_ref, lnb_ref, o_ref, u_ref):
    u = gv_ref[...] * _sigmoid(gg_ref[...])
    u_ref[...] = u
    acc = u * w_ref[CONV_D_WIDTH - 1:CONV_D_WIDTH, :]
    for j in range(CONV_D_WIDTH - 1):
        acc = acc + w_ref[j:j + 1, :] * buf_ref[j]
    o_ref[...] = _layernorm_silu(acc + b_ref[...], lng_ref[...], lnb_ref[...])


def _convd_step(proj, buf, w, b, ln_g, ln_b):
    m = proj.shape[0]
    buf_t = jnp.swapaxes(buf, 0, 1)
    params = [w, b.reshape(1, -1), ln_g.reshape(1, -1), ln_b.reshape(1, -1)]
    out, u = pl.pallas_call(
        _convd_step_kernel,
        grid=(1,),
        in_specs=[pl.BlockSpec((m, D_HALF), lambda i: (0, 4)), pl.BlockSpec((m, D_HALF), lambda i: (0, 5)),
                  _full(buf_t.shape)] + [_full(p.shape) for p in params],
        out_specs=[_full((m, D_HALF)), _full((m, D_HALF))],
        out_shape=[jax.ShapeDtypeStruct((m, D_HALF), F32)] * 2,
        compiler_params=_cp("arbitrary"))(proj, proj, buf_t, *params)
    return out, jnp.concatenate([buf[:, 1:], u[:, None]], axis=1)


def _rmsnorm_kernel(x_ref, g_ref, o_ref):
    x = x_ref[...]
    ms = jnp.mean(x * x, axis=-1, keepdims=True)
    o_ref[...] = x * lax.rsqrt(ms + RMS_EPS) * g_ref[...]


def _rmsnorm(x, g):
    m, d = x.shape
    tm = _row_tile(m, 512)
    return pl.pallas_call(
        _rmsnorm_kernel,
        grid=(m // tm,),
        in_specs=[pl.BlockSpec((tm, d), lambda i: (i, 0)), _full((1, d))],
        out_specs=pl.BlockSpec((tm, d), lambda i: (i, 0)),
        out_shape=jax.ShapeDtypeStruct((m, d), F32),
        compiler_params=_cp("parallel"))(x, g.reshape(1, d))


def _kv_layout_kernel(*refs, n_layers, paged):
    k_refs, v_refs = refs[:n_layers], refs[n_layers:2 * n_layers]
    ko_ref, vo_ref = refs[2 * n_layers:]
    layer = pl.program_id(0)

    def emit(src_ref, dst_ref):
        x = src_ref[...]
        if not paged:
            dst_ref[...] = x.reshape(dst_ref.shape)
            return
        for pg in range(dst_ref.shape[0]):
            rows = x[pg * PAGE_SIZE:(pg + 1) * PAGE_SIZE]
            for h in range(HEADS):
                dst_ref[pg, h] = rows[:, h * HEAD_DIM:(h + 1) * HEAD_DIM].T

    for l in range(n_layers):
        @pl.when(layer == l)
        def _():
            emit(k_refs[l], ko_ref)
            emit(v_refs[l], vo_ref)


def _kv_layout(projs, paged):
    n = len(projs)
    m = projs[0].shape[0]
    tm = _row_tile(m, 512)
    nt = m // tm

    def col_spec(l, col):
        return pl.BlockSpec((tm, D_HALF), lambda layer, i: (jnp.where(layer == l, i, 0), col))

    if paged:
        ppt = tm // PAGE_SIZE
        out_spec = pl.BlockSpec((ppt, HEADS, HEAD_DIM, PAGE_SIZE), lambda layer, i: (layer * nt + i, 0, 0, 0))
        shp = jax.ShapeDtypeStruct((n * m // PAGE_SIZE, HEADS, HEAD_DIM, PAGE_SIZE), F32)
    else:
        out_spec = pl.BlockSpec((tm, HEADS, HEAD_DIM), lambda layer, i: (layer * nt + i, 0, 0))
        shp = jax.ShapeDtypeStruct((n * m, HEADS, HEAD_DIM), F32)
    return pl.pallas_call(
        functools.partial(_kv_layout_kernel, n_layers=n, paged=paged),
        grid=(n, nt),
        in_specs=[col_spec(l, 2) for l in range(n)] + [col_spec(l, 3) for l in range(n)],
        out_specs=[out_spec, out_spec],
        out_shape=[shp, shp],
        compiler_params=_cp("parallel", "parallel"))(*projs, *projs)


def _rwkv_params(p, li):
    row = lambda t: t.reshape(1, -1)
    rp = dict(mu=p['rwkv_mu_rkv'][li], mz=p['rwkv_mu_z'][li], w0=row(p['rwkv_w0'][li]),
              w1=p['rwkv_w1'][li].astype(BF16), w2=p['rwkv_w2'][li].astype(BF16),
              a0=row(p['rwkv_a0'][li]), a1=p['rwkv_a1'][li].astype(BF16), a2=p['rwkv_a2'][li].astype(BF16),
              g1=p['rwkv_g1'][li].astype(BF16), g2=p['rwkv_g2'][li].astype(BF16),
              k_k=row(p['rwkv_k_k'][li]), k_a=row(p['rwkv_k_a'][li]))
    if li > 0:
        rp.update(mu_v=row(p['rwkv_mu_v'][li - 1]), v0=row(p['rwkv_v0'][li - 1]),
                  v1=p['rwkv_v1'][li - 1].astype(BF16), v2=p['rwkv_v2'][li - 1].astype(BF16))
    return rp


def _prepare(p):
    depth = p['g_mix'].shape[0]
    layers = []
    for i in range(depth):
        li = i // 2
        lay = dict(g_mix=p['g_mix'][i], g_ffn=p['g_ffn'][i],
                   w_up=p['ffn_w_up'][i].astype(BF16), w_down=p['ffn_w_down'][i].astype(BF16),
                   ffn_cw=p['ffn_conv_w'][i], ffn_cb=p['ffn_conv_b'][i])
        if i % 2 == 0:
            lay.update(w_in=p['w_in_ab'][li].astype(BF16), w_out=p['w_out_ab'][li].astype(BF16),
                       s5=_s5_params(p['s5_a_re'][li], p['s5_a_im'][li], p['s5_log_step'][li],
                                     p['s5_b_re'][li], p['s5_b_im'][li], p['s5_c_re'][li], p['s5_c_im'][li],
                                     p['s5_d'][li]),
                       w_glu=p['s5_w_glu'][li].astype(BF16), b_glu=p['s5_b_glu'][li])
        else:
            lay.update(w_in=p['w_in_cd'][li].astype(BF16), w_out=p['w_out_cd'][li].astype(BF16),
                       rwkv=_rwkv_params(p, li), r_k=p['rwkv_r_k'][li], ln_w=p['rwkv_ln_w'][li],
                       ln_b=p['rwkv_ln_b'][li], cd_w=p['conv_d_w'][li], cd_b=p['conv_d_b'][li],
                       cd_g=p['conv_d_ln_g'][li], cd_lb=p['conv_d_ln_b'][li])
        layers.append(lay)
    return layers


def _trunk_seq(x3, layers, g_final):
    batch, seq, d = x3.shape
    x = x3.reshape(batch * seq, d)
    rope = _rope_tables(seq, 0, True)
    attn_projs, s5r, s5i, wkvs, shifts, convs, ffns = [], [], [], [], [], [], []
    v_first = None
    for i, lay in enumerate(layers):
        if i % 2 == 0:
            proj = _norm_matmul(x, lay['g_mix'], lay['w_in'], rope=rope, rope_cols=(1, 2))
            a_out, hr, hi = _s5_seq(proj, batch, lay['s5'], lay['w_glu'], lay['b_glu'])
            b_out = _moba_seq(proj, batch)
            attn_projs.append(proj)
            s5r.append(hr)
            s5i.append(hi)
            x = _out_proj(a_out, b_out, lay['w_out'], x)
        else:
            proj = _norm_matmul(x, lay['g_mix'], lay['w_in'])
            r, lw, k, v, kk, b, g = _rwkv_pre(proj, None, lay['rwkv'], v_first, batch)
            if v_first is None:
                v_first = v
            c_out, s_fin = _wkv_seq(r, lw, k, v, kk, b, g, lay['r_k'], lay['ln_w'], lay['ln_b'], batch)
            d_out, cbuf = _convd_seq(proj, batch, lay['cd_w'], lay['cd_b'], lay['cd_g'], lay['cd_lb'])
            wkvs.append(s_fin)
            shifts.append(proj.reshape(batch, seq, -1)[:, -1, :4 * D_HALF])
            convs.append(cbuf)
            x = _out_proj(c_out, d_out, lay['w_out'], x)
        x, fbuf = _ffn_seq(x, batch, lay['g_ffn'], lay['w_up'], lay['ffn_cw'], lay['ffn_cb'], lay['w_down'])
        ffns.append(fbuf)
    y = _rmsnorm(x, g_final).reshape(batch, seq, d)
    assert seq % PAGE_SIZE == 0
    k_all, v_all = _kv_layout(attn_projs, paged=True)
    return (y, k_all, v_all, jnp.stack(s5r), jnp.stack(s5i), jnp.stack(wkvs),
            jnp.stack(shifts), jnp.stack(convs), jnp.stack(ffns))


def _trunk_step(x3, pos0, layers, g_final, cache_k, cache_v, page_table, s5_re0, s5_im0, wkv0, shift0,
                convd0, ffn0):
    bsz, _, d = x3.shape
    x = x3.reshape(bsz, d)
    rope = _rope_tables(bsz, pos0, False)
    cache_kt = jnp.transpose(cache_k, (0, 1, 3, 4, 2))
    cache_vt = jnp.transpose(cache_v, (0, 1, 3, 4, 2))
    attn_projs, s5r, s5i, wkvs, shifts, convs, ffns = [], [], [], [], [], [], []
    v_first = None
    for i, lay in enumerate(layers):
        li = i // 2
        if i % 2 == 0:
            proj = _norm_matmul(x, lay['g_mix'], lay['w_in'], rope=rope, rope_cols=(1, 2))
            a_out, hr, hi = _s5_step(proj, s5_re0[li].reshape(bsz, -1), s5_im0[li].reshape(bsz, -1),
                                     lay['s5'], lay['w_glu'], lay['b_glu'])
            b_out = _moba_step(proj, cache_kt, cache_vt, li, page_table)
            attn_projs.append(proj)
            s5r.append(hr)
            s5i.append(hi)
            x = _out_proj(a_out, b_out, lay['w_out'], x)
        else:
            proj = _norm_matmul(x, lay['g_mix'], lay['w_in'])
            r, lw, k, v, kk, b, g = _rwkv_pre(proj, shift0[li], lay['rwkv'], v_first, bsz)
            if v_first is None:
                v_first = v
            c_out, s_fin = _wkv_step(wkv0[li], r, lw, k, v, kk, b, g, lay['r_k'], lay['ln_w'], lay['ln_b'])
            d_out, cbuf = _convd_step(proj, convd0[li], lay['cd_w'], lay['cd_b'], lay['cd_g'], lay['cd_lb'])
            wkvs.append(s_fin)
            shifts.append(proj[:, :4 * D_HALF])
            convs.append(cbuf)
            x = _out_proj(c_out, d_out, lay['w_out'], x)
        x, fbuf = _ffn_step(x, ffn0[i], lay['g_ffn'], lay['w_up'], lay['ffn_cw'], lay['ffn_cb'], lay['w_down'])
        ffns.append(fbuf)
    y = _rmsnorm(x, g_final).reshape(bsz, 1, d)
    k_all, v_all = _kv_layout(attn_projs, paged=False)
    return (y, k_all, v_all, jnp.stack(s5r), jnp.stack(s5i), jnp.stack(wkvs),
            jnp.stack(shifts), jnp.stack(convs), jnp.stack(ffns))


def kernel(x_prompt, x_sample, cache_k_moba, cache_v_moba, page_table, state_s5_re, state_s5_im,
           state_rwkv_wkv, state_rwkv_shift, state_conv_d, state_ffn_conv, g_mix, g_ffn, g_final,
           w_in_ab, w_out_ab, s5_a_re, s5_a_im, s5_log_step, s5_b_re, s5_b_im, s5_c_re, s5_c_im, s5_d,
           s5_w_glu, s5_b_glu, w_in_cd, w_out_cd, rwkv_mu_rkv, rwkv_mu_z, rwkv_w0, rwkv_w1, rwkv_w2,
           rwkv_a0, rwkv_a1, rwkv_a2, rwkv_g1, rwkv_g2, rwkv_k_k, rwkv_k_a, rwkv_r_k, rwkv_ln_w, rwkv_ln_b,
           rwkv_mu_v, rwkv_v0, rwkv_v1, rwkv_v2, conv_d_w, conv_d_b, conv_d_ln_g, conv_d_ln_b,
           ffn_w_up, ffn_conv_w, ffn_conv_b, ffn_w_down):
    p = dict(g_mix=g_mix, g_ffn=g_ffn, w_in_ab=w_in_ab, w_out_ab=w_out_ab,
             s5_a_re=s5_a_re, s5_a_im=s5_a_im, s5_log_step=s5_log_step, s5_b_re=s5_b_re, s5_b_im=s5_b_im,
             s5_c_re=s5_c_re, s5_c_im=s5_c_im, s5_d=s5_d, s5_w_glu=s5_w_glu, s5_b_glu=s5_b_glu,
             w_in_cd=w_in_cd, w_out_cd=w_out_cd, rwkv_mu_rkv=rwkv_mu_rkv, rwkv_mu_z=rwkv_mu_z,
             rwkv_w0=rwkv_w0, rwkv_w1=rwkv_w1, rwkv_w2=rwkv_w2, rwkv_a0=rwkv_a0, rwkv_a1=rwkv_a1,
             rwkv_a2=rwkv_a2, rwkv_g1=rwkv_g1, rwkv_g2=rwkv_g2, rwkv_k_k=rwkv_k_k, rwkv_k_a=rwkv_k_a,
             rwkv_r_k=rwkv_r_k.reshape(rwkv_r_k.shape[0], -1), rwkv_ln_w=rwkv_ln_w, rwkv_ln_b=rwkv_ln_b,
             rwkv_mu_v=rwkv_mu_v, rwkv_v0=rwkv_v0, rwkv_v1=rwkv_v1, rwkv_v2=rwkv_v2,
             conv_d_w=conv_d_w, conv_d_b=conv_d_b, conv_d_ln_g=conv_d_ln_g, conv_d_ln_b=conv_d_ln_b,
             ffn_w_up=ffn_w_up, ffn_conv_w=ffn_conv_w, ffn_conv_b=ffn_conv_b, ffn_w_down=ffn_w_down)
    layers = _prepare(p)
    bp, seq, _ = x_prompt.shape
    bs = x_sample.shape[0]
    n_ab = w_in_ab.shape[0]
    n_cd = w_in_cd.shape[0]
    page = cache_k_moba.shape[2]
    past_len = page_table.shape[1] * page
    (y_p, k_p, v_p, s5r_p, s5i_p, wkv_p, sh_p, cd_p, ff_p) = _trunk_seq(x_prompt, layers, g_final)
    (y_s, k_s, v_s, s5r_s, s5i_s, wkv_s, sh_s, cd_s, ff_s) = _trunk_step(
        x_sample, past_len, layers, g_final, cache_k_moba, cache_v_moba, page_table, state_s5_re, state_s5_im,
        state_rwkv_wkv, state_rwkv_shift, state_conv_d, state_ffn_conv)
    kv_s = (n_ab, bs, 1, HEADS, HEAD_DIM)
    s5_p = (n_ab, bp, S5_GROUPS, S5_STATE)
    s5_s = (n_ab, bs, S5_GROUPS, S5_STATE)

    def pages_out(t):
        t = t.reshape(n_ab, bp, seq // page, HEADS, HEAD_DIM, page)
        return jnp.transpose(t, (0, 1, 2, 5, 3, 4))

    return (y_p, y_s, pages_out(k_p), pages_out(v_p), k_s.reshape(kv_s), v_s.reshape(kv_s),
            s5r_p.reshape(s5_p), s5i_p.reshape(s5_p), s5r_s.reshape(s5_s), s5i_s.reshape(s5_s),
            wkv_p, wkv_s, sh_p, sh_s, cd_p, cd_s, ff_p, ff_s)
```

```python
import functools
import math

import numpy as np
import jax
import jax.numpy as jnp
from jax import lax
from jax.experimental import pallas as pl
from jax.experimental.pallas import tpu as pltpu

F32 = jnp.float32
BF16 = jnp.bfloat16
HIGHEST = lax.Precision.HIGHEST

D_MODEL = 1024
D_HALF = 512
S5_GROUP = 16
S5_GROUPS = 32
S5_STATE = 64
S5_LANES = S5_GROUPS * S5_STATE
HEADS = 8
HEAD_DIM = 64
MOBA_BLOCK = 256
MOBA_TOPK = 3
PAGE_SIZE = 128
ROPE_DIM = 16
ROPE_THETA = 500000.0
RWKV_GN_EPS = 64e-5
CONV_D_WIDTH = 31
D_FF = 2816
RMS_EPS = 1e-6
LN_EPS = 1e-5
LANE = 128
SUBLANE = 8
VMEM_LIMIT = 48 * 1024 * 1024
NEG = -1e30


def _cp(*sem):
    return pltpu.CompilerParams(dimension_semantics=sem, vmem_limit_bytes=VMEM_LIMIT)


def _bdot(a, b):
    return jnp.dot(a.astype(BF16), b.astype(BF16), preferred_element_type=F32)


def _sigmoid(x):
    return 1.0 / (1.0 + jnp.exp(-x))


def _silu(x):
    return x * _sigmoid(x)


def _row_tile(m, pref):
    return pref if m % pref == 0 else m


def _rope_table_kernel(inv_ref, cos_ref, sa_ref, sb_ref, *, pos0, per_row_pos):
    rows = cos_ref.shape[0]
    lane = lax.broadcasted_iota(jnp.int32, (rows, LANE), 1) % HEAD_DIM
    if per_row_pos:
        pos = (pos0 + lax.broadcasted_iota(jnp.int32, (rows, LANE), 0)).astype(F32)
    else:
        pos = jnp.full((rows, LANE), pos0, F32)
    ang = pos * inv_ref[...]
    c, s = jnp.cos(ang), jnp.sin(ang)
    cos_ref[...] = jnp.where(lane < ROPE_DIM, c, 1.0)
    sa_ref[...] = jnp.where(lane < ROPE_DIM // 2, -s, 0.0)
    sb_ref[...] = jnp.where((lane >= ROPE_DIM // 2) & (lane < ROPE_DIM), s, 0.0)


def _rope_tables(rows, pos0, per_row_pos):
    half = ROPE_DIM // 2
    inv8 = (np.float32(1.0) / (np.float32(ROPE_THETA) ** (np.arange(half, dtype=np.float32) / np.float32(half))))
    inv = np.zeros((HEAD_DIM,), np.float32)
    inv[:half] = inv8
    inv[half:ROPE_DIM] = inv8
    inv = jnp.asarray(np.tile(inv, LANE // HEAD_DIM)[None, :])
    shp = jax.ShapeDtypeStruct((rows, LANE), F32)
    return pl.pallas_call(
        functools.partial(_rope_table_kernel, pos0=pos0, per_row_pos=per_row_pos),
        out_shape=(shp, shp, shp))(inv)


def _norm_matmul_kernel(x_ref, g_ref, w_ref, *rest, rope_cols):
    if rope_cols:
        cos_ref, sa_ref, sb_ref, o_ref = rest
    else:
        (o_ref,) = rest
    x = x_ref[...]
    ms = jnp.mean(x * x, axis=-1, keepdims=True)
    h = (x * lax.rsqrt(ms + RMS_EPS) * g_ref[...]).astype(BF16)
    tn = D_HALF
    for j in range(o_ref.shape[1] // tn):
        y = jnp.dot(h, w_ref[:, j * tn:(j + 1) * tn], preferred_element_type=F32)
        if j not in rope_cols:
            o_ref[:, j * tn:(j + 1) * tn] = y
            continue
        cos, sa, sb = cos_ref[...], sa_ref[...], sb_ref[...]
        for c in range(tn // LANE):
            yc = y[:, c * LANE:(c + 1) * LANE]
            o_ref[:, j * tn + c * LANE:j * tn + (c + 1) * LANE] = (
                yc * cos + pltpu.roll(yc, LANE - ROPE_DIM // 2, axis=1) * sa
                + pltpu.roll(yc, ROPE_DIM // 2, axis=1) * sb)


def _norm_matmul(x, g, w_bf, rope=None, rope_cols=()):
    m, d = x.shape
    n = w_bf.shape[1]
    tm = _row_tile(m, 512)
    in_specs = [pl.BlockSpec((tm, d), lambda i: (i, 0)),
                pl.BlockSpec((1, d), lambda i: (0, 0)),
                pl.BlockSpec((d, n), lambda i: (0, 0))]
    args = [x, g.reshape(1, d), w_bf]
    if rope_cols:
        nt = rope[0].shape[0] // tm
        for t in rope:
            in_specs.append(pl.BlockSpec((tm, LANE), lambda i: (i % nt, 0)))
            args.append(t)
    return pl.pallas_call(
        functools.partial(_norm_matmul_kernel, rope_cols=tuple(rope_cols)),
        grid=(m // tm,),
        in_specs=in_specs,
        out_specs=pl.BlockSpec((tm, n), lambda i: (i, 0)),
        out_shape=jax.ShapeDtypeStruct((m, n), F32),
        compiler_params=_cp("parallel"))(*args)


def _out_proj_kernel(a_ref, b_ref, wa_ref, wb_ref, res_ref, o_ref):
    o_ref[...] = res_ref[...] + (_bdot(a_ref[...], wa_ref[...]) + _bdot(b_ref[...], wb_ref[...]))


def _out_proj(a, b, w_bf, res):
    m, c = a.shape
    n = w_bf.shape[1]
    tm = _row_tile(m, 512)
    return pl.pallas_call(
        _out_proj_kernel,
        grid=(m // tm,),
        in_specs=[pl.BlockSpec((tm, c), lambda i: (i, 0)),
                  pl.BlockSpec((tm, c), lambda i: (i, 0)),
                  pl.BlockSpec((c, n), lambda i: (0, 0)),
                  pl.BlockSpec((c, n), lambda i: (1, 0)),
                  pl.BlockSpec((tm, n), lambda i: (i, 0))],
        out_specs=pl.BlockSpec((tm, n), lambda i: (i, 0)),
        out_shape=jax.ShapeDtypeStruct((m, n), F32),
        compiler_params=_cp("parallel"))(a, b, w_bf, w_bf, res)


FFN_TN = 256
FFN_TM = 1024
FFN_HALO = SUBLANE


def _ffn_seq_kernel(x_ref, xh_ref, g_ref, wa_ref, wb_ref, cwa_ref, cwb_ref, cba_ref, cbb_ref, wd_ref,
                    o_ref, ua_ref, ub_ref, h_ref, acc_ref, *, tiles_per_seq):
    i, c = pl.program_id(0), pl.program_id(1)
    tm = x_ref.shape[0]
    first = (i % tiles_per_seq) == 0

    @pl.when(c == 0)
    def _():
        def norm(x):
            ms = jnp.mean(x * x, axis=-1, keepdims=True)
            return (x * lax.rsqrt(ms + RMS_EPS) * g_ref[...]).astype(BF16)
        h_ref[0:FFN_HALO, :] = jnp.where(first, jnp.zeros((), BF16), norm(xh_ref[...]))
        h_ref[FFN_HALO:, :] = norm(x_ref[...])
        acc_ref[...] = jnp.zeros_like(acc_ref)

    h = h_ref[...]

    def branch(w_ref, cw_ref, cb_ref, tail_ref):
        u = jnp.dot(h, w_ref[...], preferred_element_type=F32)
        tail_ref[0] = u[tm + FFN_HALO - 2:, :]
        cw = cw_ref[...]
        y = (u * cw[2:3, :] + pltpu.roll(u, 1, axis=0) * cw[1:2, :]
             + pltpu.roll(u, 2, axis=0) * cw[0:1, :])
        return y[FFN_HALO:, :] + cb_ref[...]

    a = branch(wa_ref, cwa_ref, cba_ref, ua_ref)
    b = branch(wb_ref, cwb_ref, cbb_ref, ub_ref)
    acc_ref[...] += _bdot(_silu(a) * b, wd_ref[...])

    @pl.when(c == pl.num_programs(1) - 1)
    def _():
        o_ref[...] = x_ref[...] + acc_ref[...]


def _ffn_seq(x, batch, g, w_up_bf, cw, cb, w_down_bf):
    m, d = x.shape
    seq = m // batch
    tm = _row_tile(seq, FFN_TM)
    tps = seq // tm
    nc = D_FF // FFN_TN
    hb = tm // FFN_HALO
    out, ua, ub = pl.pallas_call(
        functools.partial(_ffn_seq_kernel, tiles_per_seq=tps),
        grid=(m // tm, nc),
        in_specs=[pl.BlockSpec((tm, d), lambda i, c: (i, 0)),
                  pl.BlockSpec((FFN_HALO, d), lambda i, c: (jnp.maximum(i * hb - 1, 0), 0)),
                  pl.BlockSpec((1, d), lambda i, c: (0, 0)),
                  pl.BlockSpec((d, FFN_TN), lambda i, c: (0, c)),
                  pl.BlockSpec((d, FFN_TN), lambda i, c: (0, nc + c)),
                  pl.BlockSpec((3, FFN_TN), lambda i, c: (0, c)),
                  pl.BlockSpec((3, FFN_TN), lambda i, c: (0, nc + c)),
                  pl.BlockSpec((1, FFN_TN), lambda i, c: (0, c)),
                  pl.BlockSpec((1, FFN_TN), lambda i, c: (0, nc + c)),
                  pl.BlockSpec((FFN_TN, d), lambda i, c: (c, 0))],
        out_specs=[pl.BlockSpec((tm, d), lambda i, c: (i, 0)),
                   pl.BlockSpec((1, 2, FFN_TN), lambda i, c: (i, 0, c)),
                   pl.BlockSpec((1, 2, FFN_TN), lambda i, c: (i, 0, c))],
        out_shape=[jax.ShapeDtypeStruct((m, d), F32),
                   jax.ShapeDtypeStruct((m // tm, 2, D_FF), F32),
                   jax.ShapeDtypeStruct((m // tm, 2, D_FF), F32)],
        scratch_shapes=[pltpu.VMEM((tm + FFN_HALO, d), BF16), pltpu.VMEM((tm, d), F32)],
        compiler_params=_cp("arbitrary", "arbitrary"),
    )(x, x, g.reshape(1, d), w_up_bf, w_up_bf, cw, cw, cb.reshape(1, -1), cb.reshape(1, -1), w_down_bf)
    return out, jnp.concatenate([ua[tps - 1::tps], ub[tps - 1::tps]], axis=-1)


def _ffn_step_kernel(x_ref, g_ref, wa_ref, wb_ref, cwa_ref, cwb_ref, cba_ref, cbb_ref, wd_ref,
                     bufa_ref, bufb_ref, o_ref, ua_ref, ub_ref, h_ref, acc_ref):
    c = pl.program_id(0)

    @pl.when(c == 0)
    def _():
        x = x_ref[...]
        ms = jnp.mean(x * x, axis=-1, keepdims=True)
        h_ref[...] = (x * lax.rsqrt(ms + RMS_EPS) * g_ref[...]).astype(BF16)
        acc_ref[...] = jnp.zeros_like(acc_ref)

    h = h_ref[...]

    def branch(w_ref, cw_ref, cb_ref, buf_ref, u_ref):
        u = jnp.dot(h, w_ref[...], preferred_element_type=F32)
        u_ref[...] = u
        cw = cw_ref[...]
        return u * cw[2:3, :] + buf_ref[1] * cw[1:2, :] + buf_ref[0] * cw[0:1, :] + cb_ref[...]

    a = branch(wa_ref, cwa_ref, cba_ref, bufa_ref, ua_ref)
    b = branch(wb_ref, cwb_ref, cbb_ref, bufb_ref, ub_ref)
    acc_ref[...] += _bdot(_silu(a) * b, wd_ref[...])

    @pl.when(c == pl.num_programs(0) - 1)
    def _():
        o_ref[...] = x_ref[...] + acc_ref[...]


def _ffn_step(x, buf, g, w_up_bf, cw, cb, w_down_bf):
    m, d = x.shape
    nc = D_FF // FFN_TN
    buf_t = jnp.swapaxes(buf, 0, 1)
    out, ua, ub = pl.pallas_call(
        _ffn_step_kernel,
        grid=(nc,),
        in_specs=[pl.BlockSpec((m, d), lambda c: (0, 0)),
                  pl.BlockSpec((1, d), lambda c: (0, 0)),
                  pl.BlockSpec((d, FFN_TN), lambda c: (0, c)),
                  pl.BlockSpec((d, FFN_TN), lambda c: (0, nc + c)),
                  pl.BlockSpec((3, FFN_TN), lambda c: (0, c)),
                  pl.BlockSpec((3, FFN_TN), lambda c: (0, nc + c)),
                  pl.BlockSpec((1, FFN_TN), lambda c: (0, c)),
                  pl.BlockSpec((1, FFN_TN), lambda c: (0, nc + c)),
                  pl.BlockSpec((FFN_TN, d), lambda c: (c, 0)),
                  pl.BlockSpec((2, m, FFN_TN), lambda c: (0, 0, c)),
                  pl.BlockSpec((2, m, FFN_TN), lambda c: (0, 0, nc + c))],
        out_specs=[pl.BlockSpec((m, d), lambda c: (0, 0)),
                   pl.BlockSpec((m, FFN_TN), lambda c: (0, c)),
                   pl.BlockSpec((m, FFN_TN), lambda c: (0, c))],
        out_shape=[jax.ShapeDtypeStruct((m, d), F32),
                   jax.ShapeDtypeStruct((m, D_FF), F32),
                   jax.ShapeDtypeStruct((m, D_FF), F32)],
        scratch_shapes=[pltpu.VMEM((m, d), BF16), pltpu.VMEM((m, d), F32)],
        compiler_params=_cp("arbitrary"),
    )(x, g.reshape(1, d), w_up_bf, w_up_bf, cw, cw, cb.reshape(1, -1), cb.reshape(1, -1), w_down_bf,
      buf_t, buf_t)
    u = jnp.concatenate([ua, ub], axis=-1)
    return out, jnp.stack([buf[:, 1], u], axis=1)


S5_STRIP = 512
S5_NSTRIP = S5_LANES // S5_STRIP
S5_TIME_TILE = 512


def _s5_param_kernel(are_ref, aim_ref, ls_ref, brt_ref, bit_ref, abr_ref, abi_ref, bbr_ref, bbi_ref):
    ar, ai = are_ref[...], aim_ref[...]
    dt = jnp.exp(ls_ref[...])
    mag = jnp.exp(ar * dt)
    abr, abi = mag * jnp.cos(ai * dt), mag * jnp.sin(ai * dt)
    inv_abs2 = 1.0 / (ar * ar + ai * ai)
    cr = ((abr - 1.0) * ar + abi * ai) * inv_abs2
    ci = (abi * ar - (abr - 1.0) * ai) * inv_abs2
    br, bi = brt_ref[...], bit_ref[...]
    abr_ref[...] = abr
    abi_ref[...] = abi
    bbr_ref[...] = cr * br - ci * bi
    bbi_ref[...] = cr * bi + ci * br


def _s5_params(a_re, a_im, log_step, b_re, b_im, c_re, c_im, d_skip):
    g, n, k = b_re.shape
    rep = lambda t: jnp.repeat(t, k, axis=0)
    brt = b_re.transpose(0, 2, 1).reshape(g * k, n)
    bit = b_im.transpose(0, 2, 1).reshape(g * k, n)
    shp = jax.ShapeDtypeStruct((g * k, n), F32)
    abr, abi, bbr, bbi = pl.pallas_call(_s5_param_kernel, out_shape=(shp, shp, shp, shp))(
        rep(a_re), rep(a_im), rep(log_step[:, None]), brt, bit)
    abr = abr[::k].reshape(1, g * n)
    abi = abi[::k].reshape(1, g * n)
    gs = S5_STRIP // n
    eye = jnp.eye(gs, dtype=F32)

    def in_blocks(t):
        t = t.reshape(S5_NSTRIP, gs, k, n)
        return jnp.einsum('cgkn,gh->cgkhn', t, eye).reshape(S5_NSTRIP, gs * k, gs * n).astype(BF16)

    def out_blocks(t):
        t = t.reshape(S5_NSTRIP, gs, k, n)
        return jnp.einsum('cgkn,gh->cgnhk', t, eye).reshape(S5_NSTRIP, gs * n, gs * k).astype(BF16)

    return dict(abr=abr, abi=abi, wbr=in_blocks(bbr), wbi=in_blocks(bbi),
                wcr=out_blocks(c_re), wci=out_blocks(c_im), d=d_skip.reshape(1, g * k))


def _s5_input(u, wbr_ref, wbi_ref, c):
    cw = S5_STRIP // (S5_STATE // S5_GROUP)
    uc = u[:, c * cw:(c + 1) * cw].astype(BF16)
    return (jnp.dot(uc, wbr_ref[c], preferred_element_type=F32),
            jnp.dot(uc, wbi_ref[c], preferred_element_type=F32))


def _s5_output(u, h_strip, wcr_ref, wci_ref, d_ref, wg_ref, bg_ref):
    ys = []
    for c in range(S5_NSTRIP):
        hr, hi = h_strip(c)
        ys.append(jnp.dot(hr.astype(BF16), wcr_ref[c], preferred_element_type=F32)
                  - jnp.dot(hi.astype(BF16), wci_ref[c], preferred_element_type=F32))
    y = jnp.concatenate(ys, axis=-1) + d_ref[...] * u
    yg = 0.5 * y * (1.0 + jnp.tanh(math.sqrt(2.0 / math.pi) * (y + 0.044715 * (y * y * y))))
    z = jnp.dot(yg.astype(BF16), wg_ref[...], preferred_element_type=F32) + bg_ref[...]
    return yg * _sigmoid(z)


def _cmul(ar, ai, br, bi):
    return ar * br - ai * bi, ar * bi + ai * br


def _s5_seq_kernel(u_ref, abr_ref, abi_ref, wbr_ref, wbi_ref, wcr_ref, wci_ref, d_ref, wg_ref, bg_ref,
                   o_ref, hro_ref, hio_ref, hr_ref, hi_ref, cr_ref, ci_ref, tab_ref):
    i = pl.program_id(1)
    tt = u_ref.shape[0]

    @pl.when(i == 0)
    def _():
        cr_ref[...] = jnp.zeros_like(cr_ref)
        ci_ref[...] = jnp.zeros_like(ci_ref)
        row = lax.broadcasted_iota(jnp.int32, (SUBLANE, S5_LANES), 0)
        p_r, p_i = [abr_ref[...]], [abi_ref[...]]
        for _ in range(SUBLANE - 1):
            nr, ni = _cmul(p_r[-1], p_i[-1], p_r[0], p_i[0])
            p_r.append(nr)
            p_i.append(ni)
        for k, s in enumerate((1, 2, 4)):
            tab_ref[2 * k] = jnp.where(row >= s, p_r[s - 1], 0.0)
            tab_ref[2 * k + 1] = jnp.where(row >= s, p_i[s - 1], 0.0)
        car_r = jnp.zeros((SUBLANE, S5_LANES), F32)
        car_i = jnp.zeros((SUBLANE, S5_LANES), F32)
        for j in range(SUBLANE):
            car_r = jnp.where(row == j, p_r[j], car_r)
            car_i = jnp.where(row == j, p_i[j], car_i)
        tab_ref[6] = car_r
        tab_ref[7] = car_i

    u = u_ref[...]
    for c in range(S5_NSTRIP):
        br, bi = _s5_input(u, wbr_ref, wbi_ref, c)
        hr_ref[:, c * S5_STRIP:(c + 1) * S5_STRIP] = br
        hi_ref[:, c * S5_STRIP:(c + 1) * S5_STRIP] = bi

    def block(r, carry):
        r8 = pl.multiple_of(r * SUBLANE, SUBLANE)
        for c in range(S5_NSTRIP):
            sl = slice(c * S5_STRIP, (c + 1) * S5_STRIP)
            xr, xi = hr_ref[pl.ds(r8, SUBLANE), sl], hi_ref[pl.ds(r8, SUBLANE), sl]
            for k, s in enumerate((1, 2, 4)):
                dr, di = _cmul(tab_ref[2 * k, :, sl], tab_ref[2 * k + 1, :, sl],
                               pltpu.roll(xr, s, axis=0), pltpu.roll(xi, s, axis=0))
                xr, xi = xr + dr, xi + di
            dr, di = _cmul(tab_ref[6, :, sl], tab_ref[7, :, sl], cr_ref[:, sl], ci_ref[:, sl])
            xr, xi = xr + dr, xi + di
            hr_ref[pl.ds(r8, SUBLANE), sl] = xr
            hi_ref[pl.ds(r8, SUBLANE), sl] = xi
            cr_ref[:, sl] = xr[SUBLANE - 1:, :]
            ci_ref[:, sl] = xi[SUBLANE - 1:, :]
        return carry

    lax.fori_loop(0, tt // SUBLANE, block, 0)

    def h_strip(c):
        sl = slice(c * S5_STRIP, (c + 1) * S5_STRIP)
        return hr_ref[:, sl], hi_ref[:, sl]

    o_ref[...] = _s5_output(u, h_strip, wcr_ref, wci_ref, d_ref, wg_ref, bg_ref)
    hro_ref[0] = cr_ref[...]
    hio_ref[0] = ci_ref[...]


def _full(shape):
    nd = len(shape)
    return pl.BlockSpec(shape, lambda *_: (0,) * nd)


def _s5_seq(proj, batch, sp, w_glu_bf, b_glu):
    m = proj.shape[0]
    seq = m // batch
    tt = _row_tile(seq, S5_TIME_TILE)
    nt = seq // tt
    params = [sp['abr'], sp['abi'], sp['wbr'], sp['wbi'], sp['wcr'], sp['wci'], sp['d'],
              w_glu_bf, b_glu.reshape(1, -1)]
    out, hr, hi = pl.pallas_call(
        _s5_seq_kernel,
        grid=(batch, nt),
        in_specs=[pl.BlockSpec((tt, D_HALF), lambda b, i: (b * nt + i, 0))] + [_full(p.shape) for p in params],
        out_specs=[pl.BlockSpec((tt, D_HALF), lambda b, i: (b * nt + i, 0)),
                   pl.BlockSpec((1, 1, S5_LANES), lambda b, i: (b, 0, 0)),
                   pl.BlockSpec((1, 1, S5_LANES), lambda b, i: (b, 0, 0))],
        out_shape=[jax.ShapeDtypeStruct((m, D_HALF), F32),
                   jax.ShapeDtypeStruct((batch, 1, S5_LANES), F32),
                   jax.ShapeDtypeStruct((batch, 1, S5_LANES), F32)],
        scratch_shapes=[pltpu.VMEM((tt, S5_LANES), F32), pltpu.VMEM((tt, S5_LANES), F32),
                        pltpu.VMEM((1, S5_LANES), F32), pltpu.VMEM((1, S5_LANES), F32),
                        pltpu.VMEM((8, SUBLANE, S5_LANES), F32)],
        compiler_params=_cp("arbitrary", "arbitrary"))(proj, *params)
    return out, hr[:, 0], hi[:, 0]


def _s5_step_kernel(u_ref, h0r_ref, h0i_ref, abr_ref, abi_ref, wbr_ref, wbi_ref, wcr_ref, wci_ref, d_ref,
                    wg_ref, bg_ref, o_ref, hro_ref, hio_ref):
    u = u_ref[...]
    for c in range(S5_NSTRIP):
        sl = slice(c * S5_STRIP, (c + 1) * S5_STRIP)
        br, bi = _s5_input(u, wbr_ref, wbi_ref, c)
        dr, di = _cmul(abr_ref[:, sl], abi_ref[:, sl], h0r_ref[:, sl], h0i_ref[:, sl])
        hro_ref[:, sl] = dr + br
        hio_ref[:, sl] = di + bi

    def h_strip(c):
        sl = slice(c * S5_STRIP, (c + 1) * S5_STRIP)
        return hro_ref[:, sl], hio_ref[:, sl]

    o_ref[...] = _s5_output(u, h_strip, wcr_ref, wci_ref, d_ref, wg_ref, bg_ref)


def _s5_step(proj, h0r, h0i, sp, w_glu_bf, b_glu):
    m = proj.shape[0]
    args = [h0r, h0i, sp['abr'], sp['abi'], sp['wbr'], sp['wbi'], sp['wcr'], sp['wci'], sp['d'],
            w_glu_bf, b_glu.reshape(1, -1)]
    return pl.pallas_call(
        _s5_step_kernel,
        grid=(1,),
        in_specs=[pl.BlockSpec((m, D_HALF), lambda i: (0, 0))] + [_full(a.shape) for a in args],
        out_specs=[_full((m, D_HALF)), _full((m, S5_LANES)), _full((m, S5_LANES))],
        out_shape=[jax.ShapeDtypeStruct((m, D_HALF), F32),
                   jax.ShapeDtypeStruct((m, S5_LANES), F32),
                   jax.ShapeDtypeStruct((m, S5_LANES), F32)],
        compiler_params=_cp("arbitrary"))(proj, *args)


HEADS_PER_TILE = LANE // HEAD_DIM
HEAD_TILES = HEADS // HEADS_PER_TILE
ATTN_SCALE = HEAD_DIM ** -0.5
NT_DIMS = (((1,), (1,)), ((), ()))
MOBA_GROUP = 4


def _moba_seq_kernel(q_ref, k_ref, v_ref, o_ref, kbf_ref, vt_ref, kmean_ref, sel_ref, *, nb):
    qi = pl.program_id(2)
    blk = MOBA_BLOCK

    @pl.when(qi == 0)
    def _():
        kbf_ref[...] = k_ref[...].astype(BF16)
        vt_ref[...] = v_ref[...].T.astype(BF16)
        for j in range(nb):
            kmean_ref[j:j + 1, :] = jnp.mean(k_ref[j * blk:(j + 1) * blk, :], axis=0, keepdims=True)

    q = q_ref[...]
    row = lax.broadcasted_iota(jnp.int32, (nb, blk), 0)
    causal = (lax.broadcasted_iota(jnp.int32, (blk, blk), 0) <= lax.broadcasted_iota(jnp.int32, (blk, blk), 1))
    q0 = pl.multiple_of(qi * blk, blk)
    head_lanes = [slice(h * HEAD_DIM, (h + 1) * HEAD_DIM) for h in range(HEADS_PER_TILE)]
    qts, state = [], []
    for h, sl in enumerate(head_lanes):
        qh = q[:, sl]
        s = lax.dot_general(kmean_ref[:, sl], qh, NT_DIMS, precision=HIGHEST, preferred_element_type=F32)
        rank = jnp.zeros((nb, blk), F32)
        for i in range(nb):
            si = s[i:i + 1, :]
            beats = jnp.logical_or(si > s, jnp.logical_and(si == s, i < row))
            rank = rank + jnp.where(jnp.logical_and(beats, i < qi), 1.0, 0.0)
        sel_ref[h] = jnp.where(jnp.logical_and(row < qi, rank < MOBA_TOPK), 1.0, 0.0)

        qt = (qh * ATTN_SCALE).T.astype(BF16)
        lg = jnp.dot(kbf_ref[pl.ds(q0, blk), sl], qt, preferred_element_type=F32)
        lg = jnp.where(causal, lg, NEG)
        m = jnp.max(lg, axis=0, keepdims=True)
        p = jnp.exp(lg - m)
        l = jnp.sum(p, axis=0, keepdims=True)
        acc = jnp.dot(vt_ref[sl, pl.ds(q0, blk)], p.astype(BF16), preferred_element_type=F32)
        qts.append(qt)
        state += [m, l, acc]

    grp = MOBA_GROUP
    zero = jnp.zeros_like(qts[0])
    qt_both = jnp.concatenate([jnp.concatenate([qts[0], zero], axis=1),
                               jnp.concatenate([zero, qts[1]], axis=1)], axis=0)

    def body(jj, carry):
        j0 = pl.multiple_of(jj * (grp * blk), grp * blk)
        lg_both = jnp.dot(kbf_ref[pl.ds(j0, grp * blk), :], qt_both, preferred_element_type=F32)
        out = []
        for h, sl in enumerate(head_lanes):
            m, l, acc = carry[3 * h:3 * h + 3]
            lg = lg_both[:, h * blk:(h + 1) * blk]
            lg = jnp.concatenate(
                [jnp.where(sel_ref[h, pl.ds(jj * grp + g, 1), :] > 0.0, lg[g * blk:(g + 1) * blk], NEG)
                 for g in range(grp)], axis=0)
            m_new = jnp.maximum(m, jnp.max(lg, axis=0, keepdims=True))
            alpha = jnp.exp(m - m_new)
            p = jnp.exp(lg - m_new)
            l = alpha * l + jnp.sum(p, axis=0, keepdims=True)
            acc = alpha * acc + jnp.dot(vt_ref[sl, pl.ds(j0, grp * blk)], p.astype(BF16),
                                        preferred_element_type=F32)
            out += [m_new, l, acc]
        return tuple(out)

    state = lax.fori_loop(0, (qi + grp - 1) // grp, body, tuple(state))
    for h, sl in enumerate(head_lanes):
        m, l, acc = state[3 * h:3 * h + 3]
        o_ref[:, sl] = (acc / l).T


def _moba_seq(proj, batch):
    m = proj.shape[0]
    seq = m // batch
    assert seq % (MOBA_BLOCK * MOBA_GROUP) == 0
    nb = seq // MOBA_BLOCK
    qoff, koff, voff = HEAD_TILES, 2 * HEAD_TILES, 3 * HEAD_TILES
    return pl.pallas_call(
        functools.partial(_moba_seq_kernel, nb=nb),
        grid=(batch, HEAD_TILES, nb),
        in_specs=[pl.BlockSpec((MOBA_BLOCK, LANE), lambda b, t, i: (b * nb + i, qoff + t)),
                  pl.BlockSpec((seq, LANE), lambda b, t, i: (b, koff + t)),
                  pl.BlockSpec((seq, LANE), lambda b, t, i: (b, voff + t))],
        out_specs=pl.BlockSpec((MOBA_BLOCK, LANE), lambda b, t, i: (b * nb + i, t)),
        out_shape=jax.ShapeDtypeStruct((m, D_HALF), F32),
        scratch_shapes=[pltpu.VMEM((seq, LANE), BF16), pltpu.VMEM((LANE, seq), BF16),
                        pltpu.VMEM((nb, LANE), F32), pltpu.VMEM((HEADS_PER_TILE, nb, MOBA_BLOCK), F32)],
        compiler_params=_cp("parallel", "parallel", "arbitrary"))(proj, proj, proj)


MOBA_PAGES_PER_STEP = 32
PAGES_PER_BLOCK = MOBA_BLOCK // PAGE_SIZE


def _moba_rank_kernel(pt_ref, q_ref, *rest, nblk):
    k_refs = rest[:MOBA_PAGES_PER_STEP]
    idx_ref, s_ref = rest[MOBA_PAGES_PER_STEP:]
    j = pl.program_id(1)
    bps = MOBA_PAGES_PER_STEP // PAGES_PER_BLOCK
    lane = lax.broadcasted_iota(jnp.int32, (HEADS, LANE), 1)

    @pl.when(j == 0)
    def _():
        s_ref[...] = jnp.zeros_like(s_ref)

    q = q_ref[...]
    acc = s_ref[...]
    for t in range(bps):
        per_token = sum(jnp.sum(k_refs[t * PAGES_PER_BLOCK + p][...] * q, axis=1)
                        for p in range(PAGES_PER_BLOCK))
        score = jnp.sum(per_token, axis=-1, keepdims=True) * (1.0 / MOBA_BLOCK)
        acc = acc + jnp.where(lane == j * bps + t, score, 0.0)
    s_ref[...] = acc

    @pl.when(j == pl.num_programs(1) - 1)
    def _():
        s = acc
        rank = jnp.zeros((HEADS, LANE), jnp.int32)
        for i in range(nblk):
            si = s[:, i:i + 1]
            beats = jnp.logical_or(si > s, jnp.logical_and(si == s, i < lane))
            rank = rank + beats.astype(jnp.int32)
        out = jnp.zeros((HEADS, LANE), jnp.int32)
        for slot in range(MOBA_TOPK):
            hit = jnp.logical_and(rank == slot, lane < nblk)
            idx = jnp.sum(jnp.where(hit, lane, 0), axis=-1, keepdims=True)
            out = jnp.where(lane == slot, idx, out)
        idx_ref[...] = out


def _moba_pick_kernel(pg_ref, q_ref, kn_ref, vn_ref, *rest):
    per_head = MOBA_TOPK * PAGES_PER_BLOCK
    n = HEADS * per_head
    k_refs, v_refs, o_ref = rest[:n], rest[n:2 * n], rest[2 * n]
    for h in range(HEADS):
        q = q_ref[h] * ATTN_SCALE
        lg_self = jnp.sum(q * kn_ref[h], axis=0, keepdims=True)
        pages = range(h * per_head, (h + 1) * per_head)
        lgs = [jnp.sum(k_refs[t][...] * q, axis=0, keepdims=True) for t in pages]
        m = lg_self
        for lg in lgs:
            m = jnp.maximum(m, jnp.max(lg, axis=-1, keepdims=True))
        p_self = jnp.exp(lg_self - m)
        l = p_self
        acc = p_self * vn_ref[h]
        for lg, t in zip(lgs, pages):
            p = jnp.exp(lg - m)
            l = l + jnp.sum(p, axis=-1, keepdims=True)
            acc = acc + jnp.sum(v_refs[t][...] * p, axis=-1, keepdims=True)
        o_ref[h] = acc / l


def _moba_step(proj, cache_kt, cache_vt, layer, page_table):
    bsz = proj.shape[0]
    n_pages = page_table.shape[1]
    assert n_pages % MOBA_PAGES_PER_STEP == 0 and PAGE_SIZE == LANE
    nblk = n_pages // PAGES_PER_BLOCK
    assert nblk <= LANE
    steps = n_pages // MOBA_PAGES_PER_STEP
    cols = proj.reshape(bsz, 4, HEADS, HEAD_DIM, 1)
    q_col, k_col, v_col = cols[:, 1], cols[:, 2], cols[:, 3]

    def page_spec(t):
        return pl.BlockSpec((None, None, HEADS, HEAD_DIM, PAGE_SIZE),
                            lambda b, j, pt: (layer, pt[b, j * MOBA_PAGES_PER_STEP + t], 0, 0, 0))

    idx = pl.pallas_call(
        functools.partial(_moba_rank_kernel, nblk=nblk),
        grid_spec=pltpu.PrefetchScalarGridSpec(
            num_scalar_prefetch=1,
            grid=(bsz, steps),
            in_specs=[pl.BlockSpec((None, HEADS, HEAD_DIM, 1), lambda b, j, pt: (b, 0, 0, 0))]
            + [page_spec(t) for t in range(MOBA_PAGES_PER_STEP)],
            out_specs=pl.BlockSpec((None, HEADS, LANE), lambda b, j, pt: (b, 0, 0)),
            scratch_shapes=[pltpu.VMEM((HEADS, LANE), F32)]),
        out_shape=jax.ShapeDtypeStruct((bsz, HEADS, LANE), jnp.int32),
        compiler_params=_cp("arbitrary", "arbitrary"),
    )(page_table, q_col, *([cache_kt] * MOBA_PAGES_PER_STEP))

    top = idx[:, :, :MOBA_TOPK]
    logical = top[..., None] * PAGES_PER_BLOCK + jnp.arange(PAGES_PER_BLOCK, dtype=jnp.int32)
    pages = jnp.take_along_axis(page_table, logical.reshape(bsz, -1), axis=1).reshape(-1)
    per_head = MOBA_TOPK * PAGES_PER_BLOCK

    def pick_spec(h, t):
        return pl.BlockSpec((None, None, None, HEAD_DIM, PAGE_SIZE),
                            lambda b, pg: (layer, pg[(b * HEADS + h) * per_head + t], h, 0, 0))

    tok_spec = pl.BlockSpec((None, HEADS, HEAD_DIM, 1), lambda b, pg: (b, 0, 0, 0))
    page_specs = [pick_spec(h, t) for h in range(HEADS) for t in range(per_head)]
    out = pl.pallas_call(
        _moba_pick_kernel,
        grid_spec=pltpu.PrefetchScalarGridSpec(
            num_scalar_prefetch=1,
            grid=(bsz,),
            in_specs=[tok_spec] * 3 + page_specs * 2,
            out_specs=tok_spec),
        out_shape=jax.ShapeDtypeStruct((bsz, HEADS, HEAD_DIM, 1), F32),
        compiler_params=_cp("arbitrary"),
    )(pages, q_col, k_col, v_col, *([cache_kt] * len(page_specs)), *([cache_vt] * len(page_specs)))
    return out.reshape(bsz, D_HALF)


def _seg_sum(x):
    seg = lax.broadcasted_iota(jnp.int32, (LANE, LANE), 0) // HEAD_DIM
    ones = jnp.where(seg == lax.broadcasted_iota(jnp.int32, (LANE, LANE), 1) // HEAD_DIM, 1.0, 0.0).astype(BF16)
    hi = x.astype(BF16)
    r1 = x - hi.astype(F32)
    mid = r1.astype(BF16)
    lo = (r1 - mid.astype(F32)).astype(BF16)
    parts = []
    for c in range(x.shape[1] // LANE):
        sl = slice(c * LANE, (c + 1) * LANE)
        d = lambda p: jnp.dot(p[:, sl], ones, preferred_element_type=F32)
        parts.append(d(hi) + (d(mid) + d(lo)))
    return jnp.concatenate(parts, axis=-1)


def _softplus(x):
    return jnp.maximum(x, 0.0) + jnp.log(1.0 + jnp.exp(-jnp.abs(x)))


def _rwkv_mix(cat, prev, p, vfirst):
    c = D_HALF
    dlt = prev - cat
    part = lambda t, j: t[:, j * c:(j + 1) * c]
    mu, mz = p['mu'][...], p['mz'][...]
    r = part(cat, 0) + part(dlt, 0) * mu[0:1]
    k = part(cat, 1) + part(dlt, 1) * mu[1:2]
    v = part(cat, 2) + part(dlt, 2) * mu[2:3]
    z0, dz = part(cat, 3), part(dlt, 3)
    zw, za, zg = z0 + dz * mz[0:1], z0 + dz * mz[1:2], z0 + dz * mz[2:3]
    w_raw = -_softplus(-(p['w0'][...] + _bdot(jnp.tanh(_bdot(zw, p['w1'][...])), p['w2'][...]))) - 0.5
    log_decay = -jnp.exp(w_raw)
    a = _sigmoid(p['a0'][...] + _bdot(_bdot(za, p['a1'][...]), p['a2'][...]))
    g = _bdot(_sigmoid(_bdot(zg, p['g1'][...])), p['g2'][...])
    if vfirst is not None:
        zv = z0 + dz * p['mu_v'][...]
        vmix = _sigmoid(p['v0'][...] + _bdot(_bdot(zv, p['v1'][...]), p['v2'][...]))
        v = v + (vfirst - v) * vmix
    kk = k * p['k_k'][...]
    kk = kk / jnp.maximum(jnp.sqrt(_seg_sum(kk * kk)), 1e-12)
    k = k * (1.0 + (a - 1.0) * p['k_a'][...])
    return r, log_decay, k, v, kk, kk * a, g


RWKV_PARAM_NAMES = ('mu', 'mz', 'w0', 'w1', 'w2', 'a0', 'a1', 'a2', 'g1', 'g2', 'k_k', 'k_a')
RWKV_VRES_NAMES = ('mu_v', 'v0', 'v1', 'v2')
RWKV_N_OUT = 7


def _rwkv_pre_kernel(*refs, names, has_vfirst, seq_tiles):
    refs = list(refs)
    cat_ref = refs.pop(0)
    prev_ref = refs.pop(0)
    vf_ref = refs.pop(0) if has_vfirst else None
    p = {n: refs.pop(0) for n in names}
    outs = refs
    cat = cat_ref[...]
    if seq_tiles:
        first = (pl.program_id(0) % seq_tiles) == 0
        row = lax.broadcasted_iota(jnp.int32, (cat.shape[0], 1), 0)
        before = jnp.where(first, 0.0, prev_ref[SUBLANE - 1:SUBLANE, :])
        prev = jnp.where(row == 0, before, pltpu.roll(cat, 1, axis=0))
    else:
        prev = prev_ref[...]
    res = _rwkv_mix(cat, prev, p, None if vf_ref is None else vf_ref[...])
    for o_ref, val in zip(outs, res):
        o_ref[...] = val


def _rwkv_pre(proj, prev, rp, vfirst, batch):
    m = proj.shape[0]
    cw = 4 * D_HALF
    names = RWKV_PARAM_NAMES + (RWKV_VRES_NAMES if vfirst is not None else ())
    params = [rp[n] for n in names]
    if prev is None:
        seq = m // batch
        tm = _row_tile(seq, 256)
        seq_tiles = seq // tm
        hb = tm // SUBLANE
        prev_arg = proj
        prev_spec = pl.BlockSpec((SUBLANE, cw), lambda i: (jnp.maximum(i * hb - 1, 0), 0))
    else:
        tm, seq_tiles = m, 0
        prev_arg = prev
        prev_spec = pl.BlockSpec((tm, cw), lambda i: (i, 0))
    args = [proj, prev_arg]
    in_specs = [pl.BlockSpec((tm, cw), lambda i: (i, 0)), prev_spec]
    if vfirst is not None:
        args.append(vfirst)
        in_specs.append(pl.BlockSpec((tm, D_HALF), lambda i: (i, 0)))
    args += params
    in_specs += [_full(a.shape) for a in params]
    return pl.pallas_call(
        functools.partial(_rwkv_pre_kernel, names=names, has_vfirst=vfirst is not None, seq_tiles=seq_tiles),
        grid=(m // tm,),
        in_specs=in_specs,
        out_specs=[pl.BlockSpec((tm, D_HALF), lambda i: (i, 0))] * RWKV_N_OUT,
        out_shape=[jax.ShapeDtypeStruct((m, D_HALF), F32)] * RWKV_N_OUT,
        compiler_params=_cp("parallel"))(*args)


WKV_CHUNK = 64


def _bmm(a, b):
    return lax.dot_general(a.astype(BF16), b.astype(BF16), (((2,), (1,)), ((0,), (0,))),
                           preferred_element_type=F32)


def _bmm3(a, b):
    ah = a.astype(BF16)
    al = (a - ah.astype(F32)).astype(BF16)
    bh = b.astype(BF16)
    bl = (b - bh.astype(F32)).astype(BF16)
    return _bmm(ah, bh) + (_bmm(ah, bl) + _bmm(al, bh))


def _wkv_tile(r, lw, k, v, kk, b, st):
    tt = r.shape[0]
    c, n, nh = WKV_CHUNK, HEAD_DIM, HEADS_PER_TILE
    nc = tt // c
    ri = lax.broadcasted_iota(jnp.int32, (1, c, c), 1)
    ci = lax.broadcasted_iota(jnp.int32, (1, c, c), 2)
    incl, strict = ri >= ci, ri > ci
    eye = jnp.where(ri == ci, 1.0, 0.0)
    tril = jnp.where(incl[0], 1.0, 0.0)
    lw3 = lw.reshape(nc, c, LANE)
    cum = jnp.stack([jnp.dot(tril, lw3[j], precision=HIGHEST, preferred_element_type=F32) for j in range(nc)])
    tot = cum[:, c - 1:c, :]
    e_neg = jnp.exp(-cum)
    e_end = jnp.exp(tot - cum)
    e_tot = jnp.exp(tot)

    def heads(x):
        return jnp.concatenate([x[:, :, h * n:(h + 1) * n] for h in range(nh)], axis=0)

    def chunks(x):
        return x.reshape(nc, c, LANE)

    kkm = heads(chunks(kk) * jnp.exp(cum - lw3))
    rp = heads(chunks(r) * jnp.exp(cum))
    bo, ko = heads(chunks(b) * e_neg), heads(chunks(k) * e_neg)
    bend, kend = heads(chunks(b) * e_end), heads(chunks(k) * e_end)
    vh = heads(chunks(v))
    e_tot = heads(e_tot)

    qa = lax.dot_general(jnp.concatenate([kkm, rp], axis=1).astype(BF16),
                         jnp.concatenate([bo, ko], axis=1).astype(BF16),
                         (((2,), (2,)), ((0,), (0,))), preferred_element_type=F32)
    a_ub = jnp.where(strict, qa[:, :c, :c], 0.0)
    a_vk = jnp.where(strict, qa[:, :c, c:], 0.0)
    a_rb = jnp.where(incl, qa[:, c:, :c], 0.0)
    a_rk = jnp.where(incl, qa[:, c:, c:], 0.0)

    m = 2
    t_inv = eye - jnp.where((ri // m) == (ci // m), a_ub, 0.0)
    while m < c:
        off = jnp.logical_and((ri // (2 * m)) == (ci // (2 * m)), (ri // m) != (ci // m))
        t_inv = t_inv - _bmm(_bmm(t_inv, jnp.where(off, a_ub, 0.0)), t_inv)
        m *= 2

    av = _bmm(jnp.concatenate([a_vk, a_rk], axis=1), vh)
    x1 = _bmm(t_inv, jnp.concatenate([kkm, av[:, :c]], axis=2))
    x2 = _bmm(a_rb, x1)
    x3 = _bmm(jnp.swapaxes(bend, 1, 2), x1)
    rr = rp - x2[:, :, :n]
    o_loc = av[:, c:] - x2[:, :, n:]
    trans = eye * e_tot - x3[:, :, :n]
    s_loc = _bmm(jnp.swapaxes(kend, 1, 2), vh) - x3[:, :, n:]

    by_head = lambda x: x.reshape(nh, nc, x.shape[1], x.shape[2])
    rr, o_loc, trans, s_loc = by_head(rr), by_head(o_loc), by_head(trans), by_head(s_loc)
    outs = []
    for j in range(nc):
        outs.append(_bmm(rr[:, j], st) + o_loc[:, j])
        st = _bmm3(trans[:, j], st) + s_loc[:, j]
    o = jnp.stack(outs, axis=1).reshape(nh, tt, n)
    return jnp.concatenate([o[h] for h in range(nh)], axis=-1), st


WKV_TIME_TILE = 1024


def _seg_mean(x):
    return _seg_sum(x) * (1.0 / HEAD_DIM)


def _wkv_seq_kernel(r_ref, lw_ref, k_ref, v_ref, kk_ref, b_ref, g_ref, rk_ref, lnw_ref, lnb_ref,
                    o_ref, so_ref, st_ref):
    i = pl.program_id(2)

    @pl.when(i == 0)
    def _():
        st_ref[...] = jnp.zeros_like(st_ref)

    r, k, v = r_ref[...], k_ref[...], v_ref[...]
    o, st = _wkv_tile(r, lw_ref[...], k, v, kk_ref[...], b_ref[...], st_ref[...])
    st_ref[...] = st
    mean = _seg_mean(o)
    var = _seg_mean(jnp.square(o - mean))
    on = (o - mean) * lax.rsqrt(var + RWKV_GN_EPS) * lnw_ref[...] + lnb_ref[...]
    bonus = _seg_sum(r * k * rk_ref[...]) * v
    o_ref[...] = (on + bonus) * g_ref[...]
    for h in range(HEADS_PER_TILE):
        so_ref[0, h] = st[h].T


def _wkv_seq(r, lw, k, v, kk, b, g, r_k, ln_w, ln_b, batch):
    m = r.shape[0]
    seq = m // batch
    tt = _row_tile(seq, WKV_TIME_TILE)
    assert tt % WKV_CHUNK == 0
    nt = seq // tt
    row_spec = pl.BlockSpec((tt, LANE), lambda bi, t, i: (bi * nt + i, t))
    par_spec = pl.BlockSpec((1, LANE), lambda bi, t, i: (0, t))
    return pl.pallas_call(
        _wkv_seq_kernel,
        grid=(batch, HEAD_TILES, nt),
        in_specs=[row_spec] * 7 + [par_spec] * 3,
        out_specs=[row_spec,
                   pl.BlockSpec((1, HEADS_PER_TILE, HEAD_DIM, HEAD_DIM), lambda bi, t, i: (bi, t, 0, 0))],
        out_shape=[jax.ShapeDtypeStruct((m, D_HALF), F32),
                   jax.ShapeDtypeStruct((batch, HEADS, HEAD_DIM, HEAD_DIM), F32)],
        scratch_shapes=[pltpu.VMEM((HEADS_PER_TILE, HEAD_DIM, HEAD_DIM), F32)],
        compiler_params=_cp("parallel", "parallel", "arbitrary"),
    )(r, lw, k, v, kk, b, g, r_k.reshape(1, -1), ln_w.reshape(1, -1), ln_b.reshape(1, -1))


def _wkv_finish(o, r, k, v, g, rk, lnw, lnb):
    mean = jnp.mean(o, axis=-1, keepdims=True)
    var = jnp.mean(jnp.square(o - mean), axis=-1, keepdims=True)
    on = (o - mean) * lax.rsqrt(var + RWKV_GN_EPS) * lnw + lnb
    bonus = jnp.sum(r * k * rk, axis=-1, keepdims=True) * v
    return (on + bonus) * g


def _wkv_step_kernel(s_ref, r_ref, lw_ref, k_ref, v_ref, kk_ref, b_ref, g_ref, rk_ref, lnw_ref, lnb_ref,
                     o_ref, so_ref):
    ri = lax.broadcasted_iota(jnp.int32, (HEAD_DIM, HEAD_DIM), 0)
    eye = jnp.where(ri == lax.broadcasted_iota(jnp.int32, (HEAD_DIM, HEAD_DIM), 1), 1.0, 0.0)
    outs = []
    for h in range(HEADS):
        sl = slice(h * HEAD_DIM, (h + 1) * HEAD_DIM)
        r, lw, k, v, kk, b, g = [ref[0][:, sl] for ref in (r_ref, lw_ref, k_ref, v_ref, kk_ref, b_ref, g_ref)]
        s = s_ref[0, h]
        sa = -jnp.sum(s * kk, axis=-1, keepdims=True)
        v_col = jnp.sum(eye * v, axis=-1, keepdims=True)
        s = s * jnp.exp(lw) + sa * b + v_col * k
        so_ref[0, h] = s
        o_col = jnp.sum(s * r, axis=-1, keepdims=True)
        o = jnp.sum(eye * o_col, axis=0, keepdims=True)
        outs.append(_wkv_finish(o, r, k, v, g, rk_ref[:, sl], lnw_ref[:, sl], lnb_ref[:, sl]))
    o_ref[0] = jnp.concatenate(outs, axis=-1)


def _wkv_step(state, r, lw, k, v, kk, b, g, r_k, ln_w, ln_b):
    bsz = r.shape[0]
    rows = [t.reshape(bsz, 1, D_HALF) for t in (r, lw, k, v, kk, b, g)]
    row_spec = pl.BlockSpec((1, 1, D_HALF), lambda i: (i, 0, 0))
    st_spec = pl.BlockSpec((1, HEADS, HEAD_DIM, HEAD_DIM), lambda i: (i, 0, 0, 0))
    out, st = pl.pallas_call(
        _wkv_step_kernel,
        grid=(bsz,),
        in_specs=[st_spec] + [row_spec] * 7 + [_full((1, D_HALF))] * 3,
        out_specs=[row_spec, st_spec],
        out_shape=[jax.ShapeDtypeStruct((bsz, 1, D_HALF), F32),
                   jax.ShapeDtypeStruct(state.shape, F32)],
        compiler_params=_cp("parallel"),
    )(state, *rows, r_k.reshape(1, -1), ln_w.reshape(1, -1), ln_b.reshape(1, -1))
    return out.reshape(bsz, D_HALF), st


CONV_HALO = 32


def _layernorm_silu(y, g, b):
    m = jnp.mean(y, axis=-1, keepdims=True)
    var = jnp.mean(jnp.square(y - m), axis=-1, keepdims=True)
    return _silu((y - m) * lax.rsqrt(var + LN_EPS) * g + b)


def _convd_seq_kernel(gv_ref, gg_ref, hv_ref, hg_ref, w_ref, b_ref, lng_ref, lnb_ref,
                      o_ref, tail_ref, ext_ref, *, seq_tiles):
    tt = gv_ref.shape[0]
    first = (pl.program_id(0) % seq_tiles) == 0
    ext_ref[0:CONV_HALO, :] = jnp.where(first, 0.0, hv_ref[...] * _sigmoid(hg_ref[...]))
    ext_ref[CONV_HALO:, :] = gv_ref[...] * _sigmoid(gg_ref[...])
    off = CONV_HALO - (CONV_D_WIDTH - 1)
    acc = jnp.zeros((tt, D_HALF), F32)
    for j in range(CONV_D_WIDTH):
        acc = acc + w_ref[j:j + 1, :] * ext_ref[off + j:off + j + tt, :]
    o_ref[...] = _layernorm_silu(acc + b_ref[...], lng_ref[...], lnb_ref[...])
    tail_ref[0] = ext_ref[tt:, :]


def _convd_seq(proj, batch, w, b, ln_g, ln_b):
    m = proj.shape[0]
    seq = m // batch
    tt = _row_tile(seq, 256)
    seq_tiles = seq // tt
    hb = tt // CONV_HALO
    vcol, gcol = 4, 5
    tile = lambda col: pl.BlockSpec((tt, D_HALF), lambda i: (i, col))
    halo = lambda col: pl.BlockSpec((CONV_HALO, D_HALF), lambda i: (jnp.maximum(i * hb - 1, 0), col))
    params = [w, b.reshape(1, -1), ln_g.reshape(1, -1), ln_b.reshape(1, -1)]
    out, tail = pl.pallas_call(
        functools.partial(_convd_seq_kernel, seq_tiles=seq_tiles),
        grid=(m // tt,),
        in_specs=[tile(vcol), tile(gcol), halo(vcol), halo(gcol)] + [_full(p.shape) for p in params],
        out_specs=[pl.BlockSpec((tt, D_HALF), lambda i: (i, 0)),
                   pl.BlockSpec((1, CONV_HALO, D_HALF), lambda i: (i, 0, 0))],
        out_shape=[jax.ShapeDtypeStruct((m, D_HALF), F32),
                   jax.ShapeDtypeStruct((m // tt, CONV_HALO, D_HALF), F32)],
        scratch_shapes=[pltpu.VMEM((tt + CONV_HALO, D_HALF), F32)],
        compiler_params=_cp("parallel"))(proj, proj, proj, proj, *params)
    return out, tail[seq_tiles - 1::seq_tiles, CONV_HALO - (CONV_D_WIDTH - 1):]


def _convd_step_kernel(gv_ref, gg_ref, buf_ref, w_ref, b_ref, lng_ref, lnb_ref, o_ref, u_ref):
    u = gv_ref[...] * _sigmoid(gg_ref[...])
    u_ref[...] = u
    acc = u * w_ref[CONV_D_WIDTH - 1:CONV_D_WIDTH, :]
    for j in range(CONV_D_WIDTH - 1):
        acc = acc + w_ref[j:j + 1, :] * buf_ref[j]
    o_ref[...] = _layernorm_silu(acc + b_ref[...], lng_ref[...], lnb_ref[...])


def _convd_step(proj, buf, w, b, ln_g, ln_b):
    m = proj.shape[0]
    buf_t = jnp.swapaxes(buf, 0, 1)
    params = [w, b.reshape(1, -1), ln_g.reshape(1, -1), ln_b.reshape(1, -1)]
    out, u = pl.pallas_call(
        _convd_step_kernel,
        grid=(1,),
        in_specs=[pl.BlockSpec((m, D_HALF), lambda i: (0, 4)), pl.BlockSpec((m, D_HALF), lambda i: (0, 5)),
                  _full(buf_t.shape)] + [_full(p.shape) for p in params],
        out_specs=[_full((m, D_HALF)), _full((m, D_HALF))],
        out_shape=[jax.ShapeDtypeStruct((m, D_HALF), F32)] * 2,
        compiler_params=_cp("arbitrary"))(proj, proj, buf_t, *params)
    return out, jnp.concatenate([buf[:, 1:], u[:, None]], axis=1)


def _rmsnorm_kernel(x_ref, g_ref, o_ref):
    x = x_ref[...]
    ms = jnp.mean(x * x, axis=-1, keepdims=True)
    o_ref[...] = x * lax.rsqrt(ms + RMS_EPS) * g_ref[...]


def _rmsnorm(x, g):
    m, d = x.shape
    tm = _row_tile(m, 512)
    return pl.pallas_call(
        _rmsnorm_kernel,
        grid=(m // tm,),
        in_specs=[pl.BlockSpec((tm, d), lambda i: (i, 0)), _full((1, d))],
        out_specs=pl.BlockSpec((tm, d), lambda i: (i, 0)),
        out_shape=jax.ShapeDtypeStruct((m, d), F32),
        compiler_params=_cp("parallel"))(x, g.reshape(1, d))


def _kv_layout_kernel(*refs, n_layers, paged):
    k_refs, v_refs = refs[:n_layers], refs[n_layers:2 * n_layers]
    ko_ref, vo_ref = refs[2 * n_layers:]
    layer = pl.program_id(0)

    def emit(src_ref, dst_ref):
        x = src_ref[...]
        if not paged:
            dst_ref[...] = x.reshape(dst_ref.shape)
            return
        for pg in range(dst_ref.shape[0]):
            rows = x[pg * PAGE_SIZE:(pg + 1) * PAGE_SIZE]
            for h in range(HEADS):
                dst_ref[pg, h] = rows[:, h * HEAD_DIM:(h + 1) * HEAD_DIM].T

    for l in range(n_layers):
        @pl.when(layer == l)
        def _():
            emit(k_refs[l], ko_ref)
            emit(v_refs[l], vo_ref)


def _kv_layout(projs, paged):
    n = len(projs)
    m = projs[0].shape[0]
    tm = _row_tile(m, 512)
    nt = m // tm

    def col_spec(l, col):
        return pl.BlockSpec((tm, D_HALF), lambda layer, i: (jnp.where(layer == l, i, 0), col))

    if paged:
        ppt = tm // PAGE_SIZE
        out_spec = pl.BlockSpec((ppt, HEADS, HEAD_DIM, PAGE_SIZE), lambda layer, i: (layer * nt + i, 0, 0, 0))
        shp = jax.ShapeDtypeStruct((n * m // PAGE_SIZE, HEADS, HEAD_DIM, PAGE_SIZE), F32)
    else:
        out_spec = pl.BlockSpec((tm, HEADS, HEAD_DIM), lambda layer, i: (layer * nt + i, 0, 0))
        shp = jax.ShapeDtypeStruct((n * m, HEADS, HEAD_DIM), F32)
    return pl.pallas_call(
        functools.partial(_kv_layout_kernel, n_layers=n, paged=paged),
        grid=(n, nt),
        in_specs=[col_spec(l, 2) for l in range(n)] + [col_spec(l, 3) for l in range(n)],
        out_specs=[out_spec, out_spec],
        out_shape=[shp, shp],
        compiler_params=_cp("parallel", "parallel"))(*projs, *projs)


def _rwkv_params(p, li):
    row = lambda t: t.reshape(1, -1)
    rp = dict(mu=p['rwkv_mu_rkv'][li], mz=p['rwkv_mu_z'][li], w0=row(p['rwkv_w0'][li]),
              w1=p['rwkv_w1'][li].astype(BF16), w2=p['rwkv_w2'][li].astype(BF16),
              a0=row(p['rwkv_a0'][li]), a1=p['rwkv_a1'][li].astype(BF16), a2=p['rwkv_a2'][li].astype(BF16),
              g1=p['rwkv_g1'][li].astype(BF16), g2=p['rwkv_g2'][li].astype(BF16),
              k_k=row(p['rwkv_k_k'][li]), k_a=row(p['rwkv_k_a'][li]))
    if li > 0:
        rp.update(mu_v=row(p['rwkv_mu_v'][li - 1]), v0=row(p['rwkv_v0'][li - 1]),
                  v1=p['rwkv_v1'][li - 1].astype(BF16), v2=p['rwkv_v2'][li - 1].astype(BF16))
    return rp


def _prepare(p):
    depth = p['g_mix'].shape[0]
    layers = []
    for i in range(depth):
        li = i // 2
        lay = dict(g_mix=p['g_mix'][i], g_ffn=p['g_ffn'][i],
                   w_up=p['ffn_w_up'][i].astype(BF16), w_down=p['ffn_w_down'][i].astype(BF16),
                   ffn_cw=p['ffn_conv_w'][i], ffn_cb=p['ffn_conv_b'][i])
        if i % 2 == 0:
            lay.update(w_in=p['w_in_ab'][li].astype(BF16), w_out=p['w_out_ab'][li].astype(BF16),
                       s5=_s5_params(p['s5_a_re'][li], p['s5_a_im'][li], p['s5_log_step'][li],
                                     p['s5_b_re'][li], p['s5_b_im'][li], p['s5_c_re'][li], p['s5_c_im'][li],
                                     p['s5_d'][li]),
                       w_glu=p['s5_w_glu'][li].astype(BF16), b_glu=p['s5_b_glu'][li])
        else:
            lay.update(w_in=p['w_in_cd'][li].astype(BF16), w_out=p['w_out_cd'][li].astype(BF16),
                       rwkv=_rwkv_params(p, li), r_k=p['rwkv_r_k'][li], ln_w=p['rwkv_ln_w'][li],
                       ln_b=p['rwkv_ln_b'][li], cd_w=p['conv_d_w'][li], cd_b=p['conv_d_b'][li],
                       cd_g=p['conv_d_ln_g'][li], cd_lb=p['conv_d_ln_b'][li])
        layers.append(lay)
    return layers


def _trunk_seq(x3, layers, g_final):
    batch, seq, d = x3.shape
    x = x3.reshape(batch * seq, d)
    rope = _rope_tables(seq, 0, True)
    attn_projs, s5r, s5i, wkvs, shifts, convs, ffns = [], [], [], [], [], [], []
    v_first = None
    for i, lay in enumerate(layers):
        if i % 2 == 0:
            proj = _norm_matmul(x, lay['g_mix'], lay['w_in'], rope=rope, rope_cols=(1, 2))
            a_out, hr, hi = _s5_seq(proj, batch, lay['s5'], lay['w_glu'], lay['b_glu'])
            b_out = _moba_seq(proj, batch)
            attn_projs.append(proj)
            s5r.append(hr)
            s5i.append(hi)
            x = _out_proj(a_out, b_out, lay['w_out'], x)
        else:
            proj = _norm_matmul(x, lay['g_mix'], lay['w_in'])
            r, lw, k, v, kk, b, g = _rwkv_pre(proj, None, lay['rwkv'], v_first, batch)
            if v_first is None:
                v_first = v
            c_out, s_fin = _wkv_seq(r, lw, k, v, kk, b, g, lay['r_k'], lay['ln_w'], lay['ln_b'], batch)
            d_out, cbuf = _convd_seq(proj, batch, lay['cd_w'], lay['cd_b'], lay['cd_g'], lay['cd_lb'])
            wkvs.append(s_fin)
            shifts.append(proj.reshape(batch, seq, -1)[:, -1, :4 * D_HALF])
            convs.append(cbuf)
            x = _out_proj(c_out, d_out, lay['w_out'], x)
        x, fbuf = _ffn_seq(x, batch, lay['g_ffn'], lay['w_up'], lay['ffn_cw'], lay['ffn_cb'], lay['w_down'])
        ffns.append(fbuf)
    y = _rmsnorm(x, g_final).reshape(batch, seq, d)
    assert seq % PAGE_SIZE == 0
    k_all, v_all = _kv_layout(attn_projs, paged=True)
    return (y, k_all, v_all, jnp.stack(s5r), jnp.stack(s5i), jnp.stack(wkvs),
            jnp.stack(shifts), jnp.stack(convs), jnp.stack(ffns))


def _trunk_step(x3, pos0, layers, g_final, cache_k, cache_v, page_table, s5_re0, s5_im0, wkv0, shift0,
                convd0, ffn0):
    bsz, _, d = x3.shape
    x = x3.reshape(bsz, d)
    rope = _rope_tables(bsz, pos0, False)
    cache_kt = jnp.transpose(cache_k, (0, 1, 3, 4, 2))
    cache_vt = jnp.transpose(cache_v, (0, 1, 3, 4, 2))
    attn_projs, s5r, s5i, wkvs, shifts, convs, ffns = [], [], [], [], [], [], []
    v_first = None
    for i, lay in enumerate(layers):
        li = i // 2
        if i % 2 == 0:
            proj = _norm_matmul(x, lay['g_mix'], lay['w_in'], rope=rope, rope_cols=(1, 2))
            a_out, hr, hi = _s5_step(proj, s5_re0[li].reshape(bsz, -1), s5_im0[li].reshape(bsz, -1),
                                     lay['s5'], lay['w_glu'], lay['b_glu'])
            b_out = _moba_step(proj, cache_kt, cache_vt, li, page_table)
            attn_projs.append(proj)
            s5r.append(hr)
            s5i.append(hi)
            x = _out_proj(a_out, b_out, lay['w_out'], x)
        else:
            proj = _norm_matmul(x, lay['g_mix'], lay['w_in'])
            r, lw, k, v, kk, b, g = _rwkv_pre(proj, shift0[li], lay['rwkv'], v_first, bsz)
            if v_first is None:
                v_first = v
            c_out, s_fin = _wkv_step(wkv0[li], r, lw, k, v, kk, b, g, lay['r_k'], lay['ln_w'], lay['ln_b'])
            d_out, cbuf = _convd_step(proj, convd0[li], lay['cd_w'], lay['cd_b'], lay['cd_g'], lay['cd_lb'])
            wkvs.append(s_fin)
            shifts.append(proj[:, :4 * D_HALF])
            convs.append(cbuf)
            x = _out_proj(c_out, d_out, lay['w_out'], x)
        x, fbuf = _ffn_step(x, ffn0[i], lay['g_ffn'], lay['w_up'], lay['ffn_cw'], lay['ffn_cb'], lay['w_down'])
        ffns.append(fbuf)
    y = _rmsnorm(x, g_final).reshape(bsz, 1, d)
    k_all, v_all = _kv_layout(attn_projs, paged=False)
    return (y, k_all, v_all, jnp.stack(s5r), jnp.stack(s5i), jnp.stack(wkvs),
            jnp.stack(shifts), jnp.stack(convs), jnp.stack(ffns))


def kernel(x_prompt, x_sample, cache_k_moba, cache_v_moba, page_table, state_s5_re, state_s5_im,
           state_rwkv_wkv, state_rwkv_shift, state_conv_d, state_ffn_conv, g_mix, g_ffn, g_final,
           w_in_ab, w_out_ab, s5_a_re, s5_a_im, s5_log_step, s5_b_re, s5_b_im, s5_c_re, s5_c_im, s5_d,
           s5_w_glu, s5_b_glu, w_in_cd, w_out_cd, rwkv_mu_rkv, rwkv_mu_z, rwkv_w0, rwkv_w1, rwkv_w2,
           rwkv_a0, rwkv_a1, rwkv_a2, rwkv_g1, rwkv_g2, rwkv_k_k, rwkv_k_a, rwkv_r_k, rwkv_ln_w, rwkv_ln_b,
           rwkv_mu_v, rwkv_v0, rwkv_v1, rwkv_v2, conv_d_w, conv_d_b, conv_d_ln_g, conv_d_ln_b,
           ffn_w_up, ffn_conv_w, ffn_conv_b, ffn_w_down):
    p = dict(g_mix=g_mix, g_ffn=g_ffn, w_in_ab=w_in_ab, w_out_ab=w_out_ab,
             s5_a_re=s5_a_re, s5_a_im=s5_a_im, s5_log_step=s5_log_step, s5_b_re=s5_b_re, s5_b_im=s5_b_im,
             s5_c_re=s5_c_re, s5_c_im=s5_c_im, s5_d=s5_d, s5_w_glu=s5_w_glu, s5_b_glu=s5_b_glu,
             w_in_cd=w_in_cd, w_out_cd=w_out_cd, rwkv_mu_rkv=rwkv_mu_rkv, rwkv_mu_z=rwkv_mu_z,
             rwkv_w0=rwkv_w0, rwkv_w1=rwkv_w1, rwkv_w2=rwkv_w2, rwkv_a0=rwkv_a0, rwkv_a1=rwkv_a1,
             rwkv_a2=rwkv_a2, rwkv_g1=rwkv_g1, rwkv_g2=rwkv_g2, rwkv_k_k=rwkv_k_k, rwkv_k_a=rwkv_k_a,
             rwkv_r_k=rwkv_r_k.reshape(rwkv_r_k.shape[0], -1), rwkv_ln_w=rwkv_ln_w, rwkv_ln_b=rwkv_ln_b,
             rwkv_mu_v=rwkv_mu_v, rwkv_v0=rwkv_v0, rwkv_v1=rwkv_v1, rwkv_v2=rwkv_v2,
             conv_d_w=conv_d_w, conv_d_b=conv_d_b, conv_d_ln_g=conv_d_ln_g, conv_d_ln_b=conv_d_ln_b,
             ffn_w_up=ffn_w_up, ffn_conv_w=ffn_conv_w, ffn_conv_b=ffn_conv_b, ffn_w_down=ffn_w_down)
    layers = _prepare(p)
    bp, seq, _ = x_prompt.shape
    bs = x_sample.shape[0]
    n_ab = w_in_ab.shape[0]
    n_cd = w_in_cd.shape[0]
    page = cache_k_moba.shape[2]
    past_len = page_table.shape[1] * page
    (y_p, k_p, v_p, s5r_p, s5i_p, wkv_p, sh_p, cd_p, ff_p) = _trunk_seq(x_prompt, layers, g_final)
    (y_s, k_s, v_s, s5r_s, s5i_s, wkv_s, sh_s, cd_s, ff_s) = _trunk_step(
        x_sample, past_len, layers, g_final, cache_k_moba, cache_v_moba, page_table, state_s5_re, state_s5_im,
        state_rwkv_wkv, state_rwkv_shift, state_conv_d, state_ffn_conv)
    kv_s = (n_ab, bs, 1, HEADS, HEAD_DIM)
    s5_p = (n_ab, bp, S5_GROUPS, S5_STATE)
    s5_s = (n_ab, bs, S5_GROUPS, S5_STATE)

    def pages_out(t):
        t = t.reshape(n_ab, bp, seq // page, HEADS, HEAD_DIM, page)
        return jnp.transpose(t, (0, 1, 2, 5, 3, 4))

    return (y_p, y_s, pages_out(k_p), pages_out(v_p), k_s.reshape(kv_s), v_s.reshape(kv_s),
            s5r_p.reshape(s5_p), s5i_p.reshape(s5_p), s5r_s.reshape(s5_s), s5i_s.reshape(s5_s),
            wkv_p, wkv_s, sh_p, sh_s, cd_p, cd_s, ff_p, ff_s)
```

```python
import functools
import math

import numpy as np
import jax
import jax.numpy as jnp
from jax import lax
from jax.experimental import pallas as pl
from jax.experimental.pallas import tpu as pltpu

F32 = jnp.float32
BF16 = jnp.bfloat16
HIGHEST = lax.Precision.HIGHEST

D_MODEL = 1024
D_HALF = 512
S5_GROUP = 16
S5_GROUPS = 32
S5_STATE = 64
S5_LANES = S5_GROUPS * S5_STATE
HEADS = 8
HEAD_DIM = 64
MOBA_BLOCK = 256
MOBA_TOPK = 3
PAGE_SIZE = 128
ROPE_DIM = 16
ROPE_THETA = 500000.0
RWKV_GN_EPS = 64e-5
CONV_D_WIDTH = 31
D_FF = 2816
RMS_EPS = 1e-6
LN_EPS = 1e-5
LANE = 128
SUBLANE = 8
VMEM_LIMIT = 48 * 1024 * 1024
NEG = -1e30


def _cp(*sem):
    return pltpu.CompilerParams(dimension_semantics=sem, vmem_limit_bytes=VMEM_LIMIT)


def _bdot(a, b):
    return jnp.dot(a.astype(BF16), b.astype(BF16), preferred_element_type=F32)


def _sigmoid(x):
    return 1.0 / (1.0 + jnp.exp(-x))


def _silu(x):
    return x * _sigmoid(x)


def _row_tile(m, pref):
    return pref if m % pref == 0 else m


def _rope_table_kernel(inv_ref, cos_ref, sa_ref, sb_ref, *, pos0, per_row_pos):
    rows = cos_ref.shape[0]
    lane = lax.broadcasted_iota(jnp.int32, (rows, LANE), 1) % HEAD_DIM
    if per_row_pos:
        pos = (pos0 + lax.broadcasted_iota(jnp.int32, (rows, LANE), 0)).astype(F32)
    else:
        pos = jnp.full((rows, LANE), pos0, F32)
    ang = pos * inv_ref[...]
    c, s = jnp.cos(ang), jnp.sin(ang)
    cos_ref[...] = jnp.where(lane < ROPE_DIM, c, 1.0)
    sa_ref[...] = jnp.where(lane < ROPE_DIM // 2, -s, 0.0)
    sb_ref[...] = jnp.where((lane >= ROPE_DIM // 2) & (lane < ROPE_DIM), s, 0.0)


def _rope_tables(rows, pos0, per_row_pos):
    half = ROPE_DIM // 2
    inv8 = (np.float32(1.0) / (np.float32(ROPE_THETA) ** (np.arange(half, dtype=np.float32) / np.float32(half))))
    inv = np.zeros((HEAD_DIM,), np.float32)
    inv[:half] = inv8
    inv[half:ROPE_DIM] = inv8
    inv = jnp.asarray(np.tile(inv, LANE // HEAD_DIM)[None, :])
    shp = jax.ShapeDtypeStruct((rows, LANE), F32)
    return pl.pallas_call(
        functools.partial(_rope_table_kernel, pos0=pos0, per_row_pos=per_row_pos),
        out_shape=(shp, shp, shp))(inv)


def _norm_matmul_kernel(x_ref, g_ref, w_ref, *rest, rope_cols):
    if rope_cols:
        cos_ref, sa_ref, sb_ref, o_ref = rest
    else:
        (o_ref,) = rest
    x = x_ref[...]
    ms = jnp.mean(x * x, axis=-1, keepdims=True)
    h = (x * lax.rsqrt(ms + RMS_EPS) * g_ref[...]).astype(BF16)
    tn = D_HALF
    for j in range(o_ref.shape[1] // tn):
        y = jnp.dot(h, w_ref[:, j * tn:(j + 1) * tn], preferred_element_type=F32)
        if j not in rope_cols:
            o_ref[:, j * tn:(j + 1) * tn] = y
            continue
        cos, sa, sb = cos_ref[...], sa_ref[...], sb_ref[...]
        for c in range(tn // LANE):
            yc = y[:, c * LANE:(c + 1) * LANE]
            o_ref[:, j * tn + c * LANE:j * tn + (c + 1) * LANE] = (
                yc * cos + pltpu.roll(yc, LANE - ROPE_DIM // 2, axis=1) * sa
                + pltpu.roll(yc, ROPE_DIM // 2, axis=1) * sb)


def _norm_matmul(x, g, w_bf, rope=None, rope_cols=()):
    m, d = x.shape
    n = w_bf.shape[1]
    tm = _row_tile(m, 512)
    in_specs = [pl.BlockSpec((tm, d), lambda i: (i, 0)),
                pl.BlockSpec((1, d), lambda i: (0, 0)),
                pl.BlockSpec((d, n), lambda i: (0, 0))]
    args = [x, g.reshape(1, d), w_bf]
    if rope_cols:
        nt = rope[0].shape[0] // tm
        for t in rope:
            in_specs.append(pl.BlockSpec((tm, LANE), lambda i: (i % nt, 0)))
            args.append(t)
    return pl.pallas_call(
        functools.partial(_norm_matmul_kernel, rope_cols=tuple(rope_cols)),
        grid=(m // tm,),
        in_specs=in_specs,
        out_specs=pl.BlockSpec((tm, n), lambda i: (i, 0)),
        out_shape=jax.ShapeDtypeStruct((m, n), F32),
        compiler_params=_cp("parallel"))(*args)


def _out_proj_kernel(a_ref, b_ref, wa_ref, wb_ref, res_ref, o_ref):
    o_ref[...] = res_ref[...] + (_bdot(a_ref[...], wa_ref[...]) + _bdot(b_ref[...], wb_ref[...]))


def _out_proj(a, b, w_bf, res):
    m, c = a.shape
    n = w_bf.shape[1]
    tm = _row_tile(m, 512)
    return pl.pallas_call(
        _out_proj_kernel,
        grid=(m // tm,),
        in_specs=[pl.BlockSpec((tm, c), lambda i: (i, 0)),
                  pl.BlockSpec((tm, c), lambda i: (i, 0)),
                  pl.BlockSpec((c, n), lambda i: (0, 0)),
                  pl.BlockSpec((c, n), lambda i: (1, 0)),
                  pl.BlockSpec((tm, n), lambda i: (i, 0))],
        out_specs=pl.BlockSpec((tm, n), lambda i: (i, 0)),
        out_shape=jax.ShapeDtypeStruct((m, n), F32),
        compiler_params=_cp("parallel"))(a, b, w_bf, w_bf, res)


FFN_TN = 256
FFN_TM = 1024
FFN_HALO = SUBLANE


def _ffn_seq_kernel(x_ref, xh_ref, g_ref, wa_ref, wb_ref, cwa_ref, cwb_ref, cba_ref, cbb_ref, wd_ref,
                    o_ref, ua_ref, ub_ref, h_ref, acc_ref, *, tiles_per_seq):
    i, c = pl.program_id(0), pl.program_id(1)
    tm = x_ref.shape[0]
    first = (i % tiles_per_seq) == 0

    @pl.when(c == 0)
    def _():
        def norm(x):
            ms = jnp.mean(x * x, axis=-1, keepdims=True)
            return (x * lax.rsqrt(ms + RMS_EPS) * g_ref[...]).astype(BF16)
        h_ref[0:FFN_HALO, :] = jnp.where(first, jnp.zeros((), BF16), norm(xh_ref[...]))
        h_ref[FFN_HALO:, :] = norm(x_ref[...])
        acc_ref[...] = jnp.zeros_like(acc_ref)

    h = h_ref[...]

    def branch(w_ref, cw_ref, cb_ref, tail_ref):
        u = jnp.dot(h, w_ref[...], preferred_element_type=F32)
        tail_ref[0] = u[tm + FFN_HALO - 2:, :]
        cw = cw_ref[...]
        y = (u * cw[2:3, :] + pltpu.roll(u, 1, axis=0) * cw[1:2, :]
             + pltpu.roll(u, 2, axis=0) * cw[0:1, :])
        return y[FFN_HALO:, :] + cb_ref[...]

    a = branch(wa_ref, cwa_ref, cba_ref, ua_ref)
    b = branch(wb_ref, cwb_ref, cbb_ref, ub_ref)
    acc_ref[...] += _bdot(_silu(a) * b, wd_ref[...])

    @pl.when(c == pl.num_programs(1) - 1)
    def _():
        o_ref[...] = x_ref[...] + acc_ref[...]


def _ffn_seq(x, batch, g, w_up_bf, cw, cb, w_down_bf):
    m, d = x.shape
    seq = m // batch
    tm = _row_tile(seq, FFN_TM)
    tps = seq // tm
    nc = D_FF // FFN_TN
    hb = tm // FFN_HALO
    out, ua, ub = pl.pallas_call(
        functools.partial(_ffn_seq_kernel, tiles_per_seq=tps),
        grid=(m // tm, nc),
        in_specs=[pl.BlockSpec((tm, d), lambda i, c: (i, 0)),
                  pl.BlockSpec((FFN_HALO, d), lambda i, c: (jnp.maximum(i * hb - 1, 0), 0)),
                  pl.BlockSpec((1, d), lambda i, c: (0, 0)),
                  pl.BlockSpec((d, FFN_TN), lambda i, c: (0, c)),
                  pl.BlockSpec((d, FFN_TN), lambda i, c: (0, nc + c)),
                  pl.BlockSpec((3, FFN_TN), lambda i, c: (0, c)),
                  pl.BlockSpec((3, FFN_TN), lambda i, c: (0, nc + c)),
                  pl.BlockSpec((1, FFN_TN), lambda i, c: (0, c)),
                  pl.BlockSpec((1, FFN_TN), lambda i, c: (0, nc + c)),
                  pl.BlockSpec((FFN_TN, d), lambda i, c: (c, 0))],
        out_specs=[pl.BlockSpec((tm, d), lambda i, c: (i, 0)),
                   pl.BlockSpec((1, 2, FFN_TN), lambda i, c: (i, 0, c)),
                   pl.BlockSpec((1, 2, FFN_TN), lambda i, c: (i, 0, c))],
        out_shape=[jax.ShapeDtypeStruct((m, d), F32),
                   jax.ShapeDtypeStruct((m // tm, 2, D_FF), F32),
                   jax.ShapeDtypeStruct((m // tm, 2, D_FF), F32)],
        scratch_shapes=[pltpu.VMEM((tm + FFN_HALO, d), BF16), pltpu.VMEM((tm, d), F32)],
        compiler_params=_cp("arbitrary", "arbitrary"),
    )(x, x, g.reshape(1, d), w_up_bf, w_up_bf, cw, cw, cb.reshape(1, -1), cb.reshape(1, -1), w_down_bf)
    return out, jnp.concatenate([ua[tps - 1::tps], ub[tps - 1::tps]], axis=-1)


def _ffn_step_kernel(x_ref, g_ref, wa_ref, wb_ref, cwa_ref, cwb_ref, cba_ref, cbb_ref, wd_ref,
                     bufa_ref, bufb_ref, o_ref, ua_ref, ub_ref, h_ref, acc_ref):
    c = pl.program_id(0)

    @pl.when(c == 0)
    def _():
        x = x_ref[...]
        ms = jnp.mean(x * x, axis=-1, keepdims=True)
        h_ref[...] = (x * lax.rsqrt(ms + RMS_EPS) * g_ref[...]).astype(BF16)
        acc_ref[...] = jnp.zeros_like(acc_ref)

    h = h_ref[...]

    def branch(w_ref, cw_ref, cb_ref, buf_ref, u_ref):
        u = jnp.dot(h, w_ref[...], preferred_element_type=F32)
        u_ref[...] = u
        cw = cw_ref[...]
        return u * cw[2:3, :] + buf_ref[1] * cw[1:2, :] + buf_ref[0] * cw[0:1, :] + cb_ref[...]

    a = branch(wa_ref, cwa_ref, cba_ref, bufa_ref, ua_ref)
    b = branch(wb_ref, cwb_ref, cbb_ref, bufb_ref, ub_ref)
    acc_ref[...] += _bdot(_silu(a) * b, wd_ref[...])

    @pl.when(c == pl.num_programs(0) - 1)
    def _():
        o_ref[...] = x_ref[...] + acc_ref[...]


def _ffn_step(x, buf, g, w_up_bf, cw, cb, w_down_bf):
    m, d = x.shape
    nc = D_FF // FFN_TN
    buf_t = jnp.swapaxes(buf, 0, 1)
    out, ua, ub = pl.pallas_call(
        _ffn_step_kernel,
        grid=(nc,),
        in_specs=[pl.BlockSpec((m, d), lambda c: (0, 0)),
                  pl.BlockSpec((1, d), lambda c: (0, 0)),
                  pl.BlockSpec((d, FFN_TN), lambda c: (0, c)),
                  pl.BlockSpec((d, FFN_TN), lambda c: (0, nc + c)),
                  pl.BlockSpec((3, FFN_TN), lambda c: (0, c)),
                  pl.BlockSpec((3, FFN_TN), lambda c: (0, nc + c)),
                  pl.BlockSpec((1, FFN_TN), lambda c: (0, c)),
                  pl.BlockSpec((1, FFN_TN), lambda c: (0, nc + c)),
                  pl.BlockSpec((FFN_TN, d), lambda c: (c, 0)),
                  pl.BlockSpec((2, m, FFN_TN), lambda c: (0, 0, c)),
                  pl.BlockSpec((2, m, FFN_TN), lambda c: (0, 0, nc + c))],
        out_specs=[pl.BlockSpec((m, d), lambda c: (0, 0)),
                   pl.BlockSpec((m, FFN_TN), lambda c: (0, c)),
                   pl.BlockSpec((m, FFN_TN), lambda c: (0, c))],
        out_shape=[jax.ShapeDtypeStruct((m, d), F32),
                   jax.ShapeDtypeStruct((m, D_FF), F32),
                   jax.ShapeDtypeStruct((m, D_FF), F32)],
        scratch_shapes=[pltpu.VMEM((m, d), BF16), pltpu.VMEM((m, d), F32)],
        compiler_params=_cp("arbitrary"),
    )(x, g.reshape(1, d), w_up_bf, w_up_bf, cw, cw, cb.reshape(1, -1), cb.reshape(1, -1), w_down_bf,
      buf_t, buf_t)
    u = jnp.concatenate([ua, ub], axis=-1)
    return out, jnp.stack([buf[:, 1], u], axis=1)


S5_STRIP = 512
S5_NSTRIP = S5_LANES // S5_STRIP
S5_TIME_TILE = 512


def _s5_param_kernel(are_ref, aim_ref, ls_ref, brt_ref, bit_ref, abr_ref, abi_ref, bbr_ref, bbi_ref):
    ar, ai = are_ref[...], aim_ref[...]
    dt = jnp.exp(ls_ref[...])
    mag = jnp.exp(ar * dt)
    abr, abi = mag * jnp.cos(ai * dt), mag * jnp.sin(ai * dt)
    inv_abs2 = 1.0 / (ar * ar + ai * ai)
    cr = ((abr - 1.0) * ar + abi * ai) * inv_abs2
    ci = (abi * ar - (abr - 1.0) * ai) * inv_abs2
    br, bi = brt_ref[...], bit_ref[...]
    abr_ref[...] = abr
    abi_ref[...] = abi
    bbr_ref[...] = cr * br - ci * bi
    bbi_ref[...] = cr * bi + ci * br


def _s5_params(a_re, a_im, log_step, b_re, b_im, c_re, c_im, d_skip):
    g, n, k = b_re.shape
    rep = lambda t: jnp.repeat(t, k, axis=0)
    brt = b_re.transpose(0, 2, 1).reshape(g * k, n)
    bit = b_im.transpose(0, 2, 1).reshape(g * k, n)
    shp = jax.ShapeDtypeStruct((g * k, n), F32)
    abr, abi, bbr, bbi = pl.pallas_call(_s5_param_kernel, out_shape=(shp, shp, shp, shp))(
        rep(a_re), rep(a_im), rep(log_step[:, None]), brt, bit)
    abr = abr[::k].reshape(1, g * n)
    abi = abi[::k].reshape(1, g * n)
    gs = S5_STRIP // n
    eye = jnp.eye(gs, dtype=F32)

    def in_blocks(t):
        t = t.reshape(S5_NSTRIP, gs, k, n)
        return jnp.einsum('cgkn,gh->cgkhn', t, eye).reshape(S5_NSTRIP, gs * k, gs * n).astype(BF16)

    def out_blocks(t):
        t = t.reshape(S5_NSTRIP, gs, k, n)
        return jnp.einsum('cgkn,gh->cgnhk', t, eye).reshape(S5_NSTRIP, gs * n, gs * k).astype(BF16)

    return dict(abr=abr, abi=abi, wbr=in_blocks(bbr), wbi=in_blocks(bbi),
                wcr=out_blocks(c_re), wci=out_blocks(c_im), d=d_skip.reshape(1, g * k))


def _s5_input(u, wbr_ref, wbi_ref, c):
    cw = S5_STRIP // (S5_STATE // S5_GROUP)
    uc = u[:, c * cw:(c + 1) * cw].astype(BF16)
    return (jnp.dot(uc, wbr_ref[c], preferred_element_type=F32),
            jnp.dot(uc, wbi_ref[c], preferred_element_type=F32))


def _s5_output(u, h_strip, wcr_ref, wci_ref, d_ref, wg_ref, bg_ref):
    ys = []
    for c in range(S5_NSTRIP):
        hr, hi = h_strip(c)
        ys.append(jnp.dot(hr.astype(BF16), wcr_ref[c], preferred_element_type=F32)
                  - jnp.dot(hi.astype(BF16), wci_ref[c], preferred_element_type=F32))
    y = jnp.concatenate(ys, axis=-1) + d_ref[...] * u
    yg = 0.5 * y * (1.0 + jnp.tanh(math.sqrt(2.0 / math.pi) * (y + 0.044715 * (y * y * y))))
    z = jnp.dot(yg.astype(BF16), wg_ref[...], preferred_element_type=F32) + bg_ref[...]
    return yg * _sigmoid(z)


def _cmul(ar, ai, br, bi):
    return ar * br - ai * bi, ar * bi + ai * br


def _s5_seq_kernel(u_ref, abr_ref, abi_ref, wbr_ref, wbi_ref, wcr_ref, wci_ref, d_ref, wg_ref, bg_ref,
                   o_ref, hro_ref, hio_ref, hr_ref, hi_ref, cr_ref, ci_ref, tab_ref):
    i = pl.program_id(1)
    tt = u_ref.shape[0]

    @pl.when(i == 0)
    def _():
        cr_ref[...] = jnp.zeros_like(cr_ref)
        ci_ref[...] = jnp.zeros_like(ci_ref)
        row = lax.broadcasted_iota(jnp.int32, (SUBLANE, S5_LANES), 0)
        p_r, p_i = [abr_ref[...]], [abi_ref[...]]
        for _ in range(SUBLANE - 1):
            nr, ni = _cmul(p_r[-1], p_i[-1], p_r[0], p_i[0])
            p_r.append(nr)
            p_i.append(ni)
        for k, s in enumerate((1, 2, 4)):
            tab_ref[2 * k] = jnp.where(row >= s, p_r[s - 1], 0.0)
            tab_ref[2 * k + 1] = jnp.where(row >= s, p_i[s - 1], 0.0)
        car_r = jnp.zeros((SUBLANE, S5_LANES), F32)
        car_i = jnp.zeros((SUBLANE, S5_LANES), F32)
        for j in range(SUBLANE):
            car_r = jnp.where(row == j, p_r[j], car_r)
            car_i = jnp.where(row == j, p_i[j], car_i)
        tab_ref[6] = car_r
        tab_ref[7] = car_i

    u = u_ref[...]
    for c in range(S5_NSTRIP):
        br, bi = _s5_input(u, wbr_ref, wbi_ref, c)
        hr_ref[:, c * S5_STRIP:(c + 1) * S5_STRIP] = br
        hi_ref[:, c * S5_STRIP:(c + 1) * S5_STRIP] = bi

    def block(r, carry):
        r8 = pl.multiple_of(r * SUBLANE, SUBLANE)
        for c in range(S5_NSTRIP):
            sl = slice(c * S5_STRIP, (c + 1) * S5_STRIP)
            xr, xi = hr_ref[pl.ds(r8, SUBLANE), sl], hi_ref[pl.ds(r8, SUBLANE), sl]
            for k, s in enumerate((1, 2, 4)):
                dr, di = _cmul(tab_ref[2 * k, :, sl], tab_ref[2 * k + 1, :, sl],
                               pltpu.roll(xr, s, axis=0), pltpu.roll(xi, s, axis=0))
                xr, xi = xr + dr, xi + di
            dr, di = _cmul(tab_ref[6, :, sl], tab_ref[7, :, sl], cr_ref[:, sl], ci_ref[:, sl])
            xr, xi = xr + dr, xi + di
            hr_ref[pl.ds(r8, SUBLANE), sl] = xr
            hi_ref[pl.ds(r8, SUBLANE), sl] = xi
            cr_ref[:, sl] = xr[SUBLANE - 1:, :]
            ci_ref[:, sl] = xi[SUBLANE - 1:, :]
        return carry

    lax.fori_loop(0, tt // SUBLANE, block, 0)

    def h_strip(c):
        sl = slice(c * S5_STRIP, (c + 1) * S5_STRIP)
        return hr_ref[:, sl], hi_ref[:, sl]

    o_ref[...] = _s5_output(u, h_strip, wcr_ref, wci_ref, d_ref, wg_ref, bg_ref)
    hro_ref[0] = cr_ref[...]
    hio_ref[0] = ci_ref[...]


def _full(shape):
    nd = len(shape)
    return pl.BlockSpec(shape, lambda *_: (0,) * nd)


def _s5_seq(proj, batch, sp, w_glu_bf, b_glu):
    m = proj.shape[0]
    seq = m // batch
    tt = _row_tile(seq, S5_TIME_TILE)
    nt = seq // tt
    params = [sp['abr'], sp['abi'], sp['wbr'], sp['wbi'], sp['wcr'], sp['wci'], sp['d'],
              w_glu_bf, b_glu.reshape(1, -1)]
    out, hr, hi = pl.pallas_call(
        _s5_seq_kernel,
        grid=(batch, nt),
        in_specs=[pl.BlockSpec((tt, D_HALF), lambda b, i: (b * nt + i, 0))] + [_full(p.shape) for p in params],
        out_specs=[pl.BlockSpec((tt, D_HALF), lambda b, i: (b * nt + i, 0)),
                   pl.BlockSpec((1, 1, S5_LANES), lambda b, i: (b, 0, 0)),
                   pl.BlockSpec((1, 1, S5_LANES), lambda b, i: (b, 0, 0))],
        out_shape=[jax.ShapeDtypeStruct((m, D_HALF), F32),
                   jax.ShapeDtypeStruct((batch, 1, S5_LANES), F32),
                   jax.ShapeDtypeStruct((batch, 1, S5_LANES), F32)],
        scratch_shapes=[pltpu.VMEM((tt, S5_LANES), F32), pltpu.VMEM((tt, S5_LANES), F32),
                        pltpu.VMEM((1, S5_LANES), F32), pltpu.VMEM((1, S5_LANES), F32),
                        pltpu.VMEM((8, SUBLANE, S5_LANES), F32)],
        compiler_params=_cp("arbitrary", "arbitrary"))(proj, *params)
    return out, hr[:, 0], hi[:, 0]


def _s5_step_kernel(u_ref, h0r_ref, h0i_ref, abr_ref, abi_ref, wbr_ref, wbi_ref, wcr_ref, wci_ref, d_ref,
                    wg_ref, bg_ref, o_ref, hro_ref, hio_ref):
    u = u_ref[...]
    for c in range(S5_NSTRIP):
        sl = slice(c * S5_STRIP, (c + 1) * S5_STRIP)
        br, bi = _s5_input(u, wbr_ref, wbi_ref, c)
        dr, di = _cmul(abr_ref[:, sl], abi_ref[:, sl], h0r_ref[:, sl], h0i_ref[:, sl])
        hro_ref[:, sl] = dr + br
        hio_ref[:, sl] = di + bi

    def h_strip(c):
        sl = slice(c * S5_STRIP, (c + 1) * S5_STRIP)
        return hro_ref[:, sl], hio_ref[:, sl]

    o_ref[...] = _s5_output(u, h_strip, wcr_ref, wci_ref, d_ref, wg_ref, bg_ref)


def _s5_step(proj, h0r, h0i, sp, w_glu_bf, b_glu):
    m = proj.shape[0]
    args = [h0r, h0i, sp['abr'], sp['abi'], sp['wbr'], sp['wbi'], sp['wcr'], sp['wci'], sp['d'],
            w_glu_bf, b_glu.reshape(1, -1)]
    return pl.pallas_call(
        _s5_step_kernel,
        grid=(1,),
        in_specs=[pl.BlockSpec((m, D_HALF), lambda i: (0, 0))] + [_full(a.shape) for a in args],
        out_specs=[_full((m, D_HALF)), _full((m, S5_LANES)), _full((m, S5_LANES))],
        out_shape=[jax.ShapeDtypeStruct((m, D_HALF), F32),
                   jax.ShapeDtypeStruct((m, S5_LANES), F32),
                   jax.ShapeDtypeStruct((m, S5_LANES), F32)],
        compiler_params=_cp("arbitrary"))(proj, *args)


HEADS_PER_TILE = LANE // HEAD_DIM
HEAD_TILES = HEADS // HEADS_PER_TILE
ATTN_SCALE = HEAD_DIM ** -0.5
NT_DIMS = (((1,), (1,)), ((), ()))
MOBA_GROUP = 4


def _moba_seq_kernel(q_ref, k_ref, v_ref, o_ref, kbf_ref, vt_ref, kmean_ref, sel_ref, *, nb):
    qi = pl.program_id(2)
    blk = MOBA_BLOCK

    @pl.when(qi == 0)
    def _():
        kbf_ref[...] = k_ref[...].astype(BF16)
        vt_ref[...] = v_ref[...].T.astype(BF16)
        for j in range(nb):
            kmean_ref[j:j + 1, :] = jnp.mean(k_ref[j * blk:(j + 1) * blk, :], axis=0, keepdims=True)

    q = q_ref[...]
    row = lax.broadcasted_iota(jnp.int32, (nb, blk), 0)
    causal = (lax.broadcasted_iota(jnp.int32, (blk, blk), 0) <= lax.broadcasted_iota(jnp.int32, (blk, blk), 1))
    q0 = pl.multiple_of(qi * blk, blk)
    head_lanes = [slice(h * HEAD_DIM, (h + 1) * HEAD_DIM) for h in range(HEADS_PER_TILE)]
    qts, state = [], []
    for h, sl in enumerate(head_lanes):
        qh = q[:, sl]
        s = lax.dot_general(kmean_ref[:, sl], qh, NT_DIMS, precision=HIGHEST, preferred_element_type=F32)
        rank = jnp.zeros((nb, blk), F32)
        for i in range(nb):
            si = s[i:i + 1, :]
            beats = jnp.logical_or(si > s, jnp.logical_and(si == s, i < row))
            rank = rank + jnp.where(jnp.logical_and(beats, i < qi), 1.0, 0.0)
        sel_ref[h] = jnp.where(jnp.logical_and(row < qi, rank < MOBA_TOPK), 1.0, 0.0)

        qt = (qh * ATTN_SCALE).T.astype(BF16)
        lg = jnp.dot(kbf_ref[pl.ds(q0, blk), sl], qt, preferred_element_type=F32)
        lg = jnp.where(causal, lg, NEG)
        m = jnp.max(lg, axis=0, keepdims=True)
        p = jnp.exp(lg - m)
        l = jnp.sum(p, axis=0, keepdims=True)
        acc = jnp.dot(vt_ref[sl, pl.ds(q0, blk)], p.astype(BF16), preferred_element_type=F32)
        qts.append(qt)
        state += [m, l, acc]

    grp = MOBA_GROUP
    zero = jnp.zeros_like(qts[0])
    qt_both = jnp.concatenate([jnp.concatenate([qts[0], zero], axis=1),
                               jnp.concatenate([zero, qts[1]], axis=1)], axis=0)

    def body(jj, carry):
        j0 = pl.multiple_of(jj * (grp * blk), grp * blk)
        lg_both = jnp.dot(kbf_ref[pl.ds(j0, grp * blk), :], qt_both, preferred_element_type=F32)
        out = []
        for h, sl in enumerate(head_lanes):
            m, l, acc = carry[3 * h:3 * h + 3]
            lg = lg_both[:, h * blk:(h + 1) * blk]
            lg = jnp.concatenate(
                [jnp.where(sel_ref[h, pl.ds(jj * grp + g, 1), :] > 0.0, lg[g * blk:(g + 1) * blk], NEG)
                 for g in range(grp)], axis=0)
            m_new = jnp.maximum(m, jnp.max(lg, axis=0, keepdims=True))
            alpha = jnp.exp(m - m_new)
            p = jnp.exp(lg - m_new)
            l = alpha * l + jnp.sum(p, axis=0, keepdims=True)
            acc = alpha * acc + jnp.dot(vt_ref[sl, pl.ds(j0, grp * blk)], p.astype(BF16),
                                        preferred_element_type=F32)
            out += [m_new, l, acc]
        return tuple(out)

    state = lax.fori_loop(0, (qi + grp - 1) // grp, body, tuple(state))
    for h, sl in enumerate(head_lanes):
        m, l, acc = state[3 * h:3 * h + 3]
        o_ref[:, sl] = (acc / l).T


def _moba_seq(proj, batch):
    m = proj.shape[0]
    seq = m // batch
    assert seq % (MOBA_BLOCK * MOBA_GROUP) == 0
    nb = seq // MOBA_BLOCK
    qoff, koff, voff = HEAD_TILES, 2 * HEAD_TILES, 3 * HEAD_TILES
    return pl.pallas_call(
        functools.partial(_moba_seq_kernel, nb=nb),
        grid=(batch, HEAD_TILES, nb),
        in_specs=[pl.BlockSpec((MOBA_BLOCK, LANE), lambda b, t, i: (b * nb + i, qoff + t)),
                  pl.BlockSpec((seq, LANE), lambda b, t, i: (b, koff + t)),
                  pl.BlockSpec((seq, LANE), lambda b, t, i: (b, voff + t))],
        out_specs=pl.BlockSpec((MOBA_BLOCK, LANE), lambda b, t, i: (b * nb + i, t)),
        out_shape=jax.ShapeDtypeStruct((m, D_HALF), F32),
        scratch_shapes=[pltpu.VMEM((seq, LANE), BF16), pltpu.VMEM((LANE, seq), BF16),
                        pltpu.VMEM((nb, LANE), F32), pltpu.VMEM((HEADS_PER_TILE, nb, MOBA_BLOCK), F32)],
        compiler_params=_cp("parallel", "parallel", "arbitrary"))(proj, proj, proj)


MOBA_PAGES_PER_STEP = 32
PAGES_PER_BLOCK = MOBA_BLOCK // PAGE_SIZE


def _moba_rank_kernel(pt_ref, q_ref, *rest, nblk):
    k_refs = rest[:MOBA_PAGES_PER_STEP]
    idx_ref, s_ref = rest[MOBA_PAGES_PER_STEP:]
    j = pl.program_id(1)
    bps = MOBA_PAGES_PER_STEP // PAGES_PER_BLOCK
    lane = lax.broadcasted_iota(jnp.int32, (HEADS, LANE), 1)

    @pl.when(j == 0)
    def _():
        s_ref[...] = jnp.zeros_like(s_ref)

    q = q_ref[...]
    acc = s_ref[...]
    for t in range(bps):
        per_token = sum(jnp.sum(k_refs[t * PAGES_PER_BLOCK + p][...] * q, axis=1)
                        for p in range(PAGES_PER_BLOCK))
        score = jnp.sum(per_token, axis=-1, keepdims=True) * (1.0 / MOBA_BLOCK)
        acc = acc + jnp.where(lane == j * bps + t, score, 0.0)
    s_ref[...] = acc

    @pl.when(j == pl.num_programs(1) - 1)
    def _():
        s = acc
        rank = jnp.zeros((HEADS, LANE), jnp.int32)
        for i in range(nblk):
            si = s[:, i:i + 1]
            beats = jnp.logical_or(si > s, jnp.logical_and(si == s, i < lane))
            rank = rank + beats.astype(jnp.int32)
        out = jnp.zeros((HEADS, LANE), jnp.int32)
        for slot in range(MOBA_TOPK):
            hit = jnp.logical_and(rank == slot, lane < nblk)
            idx = jnp.sum(jnp.where(hit, lane, 0), axis=-1, keepdims=True)
            out = jnp.where(lane == slot, idx, out)
        idx_ref[...] = out


def _moba_pick_kernel(pg_ref, q_ref, kn_ref, vn_ref, *rest):
    per_head = MOBA_TOPK * PAGES_PER_BLOCK
    n = HEADS * per_head
    k_refs, v_refs, o_ref = rest[:n], rest[n:2 * n], rest[2 * n]
    for h in range(HEADS):
        q = q_ref[h] * ATTN_SCALE
        lg_self = jnp.sum(q * kn_ref[h], axis=0, keepdims=True)
        pages = range(h * per_head, (h + 1) * per_head)
        lgs = [jnp.sum(k_refs[t][...] * q, axis=0, keepdims=True) for t in pages]
        m = lg_self
        for lg in lgs:
            m = jnp.maximum(m, jnp.max(lg, axis=-1, keepdims=True))
        p_self = jnp.exp(lg_self - m)
        l = p_self
        acc = p_self * vn_ref[h]
        for lg, t in zip(lgs, pages):
            p = jnp.exp(lg - m)
            l = l + jnp.sum(p, axis=-1, keepdims=True)
            acc = acc + jnp.sum(v_refs[t][...] * p, axis=-1, keepdims=True)
        o_ref[h] = acc / l


def _moba_step(proj, cache_kt, cache_vt, layer, page_table):
    bsz = proj.shape[0]
    n_pages = page_table.shape[1]
    assert n_pages % MOBA_PAGES_PER_STEP == 0 and PAGE_SIZE == LANE
    nblk = n_pages // PAGES_PER_BLOCK
    assert nblk <= LANE
    steps = n_pages // MOBA_PAGES_PER_STEP
    cols = proj.reshape(bsz, 4, HEADS, HEAD_DIM, 1)
    q_col, k_col, v_col = cols[:, 1], cols[:, 2], cols[:, 3]

    def page_spec(t):
        return pl.BlockSpec((None, None, HEADS, HEAD_DIM, PAGE_SIZE),
                            lambda b, j, pt: (layer, pt[b, j * MOBA_PAGES_PER_STEP + t], 0, 0, 0))

    idx = pl.pallas_call(
        functools.partial(_moba_rank_kernel, nblk=nblk),
        grid_spec=pltpu.PrefetchScalarGridSpec(
            num_scalar_prefetch=1,
            grid=(bsz, steps),
            in_specs=[pl.BlockSpec((None, HEADS, HEAD_DIM, 1), lambda b, j, pt: (b, 0, 0, 0))]
            + [page_spec(t) for t in range(MOBA_PAGES_PER_STEP)],
            out_specs=pl.BlockSpec((None, HEADS, LANE), lambda b, j, pt: (b, 0, 0)),
            scratch_shapes=[pltpu.VMEM((HEADS, LANE), F32)]),
        out_shape=jax.ShapeDtypeStruct((bsz, HEADS, LANE), jnp.int32),
        compiler_params=_cp("arbitrary", "arbitrary"),
    )(page_table, q_col, *([cache_kt] * MOBA_PAGES_PER_STEP))

    top = idx[:, :, :MOBA_TOPK]
    logical = top[..., None] * PAGES_PER_BLOCK + jnp.arange(PAGES_PER_BLOCK, dtype=jnp.int32)
    pages = jnp.take_along_axis(page_table, logical.reshape(bsz, -1), axis=1).reshape(-1)
    per_head = MOBA_TOPK * PAGES_PER_BLOCK

    def pick_spec(h, t):
        return pl.BlockSpec((None, None, None, HEAD_DIM, PAGE_SIZE),
                            lambda b, pg: (layer, pg[(b * HEADS + h) * per_head + t], h, 0, 0))

    tok_spec = pl.BlockSpec((None, HEADS, HEAD_DIM, 1), lambda b, pg: (b, 0, 0, 0))
    page_specs = [pick_spec(h, t) for h in range(HEADS) for t in range(per_head)]
    out = pl.pallas_call(
        _moba_pick_kernel,
        grid_spec=pltpu.PrefetchScalarGridSpec(
            num_scalar_prefetch=1,
            grid=(bsz,),
            in_specs=[tok_spec] * 3 + page_specs * 2,
            out_specs=tok_spec),
        out_shape=jax.ShapeDtypeStruct((bsz, HEADS, HEAD_DIM, 1), F32),
        compiler_params=_cp("arbitrary"),
    )(pages, q_col, k_col, v_col, *([cache_kt] * len(page_specs)), *([cache_vt] * len(page_specs)))
    return out.reshape(bsz, D_HALF)


def _seg_sum(x):
    seg = lax.broadcasted_iota(jnp.int32, (LANE, LANE), 0) // HEAD_DIM
    ones = jnp.where(seg == lax.broadcasted_iota(jnp.int32, (LANE, LANE), 1) // HEAD_DIM, 1.0, 0.0).astype(BF16)
    hi = x.astype(BF16)
    r1 = x - hi.astype(F32)
    mid = r1.astype(BF16)
    lo = (r1 - mid.astype(F32)).astype(BF16)
    parts = []
    for c in range(x.shape[1] // LANE):
        sl = slice(c * LANE, (c + 1) * LANE)
        d = lambda p: jnp.dot(p[:, sl], ones, preferred_element_type=F32)
        parts.append(d(hi) + (d(mid) + d(lo)))
    return jnp.concatenate(parts, axis=-1)


def _softplus(x):
    return jnp.maximum(x, 0.0) + jnp.log(1.0 + jnp.exp(-jnp.abs(x)))


def _rwkv_mix(cat, prev, p, vfirst):
    c = D_HALF
    dlt = prev - cat
    part = lambda t, j: t[:, j * c:(j + 1) * c]
    mu, mz = p['mu'][...], p['mz'][...]
    r = part(cat, 0) + part(dlt, 0) * mu[0:1]
    k = part(cat, 1) + part(dlt, 1) * mu[1:2]
    v = part(cat, 2) + part(dlt, 2) * mu[2:3]
    z0, dz = part(cat, 3), part(dlt, 3)
    zw, za, zg = z0 + dz * mz[0:1], z0 + dz * mz[1:2], z0 + dz * mz[2:3]
    w_raw = -_softplus(-(p['w0'][...] + _bdot(jnp.tanh(_bdot(zw, p['w1'][...])), p['w2'][...]))) - 0.5
    log_decay = -jnp.exp(w_raw)
    a = _sigmoid(p['a0'][...] + _bdot(_bdot(za, p['a1'][...]), p['a2'][...]))
    g = _bdot(_sigmoid(_bdot(zg, p['g1'][...])), p['g2'][...])
    if vfirst is not None:
        zv = z0 + dz * p['mu_v'][...]
        vmix = _sigmoid(p['v0'][...] + _bdot(_bdot(zv, p['v1'][...]), p['v2'][...]))
        v = v + (vfirst - v) * vmix
    kk = k * p['k_k'][...]
    kk = kk / jnp.maximum(jnp.sqrt(_seg_sum(kk * kk)), 1e-12)
    k = k * (1.0 + (a - 1.0) * p['k_a'][...])
    return r, log_decay, k, v, kk, kk * a, g


RWKV_PARAM_NAMES = ('mu', 'mz', 'w0', 'w1', 'w2', 'a0', 'a1', 'a2', 'g1', 'g2', 'k_k', 'k_a')
RWKV_VRES_NAMES = ('mu_v', 'v0', 'v1', 'v2')
RWKV_N_OUT = 7


def _rwkv_pre_kernel(*refs, names, has_vfirst, seq_tiles):
    refs = list(refs)
    cat_ref = refs.pop(0)
    prev_ref = refs.pop(0)
    vf_ref = refs.pop(0) if has_vfirst else None
    p = {n: refs.pop(0) for n in names}
    outs = refs
    cat = cat_ref[...]
    if seq_tiles:
        first = (pl.program_id(0) % seq_tiles) == 0
        row = lax.broadcasted_iota(jnp.int32, (cat.shape[0], 1), 0)
        before = jnp.where(first, 0.0, prev_ref[SUBLANE - 1:SUBLANE, :])
        prev = jnp.where(row == 0, before, pltpu.roll(cat, 1, axis=0))
    else:
        prev = prev_ref[...]
    res = _rwkv_mix(cat, prev, p, None if vf_ref is None else vf_ref[...])
    for o_ref, val in zip(outs, res):
        o_ref[...] = val


def _rwkv_pre(proj, prev, rp, vfirst, batch):
    m = proj.shape[0]
    cw = 4 * D_HALF
    names = RWKV_PARAM_NAMES + (RWKV_VRES_NAMES if vfirst is not None else ())
    params = [rp[n] for n in names]
    if prev is None:
        seq = m // batch
        tm = _row_tile(seq, 256)
        seq_tiles = seq // tm
        hb = tm // SUBLANE
        prev_arg = proj
        prev_spec = pl.BlockSpec((SUBLANE, cw), lambda i: (jnp.maximum(i * hb - 1, 0), 0))
    else:
        tm, seq_tiles = m, 0
        prev_arg = prev
        prev_spec = pl.BlockSpec((tm, cw), lambda i: (i, 0))
    args = [proj, prev_arg]
    in_specs = [pl.BlockSpec((tm, cw), lambda i: (i, 0)), prev_spec]
    if vfirst is not None:
        args.append(vfirst)
        in_specs.append(pl.BlockSpec((tm, D_HALF), lambda i: (i, 0)))
    args += params
    in_specs += [_full(a.shape) for a in params]
    return pl.pallas_call(
        functools.partial(_rwkv_pre_kernel, names=names, has_vfirst=vfirst is not None, seq_tiles=seq_tiles),
        grid=(m // tm,),
        in_specs=in_specs,
        out_specs=[pl.BlockSpec((tm, D_HALF), lambda i: (i, 0))] * RWKV_N_OUT,
        out_shape=[jax.ShapeDtypeStruct((m, D_HALF), F32)] * RWKV_N_OUT,
        compiler_params=_cp("parallel"))(*args)


WKV_CHUNK = 64


def _bmm(a, b):
    return lax.dot_general(a.astype(BF16), b.astype(BF16), (((2,), (1,)), ((0,), (0,))),
                           preferred_element_type=F32)


def _bmm3(a, b):
    ah = a.astype(BF16)
    al = (a - ah.astype(F32)).astype(BF16)
    bh = b.astype(BF16)
    bl = (b - bh.astype(F32)).astype(BF16)
    return _bmm(ah, bh) + (_bmm(ah, bl) + _bmm(al, bh))


def _wkv_tile(r, lw, k, v, kk, b, st):
    tt = r.shape[0]
    c, n, nh = WKV_CHUNK, HEAD_DIM, HEADS_PER_TILE
    nc = tt // c
    ri = lax.broadcasted_iota(jnp.int32, (1, c, c), 1)
    ci = lax.broadcasted_iota(jnp.int32, (1, c, c), 2)
    incl, strict = ri >= ci, ri > ci
    eye = jnp.where(ri == ci, 1.0, 0.0)
    tril = jnp.where(incl[0], 1.0, 0.0)
    lw3 = lw.reshape(nc, c, LANE)
    cum = jnp.stack([jnp.dot(tril, lw3[j], precision=HIGHEST, preferred_element_type=F32) for j in range(nc)])
    tot = cum[:, c - 1:c, :]
    e_neg = jnp.exp(-cum)
    e_end = jnp.exp(tot - cum)
    e_tot = jnp.exp(tot)

    def heads(x):
        return jnp.concatenate([x[:, :, h * n:(h + 1) * n] for h in range(nh)], axis=0)

    def chunks(x):
        return x.reshape(nc, c, LANE)

    kkm = heads(chunks(kk) * jnp.exp(cum - lw3))
    rp = heads(chunks(r) * jnp.exp(cum))
    bo, ko = heads(chunks(b) * e_neg), heads(chunks(k) * e_neg)
    bend, kend = heads(chunks(b) * e_end), heads(chunks(k) * e_end)
    vh = heads(chunks(v))
    e_tot = heads(e_tot)

    qa = lax.dot_general(jnp.concatenate([kkm, rp], axis=1).astype(BF16),
                         jnp.concatenate([bo, ko], axis=1).astype(BF16),
                         (((2,), (2,)), ((0,), (0,))), preferred_element_type=F32)
    a_ub = jnp.where(strict, qa[:, :c, :c], 0.0)
    a_vk = jnp.where(strict, qa[:, :c, c:], 0.0)
    a_rb = jnp.where(incl, qa[:, c:, :c], 0.0)
    a_rk = jnp.where(incl, qa[:, c:, c:], 0.0)

    m = 2
    t_inv = eye - jnp.where((ri // m) == (ci // m), a_ub, 0.0)
    while m < c:
        off = jnp.logical_and((ri // (2 * m)) == (ci // (2 * m)), (ri // m) != (ci // m))
        t_inv = t_inv - _bmm(_bmm(t_inv, jnp.where(off, a_ub, 0.0)), t_inv)
        m *= 2

    av = _bmm(jnp.concatenate([a_vk, a_rk], axis=1), vh)
    x1 = _bmm(t_inv, jnp.concatenate([kkm, av[:, :c]], axis=2))
    x2 = _bmm(a_rb, x1)
    x3 = _bmm(jnp.swapaxes(bend, 1, 2), x1)
    rr = rp - x2[:, :, :n]
    o_loc = av[:, c:] - x2[:, :, n:]
    trans = eye * e_tot - x3[:, :, :n]
    s_loc = _bmm(jnp.swapaxes(kend, 1, 2), vh) - x3[:, :, n:]

    assert nc % 2 == 0
    npair = nc // 2
    pairs = lambda x: x.reshape(nh, npair, 2, x.shape[1], x.shape[2])
    flat = lambda x: x.reshape(nh * npair, x.shape[2], x.shape[3])
    trans, s_loc = pairs(trans), pairs(s_loc)
    g_even, g_odd = flat(trans[:, :, 0]), flat(trans[:, :, 1])
    l_even, l_odd = flat(s_loc[:, :, 0]), flat(s_loc[:, :, 1])
    g_pair = _bmm3(g_odd, g_even).reshape(nh, npair, n, n)
    l_pair = (_bmm3(g_odd, l_even) + l_odd).reshape(nh, npair, n, n)
    starts = []
    for i in range(npair):
        starts.append(st)
        st = _bmm3(g_pair[:, i], st) + l_pair[:, i]
    st_even = jnp.stack(starts, axis=1)
    st_odd = (_bmm3(g_even, flat(st_even)) + l_even).reshape(nh, npair, n, n)
    st_all = jnp.stack([st_even, st_odd], axis=2).reshape(nh * nc, n, n)
    o = (_bmm(rr, st_all) + o_loc).reshape(nh, tt, n)
    return jnp.concatenate([o[h] for h in range(nh)], axis=-1), st


WKV_TIME_TILE = 1024


def _seg_mean(x):
    return _seg_sum(x) * (1.0 / HEAD_DIM)


def _wkv_seq_kernel(r_ref, lw_ref, k_ref, v_ref, kk_ref, b_ref, g_ref, rk_ref, lnw_ref, lnb_ref,
                    o_ref, so_ref, st_ref):
    i = pl.program_id(2)

    @pl.when(i == 0)
    def _():
        st_ref[...] = jnp.zeros_like(st_ref)

    r, k, v = r_ref[...], k_ref[...], v_ref[...]
    o, st = _wkv_tile(r, lw_ref[...], k, v, kk_ref[...], b_ref[...], st_ref[...])
    st_ref[...] = st
    mean = _seg_mean(o)
    var = _seg_mean(jnp.square(o - mean))
    on = (o - mean) * lax.rsqrt(var + RWKV_GN_EPS) * lnw_ref[...] + lnb_ref[...]
    bonus = _seg_sum(r * k * rk_ref[...]) * v
    o_ref[...] = (on + bonus) * g_ref[...]
    for h in range(HEADS_PER_TILE):
        so_ref[0, h] = st[h].T


def _wkv_seq(r, lw, k, v, kk, b, g, r_k, ln_w, ln_b, batch):
    m = r.shape[0]
    seq = m // batch
    tt = _row_tile(seq, WKV_TIME_TILE)
    assert tt % WKV_CHUNK == 0
    nt = seq // tt
    row_spec = pl.BlockSpec((tt, LANE), lambda bi, t, i: (bi * nt + i, t))
    par_spec = pl.BlockSpec((1, LANE), lambda bi, t, i: (0, t))
    return pl.pallas_call(
        _wkv_seq_kernel,
        grid=(batch, HEAD_TILES, nt),
        in_specs=[row_spec] * 7 + [par_spec] * 3,
        out_specs=[row_spec,
                   pl.BlockSpec((1, HEADS_PER_TILE, HEAD_DIM, HEAD_DIM), lambda bi, t, i: (bi, t, 0, 0))],
        out_shape=[jax.ShapeDtypeStruct((m, D_HALF), F32),
                   jax.ShapeDtypeStruct((batch, HEADS, HEAD_DIM, HEAD_DIM), F32)],
        scratch_shapes=[pltpu.VMEM((HEADS_PER_TILE, HEAD_DIM, HEAD_DIM), F32)],
        compiler_params=_cp("parallel", "parallel", "arbitrary"),
    )(r, lw, k, v, kk, b, g, r_k.reshape(1, -1), ln_w.reshape(1, -1), ln_b.reshape(1, -1))


def _wkv_finish(o, r, k, v, g, rk, lnw, lnb):
    mean = jnp.mean(o, axis=-1, keepdims=True)
    var = jnp.mean(jnp.square(o - mean), axis=-1, keepdims=True)
    on = (o - mean) * lax.rsqrt(var + RWKV_GN_EPS) * lnw + lnb
    bonus = jnp.sum(r * k * rk, axis=-1, keepdims=True) * v
    return (on + bonus) * g


def _wkv_step_kernel(s_ref, r_ref, lw_ref, k_ref, v_ref, kk_ref, b_ref, g_ref, rk_ref, lnw_ref, lnb_ref,
                     o_ref, so_ref):
    ri = lax.broadcasted_iota(jnp.int32, (HEAD_DIM, HEAD_DIM), 0)
    eye = jnp.where(ri == lax.broadcasted_iota(jnp.int32, (HEAD_DIM, HEAD_DIM), 1), 1.0, 0.0)
    outs = []
    for h in range(HEADS):
        sl = slice(h * HEAD_DIM, (h + 1) * HEAD_DIM)
        r, lw, k, v, kk, b, g = [ref[0][:, sl] for ref in (r_ref, lw_ref, k_ref, v_ref, kk_ref, b_ref, g_ref)]
        s = s_ref[0, h]
        sa = -jnp.sum(s * kk, axis=-1, keepdims=True)
        v_col = jnp.sum(eye * v, axis=-1, keepdims=True)
        s = s * jnp.exp(lw) + sa * b + v_col * k
        so_ref[0, h] = s
        o_col = jnp.sum(s * r, axis=-1, keepdims=True)
        o = jnp.sum(eye * o_col, axis=0, keepdims=True)
        outs.append(_wkv_finish(o, r, k, v, g, rk_ref[:, sl], lnw_ref[:, sl], lnb_ref[:, sl]))
    o_ref[0] = jnp.concatenate(outs, axis=-1)


def _wkv_step(state, r, lw, k, v, kk, b, g, r_k, ln_w, ln_b):
    bsz = r.shape[0]
    rows = [t.reshape(bsz, 1, D_HALF) for t in (r, lw, k, v, kk, b, g)]
    row_spec = pl.BlockSpec((1, 1, D_HALF), lambda i: (i, 0, 0))
    st_spec = pl.BlockSpec((1, HEADS, HEAD_DIM, HEAD_DIM), lambda i: (i, 0, 0, 0))
    out, st = pl.pallas_call(
        _wkv_step_kernel,
        grid=(bsz,),
        in_specs=[st_spec] + [row_spec] * 7 + [_full((1, D_HALF))] * 3,
        out_specs=[row_spec, st_spec],
        out_shape=[jax.ShapeDtypeStruct((bsz, 1, D_HALF), F32),
                   jax.ShapeDtypeStruct(state.shape, F32)],
        compiler_params=_cp("parallel"),
    )(state, *rows, r_k.reshape(1, -1), ln_w.reshape(1, -1), ln_b.reshape(1, -1))
    return out.reshape(bsz, D_HALF), st


CONV_HALO = 32


def _layernorm_silu(y, g, b):
    m = jnp.mean(y, axis=-1, keepdims=True)
    var = jnp.mean(jnp.square(y - m), axis=-1, keepdims=True)
    return _silu((y - m) * lax.rsqrt(var + LN_EPS) * g + b)


def _convd_seq_kernel(gv_ref, gg_ref, hv_ref, hg_ref, w_ref, b_ref, lng_ref, lnb_ref,
                      o_ref, tail_ref, ext_ref, *, seq_tiles):
    tt = gv_ref.shape[0]
    first = (pl.program_id(0) % seq_tiles) == 0
    ext_ref[0:CONV_HALO, :] = jnp.where(first, 0.0, hv_ref[...] * _sigmoid(hg_ref[...]))
    ext_ref[CONV_HALO:, :] = gv_ref[...] * _sigmoid(gg_ref[...])
    off = CONV_HALO - (CONV_D_WIDTH - 1)
    acc = jnp.zeros((tt, D_HALF), F32)
    for j in range(CONV_D_WIDTH):
        acc = acc + w_ref[j:j + 1, :] * ext_ref[off + j:off + j + tt, :]
    o_ref[...] = _layernorm_silu(acc + b_ref[...], lng_ref[...], lnb_ref[...])
    tail_ref[0] = ext_ref[tt:, :]


def _convd_seq(proj, batch, w, b, ln_g, ln_b):
    m = proj.shape[0]
    seq = m // batch
    tt = _row_tile(seq, 256)
    seq_tiles = seq // tt
    hb = tt // CONV_HALO
    vcol, gcol = 4, 5
    tile = lambda col: pl.BlockSpec((tt, D_HALF), lambda i: (i, col))
    halo = lambda col: pl.BlockSpec((CONV_HALO, D_HALF), lambda i: (jnp.maximum(i * hb - 1, 0), col))
    params = [w, b.reshape(1, -1), ln_g.reshape(1, -1), ln_b.reshape(1, -1)]
    out, tail = pl.pallas_call(
        functools.partial(_convd_seq_kernel, seq_tiles=seq_tiles),
        grid=(m // tt,),
        in_specs=[tile(vcol), tile(gcol), halo(vcol), halo(gcol)] + [_full(p.shape) for p in params],
        out_specs=[pl.BlockSpec((tt, D_HALF), lambda i: (i, 0)),
                   pl.BlockSpec((1, CONV_HALO, D_HALF), lambda i: (i, 0, 0))],
        out_shape=[jax.ShapeDtypeStruct((m, D_HALF), F32),
                   jax.ShapeDtypeStruct((m // tt, CONV_HALO, D_HALF), F32)],
        scratch_shapes=[pltpu.VMEM((tt + CONV_HALO, D_HALF), F32)],
        compiler_params=_cp("parallel"))(proj, proj, proj, proj, *params)
    return out, tail[seq_tiles - 1::seq_tiles, CONV_HALO - (CONV_D_WIDTH - 1):]


def _convd_step_kernel(gv_ref, gg_ref, buf_ref, w_ref, b_ref, lng_ref, lnb_ref, o_ref, u_ref):
    u = gv_ref[...] * _sigmoid(gg_ref[...])
    u_ref[...] = u
    acc = u * w_ref[CONV_D_WIDTH - 1:CONV_D_WIDTH, :]
    for j in range(CONV_D_WIDTH - 1):
        acc = acc + w_ref[j:j + 1, :] * buf_ref[j]
    o_ref[...] = _layernorm_silu(acc + b_ref[...], lng_ref[...], lnb_ref[...])


def _convd_step(proj, buf, w, b, ln_g, ln_b):
    m = proj.shape[0]
    buf_t = jnp.swapaxes(buf, 0, 1)
    params = [w, b.reshape(1, -1), ln_g.reshape(1, -1), ln_b.reshape(1, -1)]
    out, u = pl.pallas_call(
        _convd_step_kernel,
        grid=(1,),
        in_specs=[pl.BlockSpec((m, D_HALF), lambda i: (0, 4)), pl.BlockSpec((m, D_HALF), lambda i: (0, 5)),
                  _full(buf_t.shape)] + [_full(p.shape) for p in params],
        out_specs=[_full((m, D_HALF)), _full((m, D_HALF))],
        out_shape=[jax.ShapeDtypeStruct((m, D_HALF), F32)] * 2,
        compiler_params=_cp("arbitrary"))(proj, proj, buf_t, *params)
    return out, jnp.concatenate([buf[:, 1:], u[:, None]], axis=1)


def _rmsnorm_kernel(x_ref, g_ref, o_ref):
    x = x_ref[...]
    ms = jnp.mean(x * x, axis=-1, keepdims=True)
    o_ref[...] = x * lax.rsqrt(ms + RMS_EPS) * g_ref[...]


def _rmsnorm(x, g):
    m, d = x.shape
    tm = _row_tile(m, 512)
    return pl.pallas_call(
        _rmsnorm_kernel,
        grid=(m // tm,),
        in_specs=[pl.BlockSpec((tm, d), lambda i: (i, 0)), _full((1, d))],
        out_specs=pl.BlockSpec((tm, d), lambda i: (i, 0)),
        out_shape=jax.ShapeDtypeStruct((m, d), F32),
        compiler_params=_cp("parallel"))(x, g.reshape(1, d))


def _kv_layout_kernel(*refs, n_layers, paged):
    k_refs, v_refs = refs[:n_layers], refs[n_layers:2 * n_layers]
    ko_ref, vo_ref = refs[2 * n_layers:]
    layer = pl.program_id(0)

    def emit(src_ref, dst_ref):
        x = src_ref[...]
        if not paged:
            dst_ref[...] = x.reshape(dst_ref.shape)
            return
        for pg in range(dst_ref.shape[0]):
            rows = x[pg * PAGE_SIZE:(pg + 1) * PAGE_SIZE]
            for h in range(HEADS):
                dst_ref[pg, h] = rows[:, h * HEAD_DIM:(h + 1) * HEAD_DIM].T

    for l in range(n_layers):
        @pl.when(layer == l)
        def _():
            emit(k_refs[l], ko_ref)
            emit(v_refs[l], vo_ref)


def _kv_layout(projs, paged):
    n = len(projs)
    m = projs[0].shape[0]
    tm = _row_tile(m, 512)
    nt = m // tm

    def col_spec(l, col):
        return pl.BlockSpec((tm, D_HALF), lambda layer, i: (jnp.where(layer == l, i, 0), col))

    if paged:
        ppt = tm // PAGE_SIZE
        out_spec = pl.BlockSpec((ppt, HEADS, HEAD_DIM, PAGE_SIZE), lambda layer, i: (layer * nt + i, 0, 0, 0))
        shp = jax.ShapeDtypeStruct((n * m // PAGE_SIZE, HEADS, HEAD_DIM, PAGE_SIZE), F32)
    else:
        out_spec = pl.BlockSpec((tm, HEADS, HEAD_DIM), lambda layer, i: (layer * nt + i, 0, 0))
        shp = jax.ShapeDtypeStruct((n * m, HEADS, HEAD_DIM), F32)
    return pl.pallas_call(
        functools.partial(_kv_layout_kernel, n_layers=n, paged=paged),
        grid=(n, nt),
        in_specs=[col_spec(l, 2) for l in range(n)] + [col_spec(l, 3) for l in range(n)],
        out_specs=[out_spec, out_spec],
        out_shape=[shp, shp],
        compiler_params=_cp("parallel", "parallel"))(*projs, *projs)


def _rwkv_params(p, li):
    row = lambda t: t.reshape(1, -1)
    rp = dict(mu=p['rwkv_mu_rkv'][li], mz=p['rwkv_mu_z'][li], w0=row(p['rwkv_w0'][li]),
              w1=p['rwkv_w1'][li].astype(BF16), w2=p['rwkv_w2'][li].astype(BF16),
              a0=row(p['rwkv_a0'][li]), a1=p['rwkv_a1'][li].astype(BF16), a2=p['rwkv_a2'][li].astype(BF16),
              g1=p['rwkv_g1'][li].astype(BF16), g2=p['rwkv_g2'][li].astype(BF16),
              k_k=row(p['rwkv_k_k'][li]), k_a=row(p['rwkv_k_a'][li]))
    if li > 0:
        rp.update(mu_v=row(p['rwkv_mu_v'][li - 1]), v0=row(p['rwkv_v0'][li - 1]),
                  v1=p['rwkv_v1'][li - 1].astype(BF16), v2=p['rwkv_v2'][li - 1].astype(BF16))
    return rp


def _prepare(p):
    depth = p['g_mix'].shape[0]
    layers = []
    for i in range(depth):
        li = i // 2
        lay = dict(g_mix=p['g_mix'][i], g_ffn=p['g_ffn'][i],
                   w_up=p['ffn_w_up'][i].astype(BF16), w_down=p['ffn_w_down'][i].astype(BF16),
                   ffn_cw=p['ffn_conv_w'][i], ffn_cb=p['ffn_conv_b'][i])
        if i % 2 == 0:
            lay.update(w_in=p['w_in_ab'][li].astype(BF16), w_out=p['w_out_ab'][li].astype(BF16),
                       s5=_s5_params(p['s5_a_re'][li], p['s5_a_im'][li], p['s5_log_step'][li],
                                     p['s5_b_re'][li], p['s5_b_im'][li], p['s5_c_re'][li], p['s5_c_im'][li],
                                     p['s5_d'][li]),
                       w_glu=p['s5_w_glu'][li].astype(BF16), b_glu=p['s5_b_glu'][li])
        else:
            lay.update(w_in=p['w_in_cd'][li].astype(BF16), w_out=p['w_out_cd'][li].astype(BF16),
                       rwkv=_rwkv_params(p, li), r_k=p['rwkv_r_k'][li], ln_w=p['rwkv_ln_w'][li],
                       ln_b=p['rwkv_ln_b'][li], cd_w=p['conv_d_w'][li], cd_b=p['conv_d_b'][li],
                       cd_g=p['conv_d_ln_g'][li], cd_lb=p['conv_d_ln_b'][li])
        layers.append(lay)
    return layers


def _trunk_seq(x3, layers, g_final):
    batch, seq, d = x3.shape
    x = x3.reshape(batch * seq, d)
    rope = _rope_tables(seq, 0, True)
    attn_projs, s5r, s5i, wkvs, shifts, convs, ffns = [], [], [], [], [], [], []
    v_first = None
    for i, lay in enumerate(layers):
        if i % 2 == 0:
            proj = _norm_matmul(x, lay['g_mix'], lay['w_in'], rope=rope, rope_cols=(1, 2))
            a_out, hr, hi = _s5_seq(proj, batch, lay['s5'], lay['w_glu'], lay['b_glu'])
            b_out = _moba_seq(proj, batch)
            attn_projs.append(proj)
            s5r.append(hr)
            s5i.append(hi)
            x = _out_proj(a_out, b_out, lay['w_out'], x)
        else:
            proj = _norm_matmul(x, lay['g_mix'], lay['w_in'])
            r, lw, k, v, kk, b, g = _rwkv_pre(proj, None, lay['rwkv'], v_first, batch)
            if v_first is None:
                v_first = v
            c_out, s_fin = _wkv_seq(r, lw, k, v, kk, b, g, lay['r_k'], lay['ln_w'], lay['ln_b'], batch)
            d_out, cbuf = _convd_seq(proj, batch, lay['cd_w'], lay['cd_b'], lay['cd_g'], lay['cd_lb'])
            wkvs.append(s_fin)
            shifts.append(proj.reshape(batch, seq, -1)[:, -1, :4 * D_HALF])
            convs.append(cbuf)
            x = _out_proj(c_out, d_out, lay['w_out'], x)
        x, fbuf = _ffn_seq(x, batch, lay['g_ffn'], lay['w_up'], lay['ffn_cw'], lay['ffn_cb'], lay['w_down'])
        ffns.append(fbuf)
    y = _rmsnorm(x, g_final).reshape(batch, seq, d)
    assert seq % PAGE_SIZE == 0
    k_all, v_all = _kv_layout(attn_projs, paged=True)
    return (y, k_all, v_all, jnp.stack(s5r), jnp.stack(s5i), jnp.stack(wkvs),
            jnp.stack(shifts), jnp.stack(convs), jnp.stack(ffns))


def _trunk_step(x3, pos0, layers, g_final, cache_k, cache_v, page_table, s5_re0, s5_im0, wkv0, shift0,
                convd0, ffn0):
    bsz, _, d = x3.shape
    x = x3.reshape(bsz, d)
    rope = _rope_tables(bsz, pos0, False)
    cache_kt = jnp.transpose(cache_k, (0, 1, 3, 4, 2))
    cache_vt = jnp.transpose(cache_v, (0, 1, 3, 4, 2))
    attn_projs, s5r, s5i, wkvs, shifts, convs, ffns = [], [], [], [], [], [], []
    v_first = None
    for i, lay in enumerate(layers):
        li = i // 2
        if i % 2 == 0:
            proj = _norm_matmul(x, lay['g_mix'], lay['w_in'], rope=rope, rope_cols=(1, 2))
            a_out, hr, hi = _s5_step(proj, s5_re0[li].reshape(bsz, -1), s5_im0[li].reshape(bsz, -1),
                                     lay['s5'], lay['w_glu'], lay['b_glu'])
            b_out = _moba_step(proj, cache_kt, cache_vt, li, page_table)
            attn_projs.append(proj)
            s5r.append(hr)
            s5i.append(hi)
            x = _out_proj(a_out, b_out, lay['w_out'], x)
        else:
            proj = _norm_matmul(x, lay['g_mix'], lay['w_in'])
            r, lw, k, v, kk, b, g = _rwkv_pre(proj, shift0[li], lay['rwkv'], v_first, bsz)
            if v_first is None:
                v_first = v
            c_out, s_fin = _wkv_step(wkv0[li], r, lw, k, v, kk, b, g, lay['r_k'], lay['ln_w'], lay['ln_b'])
            d_out, cbuf = _convd_step(proj, convd0[li], lay['cd_w'], lay['cd_b'], lay['cd_g'], lay['cd_lb'])
            wkvs.append(s_fin)
            shifts.append(proj[:, :4 * D_HALF])
            convs.append(cbuf)
            x = _out_proj(c_out, d_out, lay['w_out'], x)
        x, fbuf = _ffn_step(x, ffn0[i], lay['g_ffn'], lay['w_up'], lay['ffn_cw'], lay['ffn_cb'], lay['w_down'])
        ffns.append(fbuf)
    y = _rmsnorm(x, g_final).reshape(bsz, 1, d)
    k_all, v_all = _kv_layout(attn_projs, paged=False)
    return (y, k_all, v_all, jnp.stack(s5r), jnp.stack(s5i), jnp.stack(wkvs),
            jnp.stack(shifts), jnp.stack(convs), jnp.stack(ffns))


def kernel(x_prompt, x_sample, cache_k_moba, cache_v_moba, page_table, state_s5_re, state_s5_im,
           state_rwkv_wkv, state_rwkv_shift, state_conv_d, state_ffn_conv, g_mix, g_ffn, g_final,
           w_in_ab, w_out_ab, s5_a_re, s5_a_im, s5_log_step, s5_b_re, s5_b_im, s5_c_re, s5_c_im, s5_d,
           s5_w_glu, s5_b_glu, w_in_cd, w_out_cd, rwkv_mu_rkv, rwkv_mu_z, rwkv_w0, rwkv_w1, rwkv_w2,
           rwkv_a0, rwkv_a1, rwkv_a2, rwkv_g1, rwkv_g2, rwkv_k_k, rwkv_k_a, rwkv_r_k, rwkv_ln_w, rwkv_ln_b,
           rwkv_mu_v, rwkv_v0, rwkv_v1, rwkv_v2, conv_d_w, conv_d_b, conv_d_ln_g, conv_d_ln_b,
           ffn_w_up, ffn_conv_w, ffn_conv_b, ffn_w_down):
    p = dict(g_mix=g_mix, g_ffn=g_ffn, w_in_ab=w_in_ab, w_out_ab=w_out_ab,
             s5_a_re=s5_a_re, s5_a_im=s5_a_im, s5_log_step=s5_log_step, s5_b_re=s5_b_re, s5_b_im=s5_b_im,
             s5_c_re=s5_c_re, s5_c_im=s5_c_im, s5_d=s5_d, s5_w_glu=s5_w_glu, s5_b_glu=s5_b_glu,
             w_in_cd=w_in_cd, w_out_cd=w_out_cd, rwkv_mu_rkv=rwkv_mu_rkv, rwkv_mu_z=rwkv_mu_z,
             rwkv_w0=rwkv_w0, rwkv_w1=rwkv_w1, rwkv_w2=rwkv_w2, rwkv_a0=rwkv_a0, rwkv_a1=rwkv_a1,
             rwkv_a2=rwkv_a2, rwkv_g1=rwkv_g1, rwkv_g2=rwkv_g2, rwkv_k_k=rwkv_k_k, rwkv_k_a=rwkv_k_a,
             rwkv_r_k=rwkv_r_k.reshape(rwkv_r_k.shape[0], -1), rwkv_ln_w=rwkv_ln_w, rwkv_ln_b=rwkv_ln_b,
             rwkv_mu_v=rwkv_mu_v, rwkv_v0=rwkv_v0, rwkv_v1=rwkv_v1, rwkv_v2=rwkv_v2,
             conv_d_w=conv_d_w, conv_d_b=conv_d_b, conv_d_ln_g=conv_d_ln_g, conv_d_ln_b=conv_d_ln_b,
             ffn_w_up=ffn_w_up, ffn_conv_w=ffn_conv_w, ffn_conv_b=ffn_conv_b, ffn_w_down=ffn_w_down)
    layers = _prepare(p)
    bp, seq, _ = x_prompt.shape
    bs = x_sample.shape[0]
    n_ab = w_in_ab.shape[0]
    n_cd = w_in_cd.shape[0]
    page = cache_k_moba.shape[2]
    past_len = page_table.shape[1] * page
    (y_p, k_p, v_p, s5r_p, s5i_p, wkv_p, sh_p, cd_p, ff_p) = _trunk_seq(x_prompt, layers, g_final)
    (y_s, k_s, v_s, s5r_s, s5i_s, wkv_s, sh_s, cd_s, ff_s) = _trunk_step(
        x_sample, past_len, layers, g_final, cache_k_moba, cache_v_moba, page_table, state_s5_re, state_s5_im,
        state_rwkv_wkv, state_rwkv_shift, state_conv_d, state_ffn_conv)
    kv_s = (n_ab, bs, 1, HEADS, HEAD_DIM)
    s5_p = (n_ab, bp, S5_GROUPS, S5_STATE)
    s5_s = (n_ab, bs, S5_GROUPS, S5_STATE)

    def pages_out(t):
        t = t.reshape(n_ab, bp, seq // page, HEADS, HEAD_DIM, page)
        return jnp.transpose(t, (0, 1, 2, 5, 3, 4))

    return (y_p, y_s, pages_out(k_p), pages_out(v_p), k_s.reshape(kv_s), v_s.reshape(kv_s),
            s5r_p.reshape(s5_p), s5i_p.reshape(s5_p), s5r_s.reshape(s5_s), s5i_s.reshape(s5_s),
            wkv_p, wkv_s, sh_p, sh_s, cd_p, cd_s, ff_p, ff_s)
```
